```python
import math
import jax
import jax.numpy as jnp
from jax import lax
import numpy as np

D_MODEL = 2048
BATCH = 2
SEQ = 4096
DEPTH = 2
DEC_BATCH = 128
DEC_SEQ = 8
PAST_LEN = 2048
PAGE_SIZE = 128

HEAD_DIM = 64
W_BRANCH = D_MODEL // 4
H_FOX = W_BRANCH // HEAD_DIM
H_DIFF = W_BRANCH // (2 * HEAD_DIM)
DIFF_VD = 2 * HEAD_DIM
W_LRU = W_BRANCH
LRU_BLOCKS = W_LRU // HEAD_DIM
LRU_BS = W_LRU // LRU_BLOCKS
CONV_W = 4
LRU_C = 8.0
H_RWKV = W_BRANCH // HEAD_DIM
RWKV_DECAY_RANK = 64
RWKV_A_RANK = 64
RWKV_G_RANK = 128
W_RWKV_PROJ = 3 * W_BRANCH + RWKV_DECAY_RANK + RWKV_A_RANK + RWKV_G_RANK
N_BRANCH = 4
D_FF = 4 * D_MODEL
NUM_BUCKETS = 32
MAX_DISTANCE = 128
Q_BLOCK = 128
RMS_EPS = 1e-6
SUBLN_EPS = 1e-5
GN_EPS = 64e-5
NEG_INF = -1e30
IN_SPLITS = (W_BRANCH, W_BRANCH, W_BRANCH, H_FOX,
             W_BRANCH, W_BRANCH, W_BRANCH,
             W_LRU, W_LRU,
             W_RWKV_PROJ,
             N_BRANCH * D_MODEL)
RWKV_SPLITS = (W_BRANCH, W_BRANCH, W_BRANCH, RWKV_DECAY_RANK, RWKV_A_RANK, RWKV_G_RANK)
N_IN = sum(IN_SPLITS)

kernel_name = 'hybrid_fox_diff_lru_rwkv7_step'


def _split_points(sizes):
    return [int(c) for c in np.cumsum(np.asarray(sizes))[:-1]]


def rms_norm(x, g, eps=RMS_EPS):
    xf = x.astype(jnp.float32)
    y = xf * lax.rsqrt(jnp.mean(xf * xf, axis=-1, keepdims=True) + eps)
    return (y * g.astype(jnp.float32)).astype(x.dtype)


def t5_bias(table, q_pos, k_pos):
    n = jnp.maximum(q_pos[:, None] - k_pos[None, :], 0)
    max_exact = NUM_BUCKETS // 2
    nf = jnp.maximum(n, 1).astype(jnp.float32)
    large = max_exact + (jnp.log(nf / max_exact) / math.log(MAX_DISTANCE / max_exact)
                         * (NUM_BUCKETS - max_exact)).astype(jnp.int32)
    bucket = jnp.where(n < max_exact, n, jnp.minimum(large, NUM_BUCKETS - 1))
    return jnp.transpose(table[bucket].astype(jnp.float32), (2, 0, 1))


def fox_attend(q, k, v, cq, ck, q_pos, k_pos):
    s = jnp.einsum('bqhd,bkhd->bhqk', q, k, preferred_element_type=jnp.float32) * (HEAD_DIM ** -0.5)
    s = s + jnp.swapaxes(cq, 1, 2)[..., :, None] - jnp.swapaxes(ck, 1, 2)[..., None, :]
    s = jnp.where(k_pos[None, :] <= q_pos[:, None], s, NEG_INF)
    p = jax.nn.softmax(s, axis=-1)
    return jnp.einsum('bhqk,bkhd->bqhd', p.astype(v.dtype), v)


def diff_attend(q, k, v, lam, bias, q_pos, k_pos, subln_g, lam_init):
    mask = k_pos[None, :] <= q_pos[:, None]

    def smap(qa, ka, b):
        s = jnp.einsum('bqhd,bkhd->bhqk', qa, ka, preferred_element_type=jnp.float32) * (HEAD_DIM ** -0.5) + b[None]
        return jax.nn.softmax(jnp.where(mask, s, NEG_INF), axis=-1)

    p = (smap(q[..., :HEAD_DIM], k[..., :HEAD_DIM], bias[:H_DIFF])
         - lam * smap(q[..., HEAD_DIM:], k[..., HEAD_DIM:], bias[H_DIFF:]))
    o = jnp.einsum('bhqk,bkhd->bqhd', p.astype(v.dtype), v)
    return rms_norm(o, subln_g, SUBLN_EPS) * (1.0 - lam_init)


def _blocks(t):
    return (t // Q_BLOCK, Q_BLOCK) if t % Q_BLOCK == 0 else (1, t)


def _sweep(fn, nb):
    out = lax.map(fn, jnp.arange(nb))
    out = jnp.moveaxis(out, 0, 1)
    return out.reshape((out.shape[0], out.shape[1] * out.shape[2]) + out.shape[3:])


def _lin_combine(e1, e2):
    a1, b1 = e1
    a2, b2 = e2
    return a1 * a2, a2 * b1 + b2


def rwkv_scan(s0, r, w, k, v, a, b):
    def step(s, inp):
        r_t, w_t, k_t, v_t, a_t, b_t = inp
        sa = jnp.einsum('bhvk,bhk->bhv', s, a_t)
        s = s * w_t[:, :, None, :] + sa[..., None] * b_t[:, :, None, :] + v_t[..., None] * k_t[:, :, None, :]
        return s, jnp.einsum('bhvk,bhk->bhv', s, r_t)
    xs = tuple(jnp.moveaxis(z, 1, 0) for z in (r, w, k, v, a, b))
    s, ys = lax.scan(step, s0.astype(jnp.float32), xs)
    return s, jnp.moveaxis(ys, 0, 1)


def _gather_pages(pool_l, page_table):
    g = pool_l[page_table]
    return g.reshape((g.shape[0], g.shape[1] * g.shape[2]) + g.shape[3:])


def layer_forward(x, past, lp, lam_init, t5_table):
    fox_k0, fox_v0, fox_lf0, diff_k0, diff_v0, conv0, h0, shift0, wkv0 = past
    f32 = jnp.float32
    dt = x.dtype
    B, T, _ = x.shape
    P = fox_k0.shape[1]
    k_pos = jnp.arange(P + T)
    nb, qb = _blocks(T)

    xn = rms_norm(x, lp['norm1_g'])
    proj = xn @ lp['w_in']
    (fq, fk, fv, ff, dq, dk, dv, lx, lg, rw, gate_pre) = jnp.split(proj, _split_points(IN_SPLITS), axis=-1)

    fq = fq.reshape(B, T, H_FOX, HEAD_DIM)
    fk = fk.reshape(B, T, H_FOX, HEAD_DIM)
    fv = fv.reshape(B, T, H_FOX, HEAD_DIM)
    logf = jax.nn.log_sigmoid(ff.astype(f32) + lp['fox_fb'].astype(f32))
    fk_all = jnp.concatenate([fox_k0.astype(dt), fk], axis=1)
    fv_all = jnp.concatenate([fox_v0.astype(dt), fv], axis=1)
    c_all = jnp.cumsum(jnp.concatenate([fox_lf0.astype(f32), logf], axis=1), axis=1)
    c_new = c_all[:, P:]

    def fox_block(i):
        s0 = i * qb
        return fox_attend(lax.dynamic_slice_in_dim(fq, s0, qb, 1), fk_all, fv_all,
                          lax.dynamic_slice_in_dim(c_new, s0, qb, 1), c_all,
                          P + s0 + jnp.arange(qb), k_pos)
    o_fox = _sweep(fox_block, nb).reshape(B, T, W_BRANCH)

    dq = dq.reshape(B, T, H_DIFF, 2 * HEAD_DIM)
    dk = dk.reshape(B, T, H_DIFF, 2 * HEAD_DIM)
    dv = dv.reshape(B, T, H_DIFF, DIFF_VD)
    dk_all = jnp.concatenate([diff_k0.astype(dt), dk], axis=1)
    dv_all = jnp.concatenate([diff_v0.astype(dt), dv], axis=1)
    lam = (jnp.exp(jnp.sum(lp['lam_q1'].astype(f32) * lp['lam_k1'].astype(f32)))
           - jnp.exp(jnp.sum(lp['lam_q2'].astype(f32) * lp['lam_k2'].astype(f32))) + lam_init)

    def diff_block(i):
        s0 = i * qb
        q_pos = P + s0 + jnp.arange(qb)
        return diff_attend(lax.dynamic_slice_in_dim(dq, s0, qb, 1), dk_all, dv_all, lam,
                           t5_bias(t5_table, q_pos, k_pos), q_pos, k_pos, lp['diff_subln_g'], lam_init)
    o_diff = _sweep(diff_block, nb).reshape(B, T, W_BRANCH)

    conv_in = jnp.concatenate([conv0.astype(dt), lx], axis=1)
    xc = lp['lru_conv_b'] + sum(lp['lru_conv_w'][j] * conv_in[:, j:j + T] for j in range(CONV_W))
    xb = xc.reshape(B, T, LRU_BLOCKS, LRU_BS)
    r_gate = jax.nn.sigmoid((jnp.einsum('btni,nij->btnj', xb, lp['lru_wa']).reshape(B, T, W_LRU) + lp['lru_ba']).astype(f32))
    i_gate = jax.nn.sigmoid((jnp.einsum('btni,nij->btnj', xb, lp['lru_wx']).reshape(B, T, W_LRU) + lp['lru_bx']).astype(f32))
    log_a = -LRU_C * r_gate * jax.nn.softplus(-lp['lru_lambda'].astype(f32))
    a_t = jnp.exp(log_a)
    u_t = jnp.sqrt(-jnp.expm1(2.0 * log_a)) * (i_gate * xc.astype(f32))
    u_t = u_t.at[:, 0].add(a_t[:, 0] * h0.astype(f32))
    _, h = lax.associative_scan(_lin_combine, (a_t, u_t), axis=1)
    o_lru = h.astype(dt) * jax.nn.gelu(lg)

    prev = jnp.concatenate([shift0[:, None, :].astype(dt), rw[:, :-1]], axis=1)
    rx = rw + (prev - rw) * lp['rwkv_mu']
    rr, rk, rv, wl, al, gl = jnp.split(rx, _split_points(RWKV_SPLITS), axis=-1)
    wd = -jax.nn.softplus(-(lp['rwkv_w0'] + jnp.tanh(wl) @ lp['rwkv_w2']).astype(f32)) - 0.5
    decay = jnp.exp(-jnp.exp(wd))
    aa = jax.nn.sigmoid((lp['rwkv_a0'] + al @ lp['rwkv_a2']).astype(f32))
    gg = jax.nn.sigmoid(gl) @ lp['rwkv_g2']
    hs = (B, T, H_RWKV, HEAD_DIM)
    kk = (rk * lp['rwkv_k_k']).astype(f32).reshape(hs)
    kk = kk / jnp.maximum(jnp.linalg.norm(kk, axis=-1, keepdims=True), 1e-12)
    kh = (rk.astype(f32) * (1.0 + (aa - 1.0) * lp['rwkv_k_a'].astype(f32))).reshape(hs)
    rh = rr.astype(f32).reshape(hs)
    vh = rv.astype(f32).reshape(hs)
    ah = aa.reshape(hs)
    wkv, y = rwkv_scan(wkv0, rh, decay.reshape(hs), kh, vh, -kk, kk * ah)
    mu_y = jnp.mean(y, axis=-1, keepdims=True)
    var_y = jnp.mean(jnp.square(y - mu_y), axis=-1, keepdims=True)
    y = (((y - mu_y) * lax.rsqrt(var_y + GN_EPS)).reshape(B, T, W_BRANCH) * lp['rwkv_lnx_g'].astype(f32)
         + lp['rwkv_lnx_b'].astype(f32))
    bonus = (jnp.sum(rh * kh * lp['rwkv_r_k'].astype(f32), axis=-1, keepdims=True) * vh).reshape(B, T, W_BRANCH)
    o_rwkv = (y + bonus).astype(dt) * gg

    g_fox, g_diff, g_lru, g_rwkv = jnp.split(jax.nn.sigmoid(gate_pre), N_BRANCH, axis=-1)
    mixed = (g_fox * (o_fox @ lp['wb_fox']) + g_diff * (o_diff @ lp['wb_diff'])
             + g_lru * (o_lru @ lp['wb_lru']) + g_rwkv * (o_rwkv @ lp['wb_rwkv']))
    x = x + mixed @ lp['w_out']

    hn = rms_norm(x, lp['norm2_g'])
    x = x + jnp.square(jax.nn.relu(hn @ lp['w_up'])) @ lp['w_down']

    new_state = (fk, fv, logf.astype(dt), dk, dv, conv_in[:, -(CONV_W - 1):],
                 h[:, -1].astype(dt), rw[:, -1], wkv.astype(dt))
    return x, new_state


def setup_inputs(seed: int = 0) -> dict:
    key = jax.random.key(seed)
    keys = iter(jax.random.split(key, 64))

    def nrm(shape, scale):
        return jax.random.normal(next(keys), shape, jnp.float32) * scale

    def unif(shape, lo, hi):
        return jax.random.uniform(next(keys), shape, jnp.float32, lo, hi)

    n_pages = PAST_LEN // PAGE_SIZE
    n_used = DEC_BATCH * n_pages
    n_pool = n_used + max(1, n_used // 4)
    L = DEPTH
    inp = {}
    inp['x_prompt'] = nrm((BATCH, SEQ, D_MODEL), 1.0)
    inp['x_sample'] = nrm((DEC_BATCH, DEC_SEQ, D_MODEL), 1.0)
    inp['cache_fox_k'] = nrm((L, n_pool, PAGE_SIZE, H_FOX, HEAD_DIM), 1.0)
    inp['cache_fox_v'] = nrm((L, n_pool, PAGE_SIZE, H_FOX, HEAD_DIM), 1.0)
    inp['cache_fox_logf'] = jax.nn.log_sigmoid(2.0 + nrm((L, n_pool, PAGE_SIZE, H_FOX), 1.0))
    inp['cache_diff_k'] = nrm((L, n_pool, PAGE_SIZE, H_DIFF, 2 * HEAD_DIM), 1.0)
    inp['cache_diff_v'] = nrm((L, n_pool, PAGE_SIZE, H_DIFF, DIFF_VD), 1.0)
    inp['state_lru_conv'] = nrm((L, DEC_BATCH, CONV_W - 1, W_LRU), 1.0)
    inp['state_lru_h'] = nrm((L, DEC_BATCH, W_LRU), 0.5)
    inp['state_rwkv_shift'] = nrm((L, DEC_BATCH, W_RWKV_PROJ), 1.0)
    inp['state_rwkv_wkv'] = nrm((L, DEC_BATCH, H_RWKV, HEAD_DIM, HEAD_DIM), 0.3)
    inp['page_table'] = jax.random.permutation(next(keys), n_pool)[:n_used].reshape(DEC_BATCH, n_pages).astype(jnp.int32)
    inp['norm1_g'] = 1.0 + nrm((L, D_MODEL), 0.02)
    inp['w_in'] = nrm((L, D_MODEL, N_IN), D_MODEL ** -0.5)
    inp['fox_fb'] = 2.0 + nrm((L, H_FOX), 0.1)
    inp['lam_q1'] = nrm((L, HEAD_DIM), 0.1)
    inp['lam_k1'] = nrm((L, HEAD_DIM), 0.1)
    inp['lam_q2'] = nrm((L, HEAD_DIM), 0.1)
    inp['lam_k2'] = nrm((L, HEAD_DIM), 0.1)
    inp['diff_subln_g'] = 1.0 + nrm((L, DIFF_VD), 0.02)
    inp['t5_table'] = nrm((NUM_BUCKETS, 2 * H_DIFF), 0.5)
    inp['lru_conv_w'] = nrm((L, CONV_W, W_LRU), CONV_W ** -0.5)
    inp['lru_conv_b'] = nrm((L, W_LRU), 0.01)
    inp['lru_wa'] = nrm((L, LRU_BLOCKS, LRU_BS, LRU_BS), LRU_BS ** -0.5)
    inp['lru_ba'] = nrm((L, W_LRU), 0.01)
    inp['lru_wx'] = nrm((L, LRU_BLOCKS, LRU_BS, LRU_BS), LRU_BS ** -0.5)
    inp['lru_bx'] = nrm((L, W_LRU), 0.01)
    a_base = unif((L, W_LRU), 0.9, 0.999) ** (1.0 / LRU_C)
    inp['lru_lambda'] = jnp.log(a_base) - jnp.log1p(-a_base)
    inp['rwkv_mu'] = unif((L, W_RWKV_PROJ), 0.0, 1.0)
    inp['rwkv_w0'] = unif((L, W_BRANCH), -4.0, 0.0)
    inp['rwkv_w2'] = nrm((L, RWKV_DECAY_RANK, W_BRANCH), 0.1)
    inp['rwkv_a0'] = nrm((L, W_BRANCH), 0.1)
    inp['rwkv_a2'] = nrm((L, RWKV_A_RANK, W_BRANCH), 0.1)
    inp['rwkv_g2'] = nrm((L, RWKV_G_RANK, W_BRANCH), RWKV_G_RANK ** -0.5)
    inp['rwkv_k_k'] = 0.85 + nrm((L, W_BRANCH), 0.02)
    inp['rwkv_k_a'] = 1.0 + nrm((L, W_BRANCH), 0.02)
    inp['rwkv_r_k'] = nrm((L, H_RWKV, HEAD_DIM), 0.1)
    inp['rwkv_lnx_g'] = 1.0 + nrm((L, W_BRANCH), 0.02)
    inp['rwkv_lnx_b'] = nrm((L, W_BRANCH), 0.01)
    inp['wb_fox'] = nrm((L, W_BRANCH, D_MODEL), W_BRANCH ** -0.5)
    inp['wb_diff'] = nrm((L, W_BRANCH, D_MODEL), W_BRANCH ** -0.5)
    inp['wb_lru'] = nrm((L, W_BRANCH, D_MODEL), W_BRANCH ** -0.5)
    inp['wb_rwkv'] = nrm((L, W_BRANCH, D_MODEL), W_BRANCH ** -0.5)
    inp['w_out'] = nrm((L, D_MODEL, D_MODEL), D_MODEL ** -0.5)
    inp['norm2_g'] = 1.0 + nrm((L, D_MODEL), 0.02)
    inp['w_up'] = nrm((L, D_MODEL, D_FF), D_MODEL ** -0.5)
    inp['w_down'] = nrm((L, D_FF, D_MODEL), D_FF ** -0.5)
    inp['final_g'] = 1.0 + nrm((D_MODEL,), 0.02)
    return inp


def reference(x_prompt, x_sample, cache_fox_k, cache_fox_v, cache_fox_logf, cache_diff_k, cache_diff_v,
              state_lru_conv, state_lru_h, state_rwkv_shift, state_rwkv_wkv, page_table,
              norm1_g, w_in, fox_fb, lam_q1, lam_k1, lam_q2, lam_k2, diff_subln_g, t5_table,
              lru_conv_w, lru_conv_b, lru_wa, lru_ba, lru_wx, lru_bx, lru_lambda,
              rwkv_mu, rwkv_w0, rwkv_w2, rwkv_a0, rwkv_a2, rwkv_g2, rwkv_k_k, rwkv_k_a, rwkv_r_k,
              rwkv_lnx_g, rwkv_lnx_b, wb_fox, wb_diff, wb_lru, wb_rwkv, w_out,
              norm2_g, w_up, w_down, final_g):
    lps = [dict(norm1_g=norm1_g[l], w_in=w_in[l], fox_fb=fox_fb[l],
                lam_q1=lam_q1[l], lam_k1=lam_k1[l], lam_q2=lam_q2[l], lam_k2=lam_k2[l],
                diff_subln_g=diff_subln_g[l],
                lru_conv_w=lru_conv_w[l], lru_conv_b=lru_conv_b[l], lru_wa=lru_wa[l], lru_ba=lru_ba[l],
                lru_wx=lru_wx[l], lru_bx=lru_bx[l], lru_lambda=lru_lambda[l],
                rwkv_mu=rwkv_mu[l], rwkv_w0=rwkv_w0[l], rwkv_w2=rwkv_w2[l], rwkv_a0=rwkv_a0[l],
                rwkv_a2=rwkv_a2[l], rwkv_g2=rwkv_g2[l], rwkv_k_k=rwkv_k_k[l], rwkv_k_a=rwkv_k_a[l],
                rwkv_r_k=rwkv_r_k[l], rwkv_lnx_g=rwkv_lnx_g[l], rwkv_lnx_b=rwkv_lnx_b[l],
                wb_fox=wb_fox[l], wb_diff=wb_diff[l], wb_lru=wb_lru[l], wb_rwkv=wb_rwkv[l],
                w_out=w_out[l], norm2_g=norm2_g[l], w_up=w_up[l], w_down=w_down[l])
           for l in range(DEPTH)]
    lam_inits = [0.8 - 0.6 * math.exp(-0.3 * l) for l in range(DEPTH)]

    bp = x_prompt.shape[0]
    dt = x_prompt.dtype
    empty = (jnp.zeros((bp, 0, H_FOX, HEAD_DIM), dt), jnp.zeros((bp, 0, H_FOX, HEAD_DIM), dt),
             jnp.zeros((bp, 0, H_FOX), dt), jnp.zeros((bp, 0, H_DIFF, 2 * HEAD_DIM), dt),
             jnp.zeros((bp, 0, H_DIFF, DIFF_VD), dt), jnp.zeros((bp, CONV_W - 1, W_LRU), dt),
             jnp.zeros((bp, W_LRU), dt), jnp.zeros((bp, W_RWKV_PROJ), dt),
             jnp.zeros((bp, H_RWKV, HEAD_DIM, HEAD_DIM), dt))
    xp = x_prompt
    p_states = []
    for l in range(DEPTH):
        xp, st = layer_forward(xp, empty, lps[l], lam_inits[l], t5_table)
        p_states.append(st)
    y_prompt = rms_norm(xp, final_g)

    xs = x_sample
    s_states = []
    for l in range(DEPTH):
        past = (_gather_pages(cache_fox_k[l], page_table), _gather_pages(cache_fox_v[l], page_table),
                _gather_pages(cache_fox_logf[l], page_table), _gather_pages(cache_diff_k[l], page_table),
                _gather_pages(cache_diff_v[l], page_table), state_lru_conv[l], state_lru_h[l],
                state_rwkv_shift[l], state_rwkv_wkv[l])
        xs, st = layer_forward(xs, past, lps[l], lam_inits[l], t5_table)
        s_states.append(st)
    y_sample = rms_norm(xs, final_g)

    (p_fox_k, p_fox_v, p_fox_logf, p_diff_k, p_diff_v, p_lru_conv, p_lru_h, p_rwkv_shift,
     p_rwkv_wkv) = [jnp.stack(z) for z in zip(*p_states)]
    (s_fox_k, s_fox_v, s_fox_logf, s_diff_k, s_diff_v, s_lru_conv, s_lru_h, s_rwkv_shift,
     s_rwkv_wkv) = [jnp.stack(z) for z in zip(*s_states)]
    return (y_prompt, y_sample,
            p_fox_k, p_fox_v, p_fox_logf, p_diff_k, p_diff_v, p_lru_conv, p_lru_h, p_rwkv_shift, p_rwkv_wkv,
            s_fox_k, s_fox_v, s_fox_logf, s_diff_k, s_diff_v, s_lru_conv, s_lru_h, s_rwkv_shift, s_rwkv_wkv)
```

```python
import functools
import math

import jax
import jax.numpy as jnp
from jax import lax
from jax.experimental import pallas as pl
from jax.experimental.pallas import tpu as pltpu

F32 = jnp.float32
BF16 = jnp.bfloat16

HEAD_DIM = 64
W_BRANCH = 512
H_FOX = 8
H_DIFF = 4
H_RWKV = 8
CONV_W = 4
LRU_C = 8.0
NUM_BUCKETS = 32
MAX_DISTANCE = 128
RMS_EPS = 1e-6
SUBLN_EPS = 1e-5
GN_EPS = 64e-5
NEG_INF = -1e30
QK_SCALE = HEAD_DIM ** -0.5

LANES = 128
SUBLANES = 8
VMEM_LIMIT_CAP = 56 * 1024 * 1024

OFF_FQ, OFF_FK, OFF_FV = 0, 512, 1024
OFF_DQ, OFF_DK, OFF_DV = 1536, 2048, 2560
OFF_LX, OFF_LG = 3072, 3584
OFF_RW = 4096
W_RWKV_PROJ = 1792
OFF_FF = 5888
OFF_GATE = 6144
FF_PAD = OFF_GATE - OFF_FF - H_FOX


def _tile(n, pref, mult):
    best = None
    t = mult
    while t <= min(n, pref):
        if n % t == 0:
            best = t
        t += mult
    return best if best is not None else n


def _cparams(n_axes, vmem_mb):
    return pltpu.CompilerParams(
        dimension_semantics=("arbitrary",) * n_axes,
        vmem_limit_bytes=min(int(vmem_mb * 1024 * 1024), VMEM_LIMIT_CAP))


def _softplus(x):
    return jnp.maximum(x, 0.0) + jnp.log1p(jnp.exp(-jnp.abs(x)))


def _split3(x):
    def top(v):
        bits = lax.bitcast_convert_type(v, jnp.uint32) & jnp.uint32(0xFFFF0000)
        return lax.bitcast_convert_type(bits, F32)

    x1 = top(x)
    r1 = x - x1
    x2 = top(r1)
    x3 = r1 - x2
    return x1.astype(BF16), x2.astype(BF16), x3.astype(BF16)


def _dot_sel_rhs(x, sel):
    x1, x2, x3 = _split3(x)
    d = functools.partial(jnp.dot, preferred_element_type=F32)
    return d(x1, sel) + d(x2, sel) + d(x3, sel)


def _dot_sel_lhs(sel, x):
    x1, x2, x3 = _split3(x)
    d = functools.partial(jnp.dot, preferred_element_type=F32)
    return d(sel, x1) + d(sel, x2) + d(sel, x3)


def _dot_nt(a, b):
    return lax.dot_general(a, b, (((1,), (1,)), ((), ())), preferred_element_type=F32)


def _rms_matmul_body(x_ref, g_ref, w_ref, o_ref, xn_ref):
    @pl.when(pl.program_id(1) == 0)
    def _():
        x = x_ref[...]
        ms = jnp.mean(x * x, axis=-1, keepdims=True)
        xn_ref[...] = (x * lax.rsqrt(ms + RMS_EPS) * g_ref[...]).astype(BF16)

    o_ref[...] = jnp.dot(xn_ref[...], w_ref[...], preferred_element_type=F32)


def _rms_matmul(x, g, w):
    m, k = x.shape
    n = w.shape[1]
    tm = _tile(m, 1024, SUBLANES)
    tn = _tile(n, 1024, LANES)
    vmem = (2 * tm * k * 4 + 2 * k * tn * 2 + 2 * tm * tn * 4 + tm * k * 2) / 2**20 + 8
    return pl.pallas_call(
        _rms_matmul_body,
        grid=(m // tm, n // tn),
        in_specs=[pl.BlockSpec((tm, k), lambda i, j: (i, 0)),
                  pl.BlockSpec((1, k), lambda i, j: (0, 0)),
                  pl.BlockSpec((k, tn), lambda i, j: (0, j))],
        out_specs=pl.BlockSpec((tm, tn), lambda i, j: (i, j)),
        out_shape=jax.ShapeDtypeStruct((m, n), F32),
        scratch_shapes=[pltpu.VMEM((tm, k), BF16)],
        compiler_params=_cparams(2, vmem),
        name="in_proj",
    )(x, g, w)


def _merge_body(of_ref, od_ref, ol_ref, or_ref, wf_ref, wd_ref, wl_ref, wr_ref,
                gf_ref, gd_ref, gl_ref, gr_ref, o_ref):
    def branch(o_r, w_r, g_r):
        y = jnp.dot(o_r[...].astype(BF16), w_r[...], preferred_element_type=F32)
        return jax.nn.sigmoid(g_r[...]) * y

    acc = branch(of_ref, wf_ref, gf_ref)
    acc = acc + branch(od_ref, wd_ref, gd_ref)
    acc = acc + branch(ol_ref, wl_ref, gl_ref)
    acc = acc + branch(or_ref, wr_ref, gr_ref)
    o_ref[...] = acc.astype(BF16)


def _merge(outs, wbs, proj, d_model):
    m, w = outs[0].shape
    tm = _tile(m, 512, 16)
    tn = _tile(d_model, 512, LANES)
    o_specs = [pl.BlockSpec((tm, w), lambda i, j: (i, 0)) for _ in range(4)]
    w_specs = [pl.BlockSpec((w, tn), lambda i, j: (0, j)) for _ in range(4)]
    g_specs = [pl.BlockSpec((tm, tn), functools.partial(
        lambda i, j, base: (i, base + j), base=(OFF_GATE + b * d_model) // tn)) for b in range(4)]
    return pl.pallas_call(
        _merge_body,
        grid=(m // tm, d_model // tn),
        in_specs=o_specs + w_specs + g_specs,
        out_specs=pl.BlockSpec((tm, tn), lambda i, j: (i, j)),
        out_shape=jax.ShapeDtypeStruct((m, d_model), BF16),
        compiler_params=_cparams(2, 40),
        name="merge",
    )(*outs, *wbs, proj, proj, proj, proj)


def _matmul_res_body(a_ref, w_ref, r_ref, o_ref):
    o_ref[...] = r_ref[...] + jnp.dot(a_ref[...], w_ref[...], preferred_element_type=F32)


def _matmul_res(a, w, res):
    m, k = a.shape
    n = w.shape[1]
    tm = _tile(m, 512, 16)
    tn = _tile(n, 1024, LANES)
    return pl.pallas_call(
        _matmul_res_body,
        grid=(m // tm, n // tn),
        in_specs=[pl.BlockSpec((tm, k), lambda i, j: (i, 0)),
                  pl.BlockSpec((k, tn), lambda i, j: (0, j)),
                  pl.BlockSpec((tm, tn), lambda i, j: (i, j))],
        out_specs=pl.BlockSpec((tm, tn), lambda i, j: (i, j)),
        out_shape=jax.ShapeDtypeStruct((m, n), F32),
        compiler_params=_cparams(2, 40),
        name="out_proj",
    )(a, w, res)


def _mlp_body(x_ref, g_ref, wu_ref, wd_ref, fg_ref, o_ref, y_ref, xn_ref, acc_ref, *, n_chunks):
    c = pl.program_id(1)

    @pl.when(c == 0)
    def _():
        x = x_ref[...]
        ms = jnp.mean(x * x, axis=-1, keepdims=True)
        xn_ref[...] = (x * lax.rsqrt(ms + RMS_EPS) * g_ref[...]).astype(BF16)
        acc_ref[...] = jnp.zeros_like(acc_ref)

    h = jnp.dot(xn_ref[...], wu_ref[...], preferred_element_type=F32)
    h = jnp.square(jnp.maximum(h, 0.0))
    acc_ref[...] += jnp.dot(h.astype(BF16), wd_ref[...], preferred_element_type=F32)

    @pl.when(c == n_chunks - 1)
    def _():
        xo = x_ref[...] + acc_ref[...]
        o_ref[...] = xo
        ms = jnp.mean(xo * xo, axis=-1, keepdims=True)
        y_ref[...] = xo * lax.rsqrt(ms + RMS_EPS) * fg_ref[...]


def _mlp(x, g, wu, wd, final_g):
    m, d = x.shape
    f = wu.shape[1]
    tm = _tile(m, 512, SUBLANES)
    tc = _tile(f, 512, LANES)
    n_chunks = f // tc
    vmem = (2 * tm * d * 4 + 4 * d * tc * 2 + 4 * tm * d * 4 + tm * d * 2 + tm * d * 4) / 2**20 + 10
    return pl.pallas_call(
        functools.partial(_mlp_body, n_chunks=n_chunks),
        grid=(m // tm, n_chunks),
        in_specs=[pl.BlockSpec((tm, d), lambda i, c: (i, 0)),
                  pl.BlockSpec((1, d), lambda i, c: (0, 0)),
                  pl.BlockSpec((d, tc), lambda i, c: (0, c)),
                  pl.BlockSpec((tc, d), lambda i, c: (c, 0)),
                  pl.BlockSpec((1, d), lambda i, c: (0, 0))],
        out_specs=[pl.BlockSpec((tm, d), lambda i, c: (i, 0)),
                   pl.BlockSpec((tm, d), lambda i, c: (i, 0))],
        out_shape=[jax.ShapeDtypeStruct((m, d), F32), jax.ShapeDtypeStruct((m, d), F32)],
        scratch_shapes=[pltpu.VMEM((tm, d), BF16), pltpu.VMEM((tm, d), F32)],
        compiler_params=_cparams(2, vmem),
        name="mlp",
    )(x, g, wu, wd, final_g)


def _cumsum_rows(x, tc):
    if tc >= LANES:
        r = lax.broadcasted_iota(jnp.int32, (tc, tc), 0)
        c = lax.broadcasted_iota(jnp.int32, (tc, tc), 1)
        tri = jnp.where(c <= r, 1.0, 0.0).astype(BF16)
        return _dot_sel_lhs(tri, x)
    rows = lax.broadcasted_iota(jnp.int32, x.shape, 0)
    parts = [jnp.sum(jnp.where(rows <= t, x, 0.0), axis=0, keepdims=True) for t in range(tc)]
    return jnp.concatenate(parts, axis=0)


def _logf_cumsum_body(ff_ref, fb_ref, c0_ref, lf_ref, c_ref, carry_ref, *, tc):
    @pl.when(pl.program_id(1) == 0)
    def _():
        carry_ref[...] = c0_ref[...]

    lf = -_softplus(-(ff_ref[...] + fb_ref[...]))
    lf_ref[...] = lf
    cs = _cumsum_rows(lf, tc) + carry_ref[...]
    c_ref[...] = cs
    carry_ref[...] = cs[tc - 1:tc, :]


def _logf_cumsum(ff, fb, c0):
    nseq, t, h = ff.shape
    tc = _tile(t, 512, SUBLANES)
    blk = pl.BlockSpec((None, tc, h), lambda s, c: (s, c, 0))
    return pl.pallas_call(
        functools.partial(_logf_cumsum_body, tc=tc),
        grid=(nseq, t // tc),
        in_specs=[blk, pl.BlockSpec((1, h), lambda s, c: (0, 0)),
                  pl.BlockSpec((None, 1, h), lambda s, c: (s, 0, 0))],
        out_specs=[blk, blk],
        out_shape=[jax.ShapeDtypeStruct((nseq, t, h), F32)] * 2,
        scratch_shapes=[pltpu.VMEM((1, h), F32)],
        compiler_params=_cparams(2, 24),
        name="logf_cumsum",
    )(ff, fb, c0)


def _paged_cumsum_body(pt_ref, lf_ref, c_ref, tot_ref, carry_ref, *, page):
    @pl.when(pl.program_id(1) == 0)
    def _():
        carry_ref[...] = jnp.zeros_like(carry_ref)

    cs = _cumsum_rows(lf_ref[...], page) + carry_ref[...]
    c_ref[...] = cs
    carry_ref[...] = cs[page - 1:page, :]
    tot_ref[...] = cs[page - 1:page, :]


def _paged_cumsum(cache_lf, page_table, layer):
    _, _, page, h = cache_lf.shape
    db, n_pages = page_table.shape
    grid_spec = pltpu.PrefetchScalarGridSpec(
        num_scalar_prefetch=1,
        grid=(db, n_pages),
        in_specs=[pl.BlockSpec((None, None, page, h), lambda b, p, pt: (layer, pt[b, p], 0, 0))],
        out_specs=[pl.BlockSpec((None, page, h), lambda b, p, pt: (b, p, 0)),
                   pl.BlockSpec((None, 1, h), lambda b, p, pt: (b, 0, 0))],
        scratch_shapes=[pltpu.VMEM((1, h), F32)])
    return pl.pallas_call(
        functools.partial(_paged_cumsum_body, page=page),
        grid_spec=grid_spec,
        out_shape=[jax.ShapeDtypeStruct((db, n_pages * page, h), F32),
                   jax.ShapeDtypeStruct((db, 1, h), F32)],
        compiler_params=_cparams(2, 16),
        name="paged_cumsum",
    )(page_table, cache_lf)


def _t5_bucket(n):
    max_exact = NUM_BUCKETS // 2
    nf = jnp.maximum(n, 1).astype(F32)
    large = max_exact + (jnp.log(nf / max_exact) / math.log(MAX_DISTANCE / max_exact)
                         * (NUM_BUCKETS - max_exact)).astype(jnp.int32)
    return jnp.where(n < max_exact, n, jnp.minimum(large, NUM_BUCKETS - 1))


def _t5_lookup(bucket, tab_ref, col):
    acc = jnp.zeros(bucket.shape, F32)
    for j in range(NUM_BUCKETS):
        acc = jnp.where(bucket == j, tab_ref[j, col], acc)
    return acc


def _softmax_update(s, v, m_ref, l_ref, acc_ref, idx):
    m_prev = m_ref[idx]
    m_new = jnp.maximum(m_prev, jnp.max(s, axis=-1, keepdims=True))
    alpha = jnp.exp(m_prev - m_new)
    p = jnp.exp(s - m_new)
    l_ref[idx] = alpha * l_ref[idx] + jnp.sum(p, axis=-1, keepdims=True)
    acc_ref[idx] = alpha * acc_ref[idx] + jnp.dot(p.astype(BF16), v, preferred_element_type=F32)
    m_ref[idx] = m_new


def _fox_flash_body(qa_ref, ka_ref, v_ref, o_ref, m_ref, l_ref, acc_ref, *, tq):
    qi = pl.program_id(1)
    ki = pl.program_id(2)

    @pl.when(ki == 0)
    def _():
        m_ref[...] = jnp.full(m_ref.shape, NEG_INF, F32)
        l_ref[...] = jnp.zeros_like(l_ref)
        acc_ref[...] = jnp.zeros_like(acc_ref)

    def tiles(masked):
        if masked:
            row = lax.broadcasted_iota(jnp.int32, (tq, tq), 0)
            col = lax.broadcasted_iota(jnp.int32, (tq, tq), 1)
            keep = col <= row
        for h in range(H_FOX):
            s = _dot_nt(qa_ref[h], ka_ref[h])
            if masked:
                s = jnp.where(keep, s, NEG_INF)
            _softmax_update(s, v_ref[h], m_ref, l_ref, acc_ref, h)

    @pl.when(ki < qi)
    def _():
        tiles(False)

    @pl.when(ki == qi)
    def _():
        tiles(True)
        for h in range(H_FOX):
            o_ref[h] = acc_ref[h] / l_ref[h]


def _fox_flash(qa, ka, v):
    b, h, t, da = qa.shape
    dv = v.shape[-1]
    tq = _tile(t, 512, LANES)
    nq = t // tq
    kv_map = lambda bi, qi, ki: (bi, 0, jnp.minimum(ki, qi), 0)
    return pl.pallas_call(
        functools.partial(_fox_flash_body, tq=tq),
        grid=(b, nq, nq),
        in_specs=[pl.BlockSpec((None, h, tq, da), lambda bi, qi, ki: (bi, 0, qi, 0)),
                  pl.BlockSpec((None, h, tq, da), kv_map),
                  pl.BlockSpec((None, h, tq, dv), kv_map)],
        out_specs=pl.BlockSpec((None, h, tq, dv), lambda bi, qi, ki: (bi, 0, qi, 0)),
        out_shape=jax.ShapeDtypeStruct((b, h, t, dv), F32),
        scratch_shapes=[pltpu.VMEM((h, tq, 1), F32), pltpu.VMEM((h, tq, 1), F32),
                        pltpu.VMEM((h, tq, dv), F32)],
        compiler_params=_cparams(3, 48),
        name="fox_flash",
    )(qa, ka, v)


def _lambda(lq1_ref, lk1_ref, lq2_ref, lk2_ref, lam_init):
    s1 = jnp.sum(lq1_ref[...] * lk1_ref[...], axis=-1, keepdims=True)
    s2 = jnp.sum(lq2_ref[...] * lk2_ref[...], axis=-1, keepdims=True)
    return jnp.exp(s1) - jnp.exp(s2) + lam_init


def _subln(o, g, lam_init):
    ms = jnp.mean(o * o, axis=-1, keepdims=True)
    return o * lax.rsqrt(ms + SUBLN_EPS) * g * (1.0 - lam_init)


def _diff_flash_body(tab_ref, q_ref, k_ref, v_ref, lq1_ref, lk1_ref, lq2_ref, lk2_ref, g_ref,
                     o_ref, bias_ref, m_ref, l_ref, acc_ref, *, tq, lam_init):
    bi = pl.program_id(0)
    qi = pl.program_id(1)
    ki = pl.program_id(2)

    @pl.when((bi == 0) & (qi == 0) & (ki == 0))
    def _():
        def fill(rb, carry):
            r0 = pl.multiple_of(rb * SUBLANES, SUBLANES)
            rows = r0 + lax.broadcasted_iota(jnp.int32, (SUBLANES, tq), 0)
            cols = lax.broadcasted_iota(jnp.int32, (SUBLANES, tq), 1)
            for which in range(2):
                bucket = _t5_bucket(jnp.maximum(rows - cols + which * tq, 0))
                for c in range(2 * H_DIFF):
                    bias_ref[c, which, pl.ds(r0, SUBLANES), :] = _t5_lookup(bucket, tab_ref, c)
            return carry
        lax.fori_loop(0, tq // SUBLANES, fill, 0)

    @pl.when(ki == 0)
    def _():
        m_ref[...] = jnp.full(m_ref.shape, NEG_INF, F32)
        l_ref[...] = jnp.zeros_like(l_ref)
        acc_ref[...] = jnp.zeros_like(acc_ref)

    def tiles(mode):
        lane = lax.broadcasted_iota(jnp.int32, (tq, 2 * HEAD_DIM), 1)
        if mode == 0:
            row = lax.broadcasted_iota(jnp.int32, (tq, tq), 0)
            col = lax.broadcasted_iota(jnp.int32, (tq, tq), 1)
            keep = col <= row
        for h in range(H_DIFF):
            q = q_ref[h]
            k = k_ref[h]
            v = v_ref[h]
            for mp in range(2):
                c = mp * H_DIFF + h
                qm = jnp.where((lane >= mp * HEAD_DIM) & (lane < (mp + 1) * HEAD_DIM), q,
                               jnp.zeros_like(q))
                s = _dot_nt(qm, k)
                if mode == 0:
                    s = jnp.where(keep, s + bias_ref[c, 0], NEG_INF)
                elif mode == 1:
                    s = s + bias_ref[c, 1]
                else:
                    s = s + tab_ref[NUM_BUCKETS - 1, c]
                _softmax_update(s, v, m_ref, l_ref, acc_ref, c)

    @pl.when(ki < qi - 1)
    def _():
        tiles(2)

    @pl.when(ki == qi - 1)
    def _():
        tiles(1)

    @pl.when(ki == qi)
    def _():
        tiles(0)
        lam = _lambda(lq1_ref, lk1_ref, lq2_ref, lk2_ref, lam_init)
        for h in range(H_DIFF):
            o = acc_ref[h] / l_ref[h] - lam * (acc_ref[H_DIFF + h] / l_ref[H_DIFF + h])
            o_ref[h] = _subln(o, g_ref[...], lam_init)


def _diff_flash(table, q, k, v, lq1, lk1, lq2, lk2, g, lam_init):
    b, h, t, d = q.shape
    tq = _tile(t, 512, LANES)
    assert tq >= MAX_DISTANCE or tq == t, "far blocks must lie in the last bucket"
    nq = t // tq
    kv_map = lambda bi, qi, ki: (bi, 0, jnp.minimum(ki, qi), 0)
    vec = pl.BlockSpec((1, HEAD_DIM), lambda bi, qi, ki: (0, 0))
    return pl.pallas_call(
        functools.partial(_diff_flash_body, tq=tq, lam_init=lam_init),
        grid=(b, nq, nq),
        in_specs=[pl.BlockSpec(memory_space=pltpu.SMEM),
                  pl.BlockSpec((None, h, tq, d), lambda bi, qi, ki: (bi, 0, qi, 0)),
                  pl.BlockSpec((None, h, tq, d), kv_map),
                  pl.BlockSpec((None, h, tq, d), kv_map),
                  vec, vec, vec, vec,
                  pl.BlockSpec((1, d), lambda bi, qi, ki: (0, 0))],
        out_specs=pl.BlockSpec((None, h, tq, d), lambda bi, qi, ki: (bi, 0, qi, 0)),
        out_shape=jax.ShapeDtypeStruct((b, h, t, d), F32),
        scratch_shapes=[pltpu.VMEM((2 * h, 2, tq, tq), F32),
                        pltpu.VMEM((2 * h, tq, 1), F32), pltpu.VMEM((2 * h, tq, 1), F32),
                        pltpu.VMEM((2 * h, tq, d), F32)],
        compiler_params=_cparams(3, 54),
        name="diff_flash",
    )(table, q, k, v, lq1, lk1, lq2, lk2, g)


def _decode_tile(qb_ref, k_ref, v_ref, bias, m_ref, l_ref, acc_ref):
    s = _dot_nt(qb_ref[...], k_ref[...].astype(BF16)) + bias
    m_prev = m_ref[...]
    m_new = jnp.maximum(m_prev, jnp.max(s, axis=-1, keepdims=True))
    alpha = jnp.exp(m_prev - m_new)
    p = jnp.exp(s - m_new)
    l_ref[...] = alpha * l_ref[...] + jnp.sum(p, axis=-1, keepdims=True)
    acc_ref[...] = alpha * acc_ref[...] + jnp.dot(p.astype(BF16), v_ref[...].astype(BF16),
                                                  preferred_element_type=F32)
    m_ref[...] = m_new


def _decode_init(m_ref, l_ref, acc_ref):
    m_ref[...] = jnp.full(m_ref.shape, NEG_INF, F32)
    l_ref[...] = jnp.zeros_like(l_ref)
    acc_ref[...] = jnp.zeros_like(acc_ref)


def _fox_decode_body(pt_ref, qb_ref, kc_ref, vc_ref, kn_ref, vn_ref, bias_ref, o_ref,
                     m_ref, l_ref, acc_ref, *, n_pages, ts):
    p = pl.program_id(1)

    @pl.when(p == 0)
    def _():
        _decode_init(m_ref, l_ref, acc_ref)

    @pl.when(p < n_pages)
    def _():
        _decode_tile(qb_ref, kc_ref, vc_ref, bias_ref[...], m_ref, l_ref, acc_ref)

    @pl.when(p == n_pages)
    def _():
        _decode_tile(qb_ref, kn_ref, vn_ref, bias_ref[...], m_ref, l_ref, acc_ref)
        for h in range(H_FOX):
            rows = slice(h * ts, (h + 1) * ts)
            cols = slice(h * HEAD_DIM, (h + 1) * HEAD_DIM)
            o_ref[:, cols] = acc_ref[rows, cols] / l_ref[rows, :]


def _decode_specs(layer, n_pages, rows, page, w):
    q_spec = pl.BlockSpec((None, rows, w), lambda b, p, pt: (b, 0, 0))
    c_spec = pl.BlockSpec((None, None, page, w),
                          lambda b, p, pt: (layer, pt[b, jnp.minimum(p, n_pages - 1)], 0, 0))
    n_spec = pl.BlockSpec((None, page, w), lambda b, p, pt: (b, 0, 0))
    return q_spec, c_spec, n_spec


def _fox_decode(page_table, qb, kc, vc, kn, vn, bias, layer, ts):
    db, rows, w = qb.shape
    page = kc.shape[2]
    n_pages = page_table.shape[1]
    q_spec, c_spec, n_spec = _decode_specs(layer, n_pages, rows, page, w)
    grid_spec = pltpu.PrefetchScalarGridSpec(
        num_scalar_prefetch=1,
        grid=(db, n_pages + 1),
        in_specs=[q_spec, c_spec, c_spec, n_spec, n_spec,
                  pl.BlockSpec((None, None, rows, page), lambda b, p, pt: (b, p, 0, 0))],
        out_specs=pl.BlockSpec((None, ts, w), lambda b, p, pt: (b, 0, 0)),
        scratch_shapes=[pltpu.VMEM((rows, 1), F32), pltpu.VMEM((rows, 1), F32),
                        pltpu.VMEM((rows, w), F32)])
    return pl.pallas_call(
        functools.partial(_fox_decode_body, n_pages=n_pages, ts=ts),
        grid_spec=grid_spec,
        out_shape=jax.ShapeDtypeStruct((db, ts, w), F32),
        compiler_params=_cparams(2, 24),
        name="fox_decode",
    )(page_table, qb, kc, vc, kn, vn, bias)


def _diff_decode_body(pt_ref, tab_ref, qb_ref, kc_ref, vc_ref, kn_ref, vn_ref,
                      lq1_ref, lk1_ref, lq2_ref, lk2_ref, g_ref, o_ref,
                      bias_ref, m_ref, l_ref, acc_ref, *, n_pages, ts, page, lam_init):
    b = pl.program_id(0)
    p = pl.program_id(1)
    past = n_pages * page

    @pl.when((b == 0) & (p == 0))
    def _():
        lane = lax.broadcasted_iota(jnp.int32, (ts, page), 1)
        qpos = past + lax.broadcasted_iota(jnp.int32, (ts, page), 0)
        for tile in range(n_pages + 1):
            kpos = tile * page + lane
            valid = kpos <= qpos
            if tile == n_pages:
                valid = valid & (lane < ts)
            bucket = _t5_bucket(jnp.maximum(qpos - kpos, 0))
            for rb in range(2 * H_DIFF):
                c = (rb % 2) * H_DIFF + rb // 2
                val = _t5_lookup(bucket, tab_ref, c)
                bias_ref[tile, rb * ts:(rb + 1) * ts, :] = jnp.where(valid, val, NEG_INF)

    @pl.when(p == 0)
    def _():
        _decode_init(m_ref, l_ref, acc_ref)

    @pl.when(p < n_pages)
    def _():
        _decode_tile(qb_ref, kc_ref, vc_ref, bias_ref[p], m_ref, l_ref, acc_ref)

    @pl.when(p == n_pages)
    def _():
        _decode_tile(qb_ref, kn_ref, vn_ref, bias_ref[n_pages], m_ref, l_ref, acc_ref)
        lam = _lambda(lq1_ref, lk1_ref, lq2_ref, lk2_ref, lam_init)
        for h in range(H_DIFF):
            r1 = slice(2 * h * ts, (2 * h + 1) * ts)
            r2 = slice((2 * h + 1) * ts, (2 * h + 2) * ts)
            cols = slice(h * 2 * HEAD_DIM, (h + 1) * 2 * HEAD_DIM)
            o = acc_ref[r1, cols] / l_ref[r1, :] - lam * (acc_ref[r2, cols] / l_ref[r2, :])
            o_ref[:, cols] = _subln(o, g_ref[...], lam_init)


def _diff_decode(page_table, table, qb, kc, vc, kn, vn, lq1, lk1, lq2, lk2, g, layer, ts, lam_init):
    db, rows, w = qb.shape
    page = kc.shape[2]
    n_pages = page_table.shape[1]
    q_spec, c_spec, n_spec = _decode_specs(layer, n_pages, rows, page, w)
    vec = pl.BlockSpec((1, HEAD_DIM), lambda b, p, pt: (0, 0))
    grid_spec = pltpu.PrefetchScalarGridSpec(
        num_scalar_prefetch=1,
        grid=(db, n_pages + 1),
        in_specs=[pl.BlockSpec(memory_space=pltpu.SMEM), q_spec, c_spec, c_spec, n_spec, n_spec,
                  vec, vec, vec, vec, pl.BlockSpec((1, 2 * HEAD_DIM), lambda b, p, pt: (0, 0))],
        out_specs=pl.BlockSpec((None, ts, w), lambda b, p, pt: (b, 0, 0)),
        scratch_shapes=[pltpu.VMEM((n_pages + 1, rows, page), F32),
                        pltpu.VMEM((rows, 1), F32), pltpu.VMEM((rows, 1), F32),
                        pltpu.VMEM((rows, w), F32)])
    return pl.pallas_call(
        functools.partial(_diff_decode_body, n_pages=n_pages, ts=ts, page=page, lam_init=lam_init),
        grid_spec=grid_spec,
        out_shape=jax.ShapeDtypeStruct((db, ts, w), F32),
        compiler_params=_cparams(2, 24),
        name="diff_decode",
    )(page_table, table, qb, kc, vc, kn, vn, lq1, lk1, lq2, lk2, g)


def _lru_body(lx_ref, lg_ref, c0_ref, h0_ref, cw_ref, cb_ref, wa_ref, ba_ref, wx_ref, bx_ref,
              lam_ref, o_ref, hn_ref, cin_ref, a_ref, u_ref, hs_ref, hc_ref, *, tc):
    @pl.when(pl.program_id(1) == 0)
    def _():
        cin_ref[0:SUBLANES, :] = jnp.zeros((SUBLANES, W_BRANCH), F32)
        cin_ref[SUBLANES - (CONV_W - 1):SUBLANES, :] = c0_ref[...]
        hc_ref[...] = h0_ref[...]

    lx = lx_ref[...]
    cin_ref[SUBLANES:SUBLANES + tc, :] = lx
    base = SUBLANES - (CONV_W - 1)
    acc = cw_ref[0:1, :] * cin_ref[base:base + tc, :]
    for j in range(1, CONV_W):
        acc = acc + cw_ref[j:j + 1, :] * cin_ref[base + j:base + j + tc, :]
    xc = cb_ref[...] + acc
    xb = xc.astype(BF16)
    r_gate = jax.nn.sigmoid(jnp.dot(xb, wa_ref[...], preferred_element_type=F32) + ba_ref[...])
    i_gate = jax.nn.sigmoid(jnp.dot(xb, wx_ref[...], preferred_element_type=F32) + bx_ref[...])
    log_a = -LRU_C * r_gate * _softplus(-lam_ref[...])
    a_ref[...] = jnp.exp(log_a)
    u_ref[...] = jnp.sqrt(1.0 - jnp.exp(2.0 * log_a)) * (i_gate * xc)

    def step(t, h):
        h = a_ref[pl.ds(t, 1), :] * h + u_ref[pl.ds(t, 1), :]
        hs_ref[pl.ds(t, 1), :] = h
        return h

    h = lax.fori_loop(0, tc, step, hc_ref[...], unroll=8)
    hc_ref[...] = h
    hn_ref[...] = h
    o_ref[...] = hs_ref[...] * jax.nn.gelu(lg_ref[...])
    cin_ref[0:SUBLANES, :] = lx[tc - SUBLANES:tc, :]


def _lru(proj, row0, nseq, t, conv0, h0, layer, cw, cb, wa, ba, wx, bx, lam):
    m_total = proj.shape[0]
    tc = _tile(t, 512, SUBLANES)
    nc = t // tc
    blk0 = row0 // tc
    col = lambda off: (lambda s, c: (blk0 + s * nc + c, off // W_BRANCH))
    vec = pl.BlockSpec((1, W_BRANCH), lambda s, c: (0, 0))
    mat = pl.BlockSpec((W_BRANCH, W_BRANCH), lambda s, c: (0, 0))
    return pl.pallas_call(
        functools.partial(_lru_body, tc=tc),
        grid=(nseq, nc),
        in_specs=[pl.BlockSpec((tc, W_BRANCH), col(OFF_LX)),
                  pl.BlockSpec((tc, W_BRANCH), col(OFF_LG)),
                  pl.BlockSpec((None, None, CONV_W - 1, W_BRANCH), lambda s, c: (layer, s, 0, 0)),
                  pl.BlockSpec((None, None, 1, W_BRANCH), lambda s, c: (layer, s, 0, 0)),
                  pl.BlockSpec((CONV_W, W_BRANCH), lambda s, c: (0, 0)),
                  vec, mat, vec, mat, vec, vec],
        out_specs=[pl.BlockSpec((tc, W_BRANCH), lambda s, c: (s * nc + c, 0)),
                   pl.BlockSpec((None, 1, W_BRANCH), lambda s, c: (s, 0, 0))],
        out_shape=[jax.ShapeDtypeStruct((nseq * t, W_BRANCH), F32),
                   jax.ShapeDtypeStruct((nseq, 1, W_BRANCH), F32)],
        scratch_shapes=[pltpu.VMEM((tc + SUBLANES, W_BRANCH), F32),
                        pltpu.VMEM((tc, W_BRANCH), F32), pltpu.VMEM((tc, W_BRANCH), F32),
                        pltpu.VMEM((tc, W_BRANCH), F32), pltpu.VMEM((1, W_BRANCH), F32)],
        compiler_params=_cparams(2, 32),
        name="lru",
    )(proj, proj, conv0, h0, cw, cb, wa, ba, wx, bx, lam)


def _rwkv_prep_body(rr_ref, rk_ref, rv_ref, lo_ref, sh_ref, mu_ref, w0_ref, w2_ref, a0_ref, a2_ref,
                    g2_ref, kk_ref, ka_ref, rkk_ref, seg_ref,
                    r_o, w_o, k_o, v_o, a_o, b_o, g_o, bon_o, xs_ref, *, tc):
    wb = W_BRANCH

    @pl.when(pl.program_id(1) == 0)
    def _():
        xs_ref[0:SUBLANES, :] = jnp.zeros((SUBLANES, W_RWKV_PROJ), F32)
        xs_ref[SUBLANES - 1:SUBLANES, :] = sh_ref[...]

    xs_ref[SUBLANES:SUBLANES + tc, 0:wb] = rr_ref[...]
    xs_ref[SUBLANES:SUBLANES + tc, wb:2 * wb] = rk_ref[...]
    xs_ref[SUBLANES:SUBLANES + tc, 2 * wb:3 * wb] = rv_ref[...]
    xs_ref[SUBLANES:SUBLANES + tc, 3 * wb:W_RWKV_PROJ] = lo_ref[...]
    cur = xs_ref[SUBLANES:SUBLANES + tc, :]
    prev = xs_ref[SUBLANES - 1:SUBLANES - 1 + tc, :]
    rx = cur + (prev - cur) * mu_ref[...]
    xs_ref[0:SUBLANES, :] = cur[tc - SUBLANES:tc, :]

    rr = rx[:, 0:wb]
    rk = rx[:, wb:2 * wb]
    rv = rx[:, 2 * wb:3 * wb]
    wl_al = rx[:, 3 * wb:3 * wb + LANES]
    gl = rx[:, 3 * wb + LANES:W_RWKV_PROJ]
    dot = functools.partial(jnp.dot, preferred_element_type=F32)
    wd = -_softplus(-(w0_ref[...] + dot(jnp.tanh(wl_al).astype(BF16), w2_ref[...]))) - 0.5
    decay = jnp.exp(-jnp.exp(wd))
    aa = jax.nn.sigmoid(a0_ref[...] + dot(wl_al.astype(BF16), a2_ref[...]))
    gg = dot(jax.nn.sigmoid(gl).astype(BF16), g2_ref[...])
    kk = rk * kk_ref[...]
    norm = jnp.sqrt(_dot_sel_rhs(kk * kk, seg_ref[...]))
    kk = kk / jnp.maximum(norm, 1e-12)
    kh = rk * (1.0 + (aa - 1.0) * ka_ref[...])
    r_o[...] = rr
    w_o[...] = decay
    k_o[...] = kh
    v_o[...] = rv
    a_o[...] = -kk
    b_o[...] = kk * aa
    g_o[...] = gg
    bon_o[...] = _dot_sel_rhs(rr * kh * rkk_ref[...], seg_ref[...]) * rv


def _rwkv_prep(proj, row0, nseq, t, shift0, layer, mu, w0, w2p, a0, a2p, g2, k_k, k_a, r_k, seg):
    tc = _tile(t, 256, SUBLANES)
    nc = t // tc
    blk0 = row0 // tc
    wb = W_BRANCH
    col = lambda off, w: (lambda s, c: (blk0 + s * nc + c, off // w))
    vec = pl.BlockSpec((1, wb), lambda s, c: (0, 0))
    out_blk = pl.BlockSpec((tc, wb), lambda s, c: (s * nc + c, 0))
    lo_w = W_RWKV_PROJ - 3 * wb
    return pl.pallas_call(
        functools.partial(_rwkv_prep_body, tc=tc),
        grid=(nseq, nc),
        in_specs=[pl.BlockSpec((tc, wb), col(OFF_RW, wb)),
                  pl.BlockSpec((tc, wb), col(OFF_RW + wb, wb)),
                  pl.BlockSpec((tc, wb), col(OFF_RW + 2 * wb, wb)),
                  pl.BlockSpec((tc, lo_w), col(OFF_RW + 3 * wb, lo_w)),
                  pl.BlockSpec((None, None, 1, W_RWKV_PROJ), lambda s, c: (layer, s, 0, 0)),
                  pl.BlockSpec((1, W_RWKV_PROJ), lambda s, c: (0, 0)),
                  vec, pl.BlockSpec((LANES, wb), lambda s, c: (0, 0)),
                  vec, pl.BlockSpec((LANES, wb), lambda s, c: (0, 0)),
                  pl.BlockSpec((LANES, wb), lambda s, c: (0, 0)),
                  vec, vec, vec, pl.BlockSpec((wb, wb), lambda s, c: (0, 0))],
        out_specs=[out_blk] * 8,
        out_shape=[jax.ShapeDtypeStruct((nseq * t, wb), F32)] * 8,
        scratch_shapes=[pltpu.VMEM((tc + SUBLANES, W_RWKV_PROJ), F32)],
        compiler_params=_cparams(2, 40),
        name="rwkv_prep",
    )(proj, proj, proj, proj, shift0, mu, w0, w2p, a0, a2p, g2, k_k, k_a, r_k, seg)


def _rwkv_scan_body(r_ref, w_ref, k_ref, v_ref, a_ref, b_ref, s0_ref, yt_ref, sn_ref, st_ref,
                    *, nb, tc, n_chunks):
    c = pl.program_id(1)

    @pl.when(c == 0)
    def _():
        st_ref[...] = s0_ref[...]

    yt_ref[...] = jnp.zeros_like(yt_ref)
    lane =lax.broadcasted_iota(jnp.int32, (HEAD_DIM, tc), 1)
    eye = jnp.where(lax.broadcasted_iota(jnp.int32, (HEAD_DIM, HEAD_DIM), 0)
                    == lax.broadcasted_iota(jnp.int32, (HEAD_DIM, HEAD_DIM), 1), 1.0, 0.0)

    def step(t, carry):
        for s in range(nb):
            for h in range(H_RWKV):
                row = lambda ref: ref[s, t, pl.ds(h, 1), :]
                st = st_ref[s, h]
                sa = jnp.sum(st * row(a_ref), axis=1, keepdims=True)
                vcol = jnp.sum(eye * row(v_ref), axis=1, keepdims=True)
                st = st * row(w_ref) + sa * row(b_ref) + vcol * row(k_ref)
                st_ref[s, h] = st
                y = jnp.sum(st * row(r_ref), axis=1, keepdims=True)
                yt_ref[s, h] = jnp.where(lane == t, y, yt_ref[s, h])
        return carry

    lax.fori_loop(0, tc, step, 0)

    @pl.when(c == n_chunks - 1)
    def _():
        sn_ref[...] = st_ref[...]


def _rwkv_scan(r, w, k, v, a, b, s0, layer):
    nseq, t, h, n = r.shape
    nb = 2 if nseq % 2 == 0 else 1
    tc = _tile(t, LANES, LANES)
    n_chunks = t // tc
    blk = pl.BlockSpec((nb, tc, h, n), lambda s, c: (s, c, 0, 0))
    st_blk = pl.BlockSpec((nb, h, n, n), lambda s, c: (s, 0, 0, 0))
    return pl.pallas_call(
        functools.partial(_rwkv_scan_body, nb=nb, tc=tc, n_chunks=n_chunks),
        grid=(nseq // nb, n_chunks),
        in_specs=[blk] * 6 + [pl.BlockSpec((None, nb, h, n, n), lambda s, c: (layer, s, 0, 0, 0))],
        out_specs=[pl.BlockSpec((nb, h, n, tc), lambda s, c: (s, 0, 0, c)), st_blk],
        out_shape=[jax.ShapeDtypeStruct((nseq, h, n, t), F32),
                   jax.ShapeDtypeStruct((nseq, h, n, n), F32)],
        scratch_shapes=[pltpu.VMEM((nb, h, n, n), F32)],
        compiler_params=_cparams(2, 40),
        name="rwkv_scan",
    )(r, w, k, v, a, b, s0)


def _rwkv_post_body(y_ref, g_ref, bon_ref, lg_ref, lb_ref, seg_ref, o_ref):
    y = y_ref[...]
    inv_n = 1.0 / HEAD_DIM
    mu = _dot_sel_rhs(y, seg_ref[...]) * inv_n
    d = y - mu
    var = _dot_sel_rhs(d * d, seg_ref[...]) * inv_n
    yn = d * lax.rsqrt(var + GN_EPS) * lg_ref[...] + lb_ref[...]
    o_ref[...] = (yn + bon_ref[...]) * g_ref[...]


def _rwkv_post(y, gg, bonus, lnx_g, lnx_b, seg):
    m, wb = y.shape
    tm = _tile(m, 512, SUBLANES)
    blk = pl.BlockSpec((tm, wb), lambda i: (i, 0))
    vec = pl.BlockSpec((1, wb), lambda i: (0, 0))
    return pl.pallas_call(
        _rwkv_post_body,
        grid=(m // tm,),
        in_specs=[blk, blk, blk, vec, vec, pl.BlockSpec((wb, wb), lambda i: (0, 0))],
        out_specs=blk,
        out_shape=jax.ShapeDtypeStruct((m, wb), F32),
        compiler_params=_cparams(1, 24),
        name="rwkv_post",
    )(y, gg, bonus, lnx_g, lnx_b, seg)


def _block_diag(blocks):
    n, bi, bj = blocks.shape
    eye = jnp.eye(n, dtype=blocks.dtype)
    return jnp.einsum('nij,nm->nimj', blocks, eye).reshape(n * bi, n * bj)


def _heads_first(x, b, t, h, d):
    return x.reshape(b, t, h, d).transpose(0, 2, 1, 3)


def kernel(x_prompt, x_sample, cache_fox_k, cache_fox_v, cache_fox_logf, cache_diff_k, cache_diff_v, state_lru_conv, state_lru_h, state_rwkv_shift, state_rwkv_wkv, page_table, norm1_g, w_in, fox_fb, lam_q1, lam_k1, lam_q2, lam_k2, diff_subln_g, t5_table, lru_conv_w, lru_conv_b, lru_wa, lru_ba, lru_wx, lru_bx, lru_lambda, rwkv_mu, rwkv_w0, rwkv_w2, rwkv_a0, rwkv_a2, rwkv_g2, rwkv_k_k, rwkv_k_a, rwkv_r_k, rwkv_lnx_g, rwkv_lnx_b, wb_fox, wb_diff, wb_lru, wb_rwkv, w_out, norm2_g, w_up, w_down, final_g):
    B, T, D = x_prompt.shape
    DB, TS, _ = x_sample.shape
    L = w_in.shape[0]
    n_pool, page = cache_fox_k.shape[1], cache_fox_k.shape[2]
    n_pages = page_table.shape[1]
    mp, ms = B * T, DB * TS
    wb = W_BRANCH
    assert TS % SUBLANES == 0 and TS <= page and T % SUBLANES == 0

    x = jnp.concatenate([x_prompt.reshape(mp, D), x_sample.reshape(ms, D)], axis=0)
    kc_fox = cache_fox_k.reshape(L, n_pool, page, wb)
    vc_fox = cache_fox_v.reshape(L, n_pool, page, wb)
    kc_diff = cache_diff_k.reshape(L, n_pool, page, wb)
    vc_diff = cache_diff_v.reshape(L, n_pool, page, wb)
    seg = _block_diag(jnp.ones((H_RWKV, HEAD_DIM, HEAD_DIM), BF16))
    zeros_conv = jnp.zeros((L, B, CONV_W - 1, wb), F32)
    zeros_h = jnp.zeros((L, B, 1, wb), F32)
    zeros_shift = jnp.zeros((L, B, 1, W_RWKV_PROJ), F32)
    zeros_wkv = jnp.zeros((L, B, H_RWKV, HEAD_DIM, HEAD_DIM), F32)
    eye_f = jnp.eye(H_FOX, dtype=F32)
    eye_d = jnp.eye(H_DIFF, dtype=F32)
    eye_2 = jnp.eye(2, dtype=F32)
    row = lambda v: v.reshape(1, -1)

    def pad_rows(z):
        return jnp.pad(z, ((0, 0), (0, page - TS), (0, 0)))

    p_states, s_states = [], []
    y_norm = None
    for l in range(L):
        lam_init = 0.8 - 0.6 * math.exp(-0.3 * l)
        w = w_in[l]
        n_a = OFF_FF
        split_ff = 3 * wb
        w_r = jnp.concatenate(
            [w[:, :split_ff], w[:, split_ff + H_FOX:split_ff + H_FOX + (n_a - split_ff)],
             w[:, split_ff:split_ff + H_FOX], jnp.zeros((D, FF_PAD), F32),
             w[:, split_ff + H_FOX + (n_a - split_ff):]], axis=1).astype(BF16)
        proj = _rms_matmul(x, row(norm1_g[l]), w_r)
        pp, ps = proj[:mp], proj[mp:]

        lf_p, c_p = _logf_cumsum(pp[:, OFF_FF:OFF_FF + H_FOX].reshape(B, T, H_FOX), row(fox_fb[l]),
                                 jnp.zeros((B, 1, H_FOX), F32))
        c_past, c_tot = _paged_cumsum(cache_fox_logf, page_table, l)
        lf_s, c_s = _logf_cumsum(ps[:, OFF_FF:OFF_FF + H_FOX].reshape(DB, TS, H_FOX), row(fox_fb[l]), c_tot)

        fq = pp[:, OFF_FQ:OFF_FQ + wb].reshape(B, T, H_FOX, HEAD_DIM)
        fk = pp[:, OFF_FK:OFF_FK + wb].reshape(B, T, H_FOX, HEAD_DIM)
        fv = pp[:, OFF_FV:OFF_FV + wb].reshape(B, T, H_FOX, HEAD_DIM)
        c1, c2, c3 = (z[..., None] for z in _split3(c_p))
        ones = jnp.ones((B, T, H_FOX, 3), BF16)
        zpad = jnp.zeros((B, T, H_FOX, 2 * HEAD_DIM - HEAD_DIM - 6), BF16)
        qa = jnp.concatenate([(fq * QK_SCALE).astype(BF16), c1, c2, c3, ones, zpad], axis=-1)
        ka = jnp.concatenate([fk.astype(BF16), ones, -c1, -c2, -c3, zpad], axis=-1)
        o_fox_p = _fox_flash(qa.transpose(0, 2, 1, 3), ka.transpose(0, 2, 1, 3),
                             fv.astype(BF16).transpose(0, 2, 1, 3))
        o_fox_p = o_fox_p.transpose(0, 2, 1, 3).reshape(mp, wb)

        fq_s = ps[:, OFF_FQ:OFF_FQ + wb].reshape(DB, TS, H_FOX, HEAD_DIM) * QK_SCALE
        qb = jnp.einsum('bthd,hg->bhtgd', fq_s, eye_f).reshape(DB, H_FOX * TS, wb).astype(BF16)
        cq = c_s.transpose(0, 2, 1)
        ck = c_past.reshape(DB, n_pages, page, H_FOX).transpose(0, 1, 3, 2)
        bias_past = cq[:, None, :, :, None] - ck[:, :, :, None, :]
        bias_new = cq[:, :, :, None] - cq[:, :, None, :]
        causal = jnp.arange(TS)[None, :] <= jnp.arange(TS)[:, None]
        bias_new = jnp.where(causal, bias_new, NEG_INF)
        bias_new = jnp.pad(bias_new, ((0, 0), (0, 0), (0, 0), (0, page - TS)), constant_values=NEG_INF)
        bias = jnp.concatenate([bias_past, bias_new[:, None]], axis=1).reshape(
            DB, n_pages + 1, H_FOX * TS, page)
        kn = pad_rows(ps[:, OFF_FK:OFF_FK + wb].reshape(DB, TS, wb))
        vn = pad_rows(ps[:, OFF_FV:OFF_FV + wb].reshape(DB, TS, wb))
        o_fox_s = _fox_decode(page_table, qb, kc_fox, vc_fox, kn, vn, bias, l, TS).reshape(ms, wb)

        lam_vecs = (row(lam_q1[l]), row(lam_k1[l]), row(lam_q2[l]), row(lam_k2[l]))
        g_sub = row(diff_subln_g[l])
        dq = _heads_first(pp[:, OFF_DQ:OFF_DQ + wb] * QK_SCALE, B, T, H_DIFF, 2 * HEAD_DIM).astype(BF16)
        dk = _heads_first(pp[:, OFF_DK:OFF_DK + wb], B, T, H_DIFF, 2 * HEAD_DIM).astype(BF16)
        dv = _heads_first(pp[:, OFF_DV:OFF_DV + wb], B, T, H_DIFF, 2 * HEAD_DIM).astype(BF16)
        o_diff_p = _diff_flash(t5_table, dq, dk, dv, *lam_vecs, g_sub, lam_init)
        o_diff_p = o_diff_p.transpose(0, 2, 1, 3).reshape(mp, wb)

        dq_s = ps[:, OFF_DQ:OFF_DQ + wb].reshape(DB, TS, H_DIFF, 2, HEAD_DIM) * QK_SCALE
        qb_d = jnp.einsum('bthmd,hg,mn->bhmtgnd', dq_s, eye_d, eye_2).reshape(
            DB, 2 * H_DIFF * TS, wb).astype(BF16)
        kn_d = pad_rows(ps[:, OFF_DK:OFF_DK + wb].reshape(DB, TS, wb))
        vn_d = pad_rows(ps[:, OFF_DV:OFF_DV + wb].reshape(DB, TS, wb))
        o_diff_s = _diff_decode(page_table, t5_table, qb_d, kc_diff, vc_diff, kn_d, vn_d,
                                *lam_vecs, g_sub, l, TS, lam_init).reshape(ms, wb)

        lru_w = (lru_conv_w[l], row(lru_conv_b[l]), _block_diag(lru_wa[l]).astype(BF16), row(lru_ba[l]),
                 _block_diag(lru_wx[l]).astype(BF16), row(lru_bx[l]), row(lru_lambda[l]))
        o_lru_p, h_p = _lru(proj, 0, B, T, zeros_conv, zeros_h, l, *lru_w)
        o_lru_s, h_s = _lru(proj, mp, DB, TS, state_lru_conv, state_lru_h.reshape(L, DB, 1, wb), l, *lru_w)

        rank = rwkv_w2.shape[1]
        w2p = jnp.concatenate([rwkv_w2[l], jnp.zeros((LANES - rank, wb), F32)], axis=0).astype(BF16)
        a2p = jnp.concatenate([jnp.zeros((LANES - rwkv_a2.shape[1], wb), F32), rwkv_a2[l]], axis=0).astype(BF16)
        rw_w = (row(rwkv_mu[l]), row(rwkv_w0[l]), w2p, row(rwkv_a0[l]), a2p, rwkv_g2[l].astype(BF16),
                row(rwkv_k_k[l]), row(rwkv_k_a[l]), row(rwkv_r_k[l]), seg)
        prep_p = _rwkv_prep(proj, 0, B, T, zeros_shift, l, *rw_w)
        prep_s = _rwkv_prep(proj, mp, DB, TS, state_rwkv_shift.reshape(L, DB, 1, W_RWKV_PROJ), l, *rw_w)
        hs = lambda z, n, t: z.reshape(n, t, H_RWKV, HEAD_DIM)
        yt_p, wkv_p = _rwkv_scan(*(hs(z, B, T) for z in prep_p[:6]), zeros_wkv, l)
        yt_s, wkv_s = _rwkv_scan(*(hs(z, DB, TS) for z in prep_s[:6]), state_rwkv_wkv, l)
        y_rw = jnp.concatenate([yt_p.transpose(0, 3, 1, 2).reshape(mp, wb),
                                yt_s.transpose(0, 3, 1, 2).reshape(ms, wb)], axis=0)
        gg = jnp.concatenate([prep_p[6], prep_s[6]], axis=0)
        bonus = jnp.concatenate([prep_p[7], prep_s[7]], axis=0)
        o_rwkv = _rwkv_post(y_rw, gg, bonus, row(rwkv_lnx_g[l]), row(rwkv_lnx_b[l]), seg)

        outs = (jnp.concatenate([o_fox_p, o_fox_s], axis=0), jnp.concatenate([o_diff_p, o_diff_s], axis=0),
                jnp.concatenate([o_lru_p, o_lru_s], axis=0), o_rwkv)
        wbs = (wb_fox[l].astype(BF16), wb_diff[l].astype(BF16), wb_lru[l].astype(BF16), wb_rwkv[l].astype(BF16))
        mixed = _merge(outs, wbs, proj, D)
        x = _matmul_res(mixed, w_out[l].astype(BF16), x)
        x, y_norm = _mlp(x, row(norm2_g[l]), w_up[l].astype(BF16), w_down[l].astype(BF16), row(final_g))

        def states(pz, n, t, lf, conv0, h_last, wkv):
            lx = pz[:, OFF_LX:OFF_LX + wb].reshape(n, t, wb)
            conv_in = jnp.concatenate([conv0, lx], axis=1)
            rw = pz[:, OFF_RW:OFF_RW + W_RWKV_PROJ].reshape(n, t, W_RWKV_PROJ)
            return (pz[:, OFF_FK:OFF_FK + wb].reshape(n, t, H_FOX, HEAD_DIM),
                    pz[:, OFF_FV:OFF_FV + wb].reshape(n, t, H_FOX, HEAD_DIM),
                    lf,
                    pz[:, OFF_DK:OFF_DK + wb].reshape(n, t, H_DIFF, 2 * HEAD_DIM),
                    pz[:, OFF_DV:OFF_DV + wb].reshape(n, t, H_DIFF, 2 * HEAD_DIM),
                    conv_in[:, -(CONV_W - 1):], h_last.reshape(n, wb), rw[:, -1], wkv)

        p_states.append(states(pp, B, T, lf_p, zeros_conv[0], h_p, wkv_p))
        s_states.append(states(ps, DB, TS, lf_s, state_lru_conv[l], h_s, wkv_s))

    y_prompt = y_norm[:mp].reshape(B, T, D)
    y_sample = y_norm[mp:].reshape(DB, TS, D)
    p_out = [jnp.stack(z) for z in zip(*p_states)]
    s_out = [jnp.stack(z) for z in zip(*s_states)]
    return (y_prompt, y_sample, *p_out, *s_out)
```

```python
import functools
import math

import jax
import jax.numpy as jnp
from jax import lax
from jax.experimental import pallas as pl
from jax.experimental.pallas import tpu as pltpu

F32 = jnp.float32
BF16 = jnp.bfloat16

HEAD_DIM = 64
W_BRANCH = 512
H_FOX = 8
H_DIFF = 4
H_RWKV = 8
CONV_W = 4
LRU_C = 8.0
NUM_BUCKETS = 32
MAX_DISTANCE = 128
RMS_EPS = 1e-6
SUBLN_EPS = 1e-5
GN_EPS = 64e-5
NEG_INF = -1e30
QK_SCALE = HEAD_DIM ** -0.5

LANES = 128
SUBLANES = 8
VMEM_LIMIT_CAP = 56 * 1024 * 1024

OFF_FQ, OFF_FK, OFF_FV = 0, 512, 1024
OFF_DQ, OFF_DK, OFF_DV = 1536, 2048, 2560
OFF_LX, OFF_LG = 3072, 3584
OFF_RW = 4096
W_RWKV_PROJ = 1792
OFF_FF = 5888
OFF_GATE = 6144
FF_PAD = OFF_GATE - OFF_FF - H_FOX


def _tile(n, pref, mult):
    best = None
    t = mult
    while t <= min(n, pref):
        if n % t == 0:
            best = t
        t += mult
    return best if best is not None else n


def _cparams(n_axes, vmem_mb):
    return pltpu.CompilerParams(
        dimension_semantics=("arbitrary",) * n_axes,
        vmem_limit_bytes=min(int(vmem_mb * 1024 * 1024), VMEM_LIMIT_CAP))


def _softplus(x):
    return jnp.maximum(x, 0.0) + jnp.log1p(jnp.exp(-jnp.abs(x)))


def _split3(x):
    def top(v):
        bits = lax.bitcast_convert_type(v, jnp.uint32) & jnp.uint32(0xFFFF0000)
        return lax.bitcast_convert_type(bits, F32)

    x1 = top(x)
    r1 = x - x1
    x2 = top(r1)
    x3 = r1 - x2
    return x1.astype(BF16), x2.astype(BF16), x3.astype(BF16)


def _dot_sel_rhs(x, sel):
    x1, x2, x3 = _split3(x)
    d = functools.partial(jnp.dot, preferred_element_type=F32)
    return d(x1, sel) + d(x2, sel) + d(x3, sel)


def _dot_sel_lhs(sel, x):
    x1, x2, x3 = _split3(x)
    d = functools.partial(jnp.dot, preferred_element_type=F32)
    return d(sel, x1) + d(sel, x2) + d(sel, x3)


def _dot_nt(a, b):
    return lax.dot_general(a, b, (((1,), (1,)), ((), ())), preferred_element_type=F32)


def _rms_matmul_body(x_ref, g_ref, w_ref, o_ref, xn_ref):
    @pl.when(pl.program_id(1) == 0)
    def _():
        x = x_ref[...]
        ms = jnp.mean(x * x, axis=-1, keepdims=True)
        xn_ref[...] = (x * lax.rsqrt(ms + RMS_EPS) * g_ref[...]).astype(BF16)

    o_ref[...] = jnp.dot(xn_ref[...], w_ref[...], preferred_element_type=F32)


def _rms_matmul(x, g, w):
    m, k = x.shape
    n = w.shape[1]
    tm = _tile(m, 1024, SUBLANES)
    tn = _tile(n, 1024, LANES)
    vmem = (2 * tm * k * 4 + 2 * k * tn * 2 + 2 * tm * tn * 4 + tm * k * 2) / 2**20 + 8
    return pl.pallas_call(
        _rms_matmul_body,
        grid=(m // tm, n // tn),
        in_specs=[pl.BlockSpec((tm, k), lambda i, j: (i, 0)),
                  pl.BlockSpec((1, k), lambda i, j: (0, 0)),
                  pl.BlockSpec((k, tn), lambda i, j: (0, j))],
        out_specs=pl.BlockSpec((tm, tn), lambda i, j: (i, j)),
        out_shape=jax.ShapeDtypeStruct((m, n), F32),
        scratch_shapes=[pltpu.VMEM((tm, k), BF16)],
        compiler_params=_cparams(2, vmem),
        name="in_proj",
    )(x, g, w)


def _merge_body(of_ref, od_ref, ol_ref, or_ref, wf_ref, wd_ref, wl_ref, wr_ref,
                gf_ref, gd_ref, gl_ref, gr_ref, o_ref):
    def branch(o_r, w_r, g_r):
        y = jnp.dot(o_r[...].astype(BF16), w_r[...], preferred_element_type=F32)
        return jax.nn.sigmoid(g_r[...]) * y

    acc = branch(of_ref, wf_ref, gf_ref)
    acc = acc + branch(od_ref, wd_ref, gd_ref)
    acc = acc + branch(ol_ref, wl_ref, gl_ref)
    acc = acc + branch(or_ref, wr_ref, gr_ref)
    o_ref[...] = acc.astype(BF16)


def _merge(outs, wbs, proj, d_model):
    m, w = outs[0].shape
    tm = _tile(m, 512, 16)
    tn = _tile(d_model, 512, LANES)
    o_specs = [pl.BlockSpec((tm, w), lambda i, j: (i, 0)) for _ in range(4)]
    w_specs = [pl.BlockSpec((w, tn), lambda i, j: (0, j)) for _ in range(4)]
    g_specs = [pl.BlockSpec((tm, tn), functools.partial(
        lambda i, j, base: (i, base + j), base=(OFF_GATE + b * d_model) // tn)) for b in range(4)]
    return pl.pallas_call(
        _merge_body,
        grid=(m // tm, d_model // tn),
        in_specs=o_specs + w_specs + g_specs,
        out_specs=pl.BlockSpec((tm, tn), lambda i, j: (i, j)),
        out_shape=jax.ShapeDtypeStruct((m, d_model), BF16),
        compiler_params=_cparams(2, 40),
        name="merge",
    )(*outs, *wbs, proj, proj, proj, proj)


def _matmul_res_body(a_ref, w_ref, r_ref, o_ref):
    o_ref[...] = r_ref[...] + jnp.dot(a_ref[...], w_ref[...], preferred_element_type=F32)


def _matmul_res(a, w, res):
    m, k = a.shape
    n = w.shape[1]
    tm = _tile(m, 512, 16)
    tn = _tile(n, 1024, LANES)
    return pl.pallas_call(
        _matmul_res_body,
        grid=(m // tm, n // tn),
        in_specs=[pl.BlockSpec((tm, k), lambda i, j: (i, 0)),
                  pl.BlockSpec((k, tn), lambda i, j: (0, j)),
                  pl.BlockSpec((tm, tn), lambda i, j: (i, j))],
        out_specs=pl.BlockSpec((tm, tn), lambda i, j: (i, j)),
        out_shape=jax.ShapeDtypeStruct((m, n), F32),
        compiler_params=_cparams(2, 40),
        name="out_proj",
    )(a, w, res)


def _mlp_body(x_ref, g_ref, wu_ref, wd_ref, fg_ref, o_ref, y_ref, xn_ref, acc_ref, *, n_chunks):
    c = pl.program_id(1)

    @pl.when(c == 0)
    def _():
        x = x_ref[...]
        ms = jnp.mean(x * x, axis=-1, keepdims=True)
        xn_ref[...] = (x * lax.rsqrt(ms + RMS_EPS) * g_ref[...]).astype(BF16)
        acc_ref[...] = jnp.zeros_like(acc_ref)

    h = jnp.dot(xn_ref[...], wu_ref[...], preferred_element_type=F32)
    h = jnp.square(jnp.maximum(h, 0.0))
    acc_ref[...] += jnp.dot(h.astype(BF16), wd_ref[...], preferred_element_type=F32)

    @pl.when(c == n_chunks - 1)
    def _():
        xo = x_ref[...] + acc_ref[...]
        o_ref[...] = xo
        ms = jnp.mean(xo * xo, axis=-1, keepdims=True)
        y_ref[...] = xo * lax.rsqrt(ms + RMS_EPS) * fg_ref[...]


def _mlp(x, g, wu, wd, final_g):
    m, d = x.shape
    f = wu.shape[1]
    tm = _tile(m, 512, SUBLANES)
    tc = _tile(f, 512, LANES)
    n_chunks = f // tc
    vmem = (2 * tm * d * 4 + 4 * d * tc * 2 + 4 * tm * d * 4 + tm * d * 2 + tm * d * 4) / 2**20 + 10
    return pl.pallas_call(
        functools.partial(_mlp_body, n_chunks=n_chunks),
        grid=(m // tm, n_chunks),
        in_specs=[pl.BlockSpec((tm, d), lambda i, c: (i, 0)),
                  pl.BlockSpec((1, d), lambda i, c: (0, 0)),
                  pl.BlockSpec((d, tc), lambda i, c: (0, c)),
                  pl.BlockSpec((tc, d), lambda i, c: (c, 0)),
                  pl.BlockSpec((1, d), lambda i, c: (0, 0))],
        out_specs=[pl.BlockSpec((tm, d), lambda i, c: (i, 0)),
                   pl.BlockSpec((tm, d), lambda i, c: (i, 0))],
        out_shape=[jax.ShapeDtypeStruct((m, d), F32), jax.ShapeDtypeStruct((m, d), F32)],
        scratch_shapes=[pltpu.VMEM((tm, d), BF16), pltpu.VMEM((tm, d), F32)],
        compiler_params=_cparams(2, vmem),
        name="mlp",
    )(x, g, wu, wd, final_g)


def _cumsum_rows(x, tc):
    if tc >= LANES:
        r = lax.broadcasted_iota(jnp.int32, (tc, tc), 0)
        c = lax.broadcasted_iota(jnp.int32, (tc, tc), 1)
        tri = jnp.where(c <= r, 1.0, 0.0).astype(BF16)
        return _dot_sel_lhs(tri, x)
    rows = lax.broadcasted_iota(jnp.int32, x.shape, 0)
    parts = [jnp.sum(jnp.where(rows <= t, x, 0.0), axis=0, keepdims=True) for t in range(tc)]
    return jnp.concatenate(parts, axis=0)


def _logf_cumsum_body(ff_ref, fb_ref, c0_ref, lf_ref, c_ref, carry_ref, *, tc):
    @pl.when(pl.program_id(1) == 0)
    def _():
        carry_ref[...] = c0_ref[...]

    lf = -_softplus(-(ff_ref[...] + fb_ref[...]))
    lf_ref[...] = lf
    cs = _cumsum_rows(lf, tc) + carry_ref[...]
    c_ref[...] = cs
    carry_ref[...] = cs[tc - 1:tc, :]


def _logf_cumsum(ff, fb, c0):
    nseq, t, h = ff.shape
    tc = _tile(t, 512, SUBLANES)
    blk = pl.BlockSpec((None, tc, h), lambda s, c: (s, c, 0))
    return pl.pallas_call(
        functools.partial(_logf_cumsum_body, tc=tc),
        grid=(nseq, t // tc),
        in_specs=[blk, pl.BlockSpec((1, h), lambda s, c: (0, 0)),
                  pl.BlockSpec((None, 1, h), lambda s, c: (s, 0, 0))],
        out_specs=[blk, blk],
        out_shape=[jax.ShapeDtypeStruct((nseq, t, h), F32)] * 2,
        scratch_shapes=[pltpu.VMEM((1, h), F32)],
        compiler_params=_cparams(2, 24),
        name="logf_cumsum",
    )(ff, fb, c0)


def _paged_cumsum_body(pt_ref, lf_ref, c_ref, tot_ref, carry_ref, *, page):
    @pl.when(pl.program_id(1) == 0)
    def _():
        carry_ref[...] = jnp.zeros_like(carry_ref)

    cs = _cumsum_rows(lf_ref[...], page) + carry_ref[...]
    c_ref[...] = cs
    carry_ref[...] = cs[page - 1:page, :]
    tot_ref[...] = cs[page - 1:page, :]


def _paged_cumsum(cache_lf, page_table, layer):
    _, _, page, h = cache_lf.shape
    db, n_pages = page_table.shape
    grid_spec = pltpu.PrefetchScalarGridSpec(
        num_scalar_prefetch=1,
        grid=(db, n_pages),
        in_specs=[pl.BlockSpec((None, None, page, h), lambda b, p, pt: (layer, pt[b, p], 0, 0))],
        out_specs=[pl.BlockSpec((None, page, h), lambda b, p, pt: (b, p, 0)),
                   pl.BlockSpec((None, 1, h), lambda b, p, pt: (b, 0, 0))],
        scratch_shapes=[pltpu.VMEM((1, h), F32)])
    return pl.pallas_call(
        functools.partial(_paged_cumsum_body, page=page),
        grid_spec=grid_spec,
        out_shape=[jax.ShapeDtypeStruct((db, n_pages * page, h), F32),
                   jax.ShapeDtypeStruct((db, 1, h), F32)],
        compiler_params=_cparams(2, 16),
        name="paged_cumsum",
    )(page_table, cache_lf)


def _t5_bucket(n):
    max_exact = NUM_BUCKETS // 2
    nf = jnp.maximum(n, 1).astype(F32)
    large = max_exact + (jnp.log(nf / max_exact) / math.log(MAX_DISTANCE / max_exact)
                         * (NUM_BUCKETS - max_exact)).astype(jnp.int32)
    return jnp.where(n < max_exact, n, jnp.minimum(large, NUM_BUCKETS - 1))


def _t5_lookup(bucket, tab_ref, col):
    acc = jnp.zeros(bucket.shape, F32)
    for j in range(NUM_BUCKETS):
        acc = jnp.where(bucket == j, tab_ref[j, col], acc)
    return acc


def _softmax_update(s, v, m_ref, l_ref, acc_ref, idx):
    m_prev = m_ref[idx]
    m_new = jnp.maximum(m_prev, jnp.max(s, axis=-1, keepdims=True))
    alpha = jnp.exp(m_prev - m_new)
    p = jnp.exp(s - m_new)
    l_ref[idx] = alpha * l_ref[idx] + jnp.sum(p, axis=-1, keepdims=True)
    acc_ref[idx] = alpha * acc_ref[idx] + jnp.dot(p.astype(BF16), v, preferred_element_type=F32)
    m_ref[idx] = m_new


def _fox_flash_body(qa_ref, ka_ref, v_ref, o_ref, m_ref, l_ref, acc_ref, *, tq):
    qi = pl.program_id(1)
    ki = pl.program_id(2)

    @pl.when(ki == 0)
    def _():
        m_ref[...] = jnp.full(m_ref.shape, NEG_INF, F32)
        l_ref[...] = jnp.zeros_like(l_ref)
        acc_ref[...] = jnp.zeros_like(acc_ref)

    def tiles(masked):
        if masked:
            row = lax.broadcasted_iota(jnp.int32, (tq, tq), 0)
            col = lax.broadcasted_iota(jnp.int32, (tq, tq), 1)
            keep = col <= row
        for h in range(H_FOX):
            s = _dot_nt(qa_ref[h], ka_ref[h])
            if masked:
                s = jnp.where(keep, s, NEG_INF)
            _softmax_update(s, v_ref[h], m_ref, l_ref, acc_ref, h)

    @pl.when(ki < qi)
    def _():
        tiles(False)

    @pl.when(ki == qi)
    def _():
        tiles(True)
        for h in range(H_FOX):
            o_ref[h] = acc_ref[h] / l_ref[h]


def _fox_flash(qa, ka, v):
    b, h, t, da = qa.shape
    dv = v.shape[-1]
    tq = _tile(t, 512, LANES)
    nq = t // tq
    kv_map = lambda bi, qi, ki: (bi, 0, jnp.minimum(ki, qi), 0)
    return pl.pallas_call(
        functools.partial(_fox_flash_body, tq=tq),
        grid=(b, nq, nq),
        in_specs=[pl.BlockSpec((None, h, tq, da), lambda bi, qi, ki: (bi, 0, qi, 0)),
                  pl.BlockSpec((None, h, tq, da), kv_map),
                  pl.BlockSpec((None, h, tq, dv), kv_map)],
        out_specs=pl.BlockSpec((None, h, tq, dv), lambda bi, qi, ki: (bi, 0, qi, 0)),
        out_shape=jax.ShapeDtypeStruct((b, h, t, dv), F32),
        scratch_shapes=[pltpu.VMEM((h, tq, 1), F32), pltpu.VMEM((h, tq, 1), F32),
                        pltpu.VMEM((h, tq, dv), F32)],
        compiler_params=_cparams(3, 48),
        name="fox_flash",
    )(qa, ka, v)


def _lambda(lq1_ref, lk1_ref, lq2_ref, lk2_ref, lam_init):
    s1 = jnp.sum(lq1_ref[...] * lk1_ref[...], axis=-1, keepdims=True)
    s2 = jnp.sum(lq2_ref[...] * lk2_ref[...], axis=-1, keepdims=True)
    return jnp.exp(s1) - jnp.exp(s2) + lam_init


def _subln(o, g, lam_init):
    ms = jnp.mean(o * o, axis=-1, keepdims=True)
    return o * lax.rsqrt(ms + SUBLN_EPS) * g * (1.0 - lam_init)


def _diff_flash_body(tab_ref, q_ref, k_ref, v_ref, lq1_ref, lk1_ref, lq2_ref, lk2_ref, g_ref,
                     o_ref, bias_ref, m_ref, l_ref, acc_ref, *, tq, lam_init):
    bi = pl.program_id(0)
    qi = pl.program_id(1)
    ki = pl.program_id(2)

    @pl.when((bi == 0) & (qi == 0) & (ki == 0))
    def _():
        def fill(rb, carry):
            r0 = pl.multiple_of(rb * SUBLANES, SUBLANES)
            rows = r0 + lax.broadcasted_iota(jnp.int32, (SUBLANES, tq), 0)
            cols = lax.broadcasted_iota(jnp.int32, (SUBLANES, tq), 1)
            for which in range(2):
                bucket = _t5_bucket(jnp.maximum(rows - cols + which * tq, 0))
                for c in range(2 * H_DIFF):
                    bias_ref[c, which, pl.ds(r0, SUBLANES), :] = _t5_lookup(bucket, tab_ref, c)
            return carry
        lax.fori_loop(0, tq // SUBLANES, fill, 0)

    @pl.when(ki == 0)
    def _():
        m_ref[...] = jnp.full(m_ref.shape, NEG_INF, F32)
        l_ref[...] = jnp.zeros_like(l_ref)
        acc_ref[...] = jnp.zeros_like(acc_ref)

    def tiles(mode):
        lane = lax.broadcasted_iota(jnp.int32, (tq, 2 * HEAD_DIM), 1)
        if mode == 0:
            row = lax.broadcasted_iota(jnp.int32, (tq, tq), 0)
            col = lax.broadcasted_iota(jnp.int32, (tq, tq), 1)
            keep = col <= row
        for h in range(H_DIFF):
            q = q_ref[h]
            k = k_ref[h]
            v = v_ref[h]
            for mp in range(2):
                c = mp * H_DIFF + h
                qm = jnp.where((lane >= mp * HEAD_DIM) & (lane < (mp + 1) * HEAD_DIM), q,
                               jnp.zeros_like(q))
                s = _dot_nt(qm, k)
                if mode == 0:
                    s = jnp.where(keep, s + bias_ref[c, 0], NEG_INF)
                elif mode == 1:
                    s = s + bias_ref[c, 1]
                else:
                    s = s + tab_ref[NUM_BUCKETS - 1, c]
                _softmax_update(s, v, m_ref, l_ref, acc_ref, c)

    @pl.when(ki < qi - 1)
    def _():
        tiles(2)

    @pl.when(ki == qi - 1)
    def _():
        tiles(1)

    @pl.when(ki == qi)
    def _():
        tiles(0)
        lam = _lambda(lq1_ref, lk1_ref, lq2_ref, lk2_ref, lam_init)
        for h in range(H_DIFF):
            o = acc_ref[h] / l_ref[h] - lam * (acc_ref[H_DIFF + h] / l_ref[H_DIFF + h])
            o_ref[h] = _subln(o, g_ref[...], lam_init)


def _diff_flash(table, q, k, v, lq1, lk1, lq2, lk2, g, lam_init):
    b, h, t, d = q.shape
    tq = _tile(t, 512, LANES)
    assert tq >= MAX_DISTANCE or tq == t, "far blocks must lie in the last bucket"
    nq = t // tq
    kv_map = lambda bi, qi, ki: (bi, 0, jnp.minimum(ki, qi), 0)
    vec = pl.BlockSpec((1, HEAD_DIM), lambda bi, qi, ki: (0, 0))
    return pl.pallas_call(
        functools.partial(_diff_flash_body, tq=tq, lam_init=lam_init),
        grid=(b, nq, nq),
        in_specs=[pl.BlockSpec(memory_space=pltpu.SMEM),
                  pl.BlockSpec((None, h, tq, d), lambda bi, qi, ki: (bi, 0, qi, 0)),
                  pl.BlockSpec((None, h, tq, d), kv_map),
                  pl.BlockSpec((None, h, tq, d), kv_map),
                  vec, vec, vec, vec,
                  pl.BlockSpec((1, d), lambda bi, qi, ki: (0, 0))],
        out_specs=pl.BlockSpec((None, h, tq, d), lambda bi, qi, ki: (bi, 0, qi, 0)),
        out_shape=jax.ShapeDtypeStruct((b, h, t, d), F32),
        scratch_shapes=[pltpu.VMEM((2 * h, 2, tq, tq), F32),
                        pltpu.VMEM((2 * h, tq, 1), F32), pltpu.VMEM((2 * h, tq, 1), F32),
                        pltpu.VMEM((2 * h, tq, d), F32)],
        compiler_params=_cparams(3, 54),
        name="diff_flash",
    )(table, q, k, v, lq1, lk1, lq2, lk2, g)


def _decode_tile(qb_ref, k_ref, v_ref, bias, m_ref, l_ref, acc_ref):
    s = _dot_nt(qb_ref[...], k_ref[...].astype(BF16)) + bias
    m_prev = m_ref[...]
    m_new = jnp.maximum(m_prev, jnp.max(s, axis=-1, keepdims=True))
    alpha = jnp.exp(m_prev - m_new)
    p = jnp.exp(s - m_new)
    l_ref[...] = alpha * l_ref[...] + jnp.sum(p, axis=-1, keepdims=True)
    acc_ref[...] = alpha * acc_ref[...] + jnp.dot(p.astype(BF16), v_ref[...].astype(BF16),
                                                  preferred_element_type=F32)
    m_ref[...] = m_new


def _decode_init(m_ref, l_ref, acc_ref):
    m_ref[...] = jnp.full(m_ref.shape, NEG_INF, F32)
    l_ref[...] = jnp.zeros_like(l_ref)
    acc_ref[...] = jnp.zeros_like(acc_ref)


def _fox_decode_body(pt_ref, qb_ref, kc_ref, vc_ref, kn_ref, vn_ref, bias_ref, o_ref,
                     m_ref, l_ref, acc_ref, *, n_pages, ts):
    p = pl.program_id(1)

    @pl.when(p == 0)
    def _():
        _decode_init(m_ref, l_ref, acc_ref)

    @pl.when(p < n_pages)
    def _():
        _decode_tile(qb_ref, kc_ref, vc_ref, bias_ref[...], m_ref, l_ref, acc_ref)

    @pl.when(p == n_pages)
    def _():
        _decode_tile(qb_ref, kn_ref, vn_ref, bias_ref[...], m_ref, l_ref, acc_ref)
        for h in range(H_FOX):
            rows = slice(h * ts, (h + 1) * ts)
            cols = slice(h * HEAD_DIM, (h + 1) * HEAD_DIM)
            o_ref[:, cols] = acc_ref[rows, cols] / l_ref[rows, :]


def _decode_specs(layer, n_pages, rows, page, w):
    q_spec = pl.BlockSpec((None, rows, w), lambda b, p, pt: (b, 0, 0))
    c_spec = pl.BlockSpec((None, None, page, w),
                          lambda b, p, pt: (layer, pt[b, jnp.minimum(p, n_pages - 1)], 0, 0))
    n_spec = pl.BlockSpec((None, page, w), lambda b, p, pt: (b, 0, 0))
    return q_spec, c_spec, n_spec


def _fox_decode(page_table, qb, kc, vc, kn, vn, bias, layer, ts):
    db, rows, w = qb.shape
    page = kc.shape[2]
    n_pages = page_table.shape[1]
    q_spec, c_spec, n_spec = _decode_specs(layer, n_pages, rows, page, w)
    grid_spec = pltpu.PrefetchScalarGridSpec(
        num_scalar_prefetch=1,
        grid=(db, n_pages + 1),
        in_specs=[q_spec, c_spec, c_spec, n_spec, n_spec,
                  pl.BlockSpec((None, None, rows, page), lambda b, p, pt: (b, p, 0, 0))],
        out_specs=pl.BlockSpec((None, ts, w), lambda b, p, pt: (b, 0, 0)),
        scratch_shapes=[pltpu.VMEM((rows, 1), F32), pltpu.VMEM((rows, 1), F32),
                        pltpu.VMEM((rows, w), F32)])
    return pl.pallas_call(
        functools.partial(_fox_decode_body, n_pages=n_pages, ts=ts),
        grid_spec=grid_spec,
        out_shape=jax.ShapeDtypeStruct((db, ts, w), F32),
        compiler_params=_cparams(2, 24),
        name="fox_decode",
    )(page_table, qb, kc, vc, kn, vn, bias)


def _diff_decode_body(pt_ref, tab_ref, qb_ref, kc_ref, vc_ref, kn_ref, vn_ref,
                      lq1_ref, lk1_ref, lq2_ref, lk2_ref, g_ref, o_ref,
                      bias_ref, m_ref, l_ref, acc_ref, *, n_pages, ts, page, lam_init):
    b = pl.program_id(0)
    p = pl.program_id(1)
    past = n_pages * page

    @pl.when((b == 0) & (p == 0))
    def _():
        lane = lax.broadcasted_iota(jnp.int32, (ts, page), 1)
        qpos = past + lax.broadcasted_iota(jnp.int32, (ts, page), 0)
        for tile in range(n_pages + 1):
            kpos = tile * page + lane
            valid = kpos <= qpos
            if tile == n_pages:
                valid = valid & (lane < ts)
            bucket = _t5_bucket(jnp.maximum(qpos - kpos, 0))
            for rb in range(2 * H_DIFF):
                c = (rb % 2) * H_DIFF + rb // 2
                val = _t5_lookup(bucket, tab_ref, c)
                bias_ref[tile, rb * ts:(rb + 1) * ts, :] = jnp.where(valid, val, NEG_INF)

    @pl.when(p == 0)
    def _():
        _decode_init(m_ref, l_ref, acc_ref)

    @pl.when(p < n_pages)
    def _():
        _decode_tile(qb_ref, kc_ref, vc_ref, bias_ref[p], m_ref, l_ref, acc_ref)

    @pl.when(p == n_pages)
    def _():
        _decode_tile(qb_ref, kn_ref, vn_ref, bias_ref[n_pages], m_ref, l_ref, acc_ref)
        lam = _lambda(lq1_ref, lk1_ref, lq2_ref, lk2_ref, lam_init)
        for h in range(H_DIFF):
            r1 = slice(2 * h * ts, (2 * h + 1) * ts)
            r2 = slice((2 * h + 1) * ts, (2 * h + 2) * ts)
            cols = slice(h * 2 * HEAD_DIM, (h + 1) * 2 * HEAD_DIM)
            o = acc_ref[r1, cols] / l_ref[r1, :] - lam * (acc_ref[r2, cols] / l_ref[r2, :])
            o_ref[:, cols] = _subln(o, g_ref[...], lam_init)


def _diff_decode(page_table, table, qb, kc, vc, kn, vn, lq1, lk1, lq2, lk2, g, layer, ts, lam_init):
    db, rows, w = qb.shape
    page = kc.shape[2]
    n_pages = page_table.shape[1]
    q_spec, c_spec, n_spec = _decode_specs(layer, n_pages, rows, page, w)
    vec = pl.BlockSpec((1, HEAD_DIM), lambda b, p, pt: (0, 0))
    grid_spec = pltpu.PrefetchScalarGridSpec(
        num_scalar_prefetch=1,
        grid=(db, n_pages + 1),
        in_specs=[pl.BlockSpec(memory_space=pltpu.SMEM), q_spec, c_spec, c_spec, n_spec, n_spec,
                  vec, vec, vec, vec, pl.BlockSpec((1, 2 * HEAD_DIM), lambda b, p, pt: (0, 0))],
        out_specs=pl.BlockSpec((None, ts, w), lambda b, p, pt: (b, 0, 0)),
        scratch_shapes=[pltpu.VMEM((n_pages + 1, rows, page), F32),
                        pltpu.VMEM((rows, 1), F32), pltpu.VMEM((rows, 1), F32),
                        pltpu.VMEM((rows, w), F32)])
    return pl.pallas_call(
        functools.partial(_diff_decode_body, n_pages=n_pages, ts=ts, page=page, lam_init=lam_init),
        grid_spec=grid_spec,
        out_shape=jax.ShapeDtypeStruct((db, ts, w), F32),
        compiler_params=_cparams(2, 24),
        name="diff_decode",
    )(page_table, table, qb, kc, vc, kn, vn, lq1, lk1, lq2, lk2, g)


class _Softmax:
    def __init__(self, rows, width):
        self.m = jnp.full((rows, 1), NEG_INF, F32)
        self.l = jnp.zeros((rows, 1), F32)
        self.acc = jnp.zeros((rows, width), F32)

    def update(self, s, pv):
        m_new = jnp.maximum(self.m, jnp.max(s, axis=-1, keepdims=True))
        alpha = jnp.exp(self.m - m_new)
        p = jnp.exp(s - m_new)
        self.l = alpha * self.l + jnp.sum(p, axis=-1, keepdims=True)
        self.acc = alpha * self.acc + pv(p.astype(BF16))
        self.m = m_new


def _page_specs(layer, n_pages, rows, cols):
    return [pl.BlockSpec((None, None, rows, cols),
                         functools.partial(lambda b, pt, p: (layer, pt[b, p], 0, 0), p=p))
            for p in range(n_pages)]


def _fox_dec_body(pt_ref, qb_ref, *refs, n_pages, ts):
    kt = refs[:n_pages]
    vt = refs[n_pages:2 * n_pages]
    lf = refs[2 * n_pages:3 * n_pages]
    knt_ref, vnt_ref, cnb_ref, cnt_ref, o_ref = refs[3 * n_pages:]
    rows = H_FOX * ts
    page = knt_ref.shape[-1]
    qb = qb_ref[...]
    sm = _Softmax(rows, W_BRANCH)
    qk = lambda k_ref: jnp.dot(qb, k_ref[...].astype(BF16), preferred_element_type=F32)
    pv = lambda v_ref: (lambda p: _dot_nt(p, v_ref[...].astype(BF16)))

    lane = lax.broadcasted_iota(jnp.int32, (ts, H_FOX, page), 2).reshape(rows, page)
    tok = lax.broadcasted_iota(jnp.int32, (ts, H_FOX, page), 0).reshape(rows, page)
    cn_keys = jnp.concatenate([cnt_ref[...]] * ts, axis=0)
    bias_new = jnp.where((lane <= tok) & (lane < ts), cnb_ref[...] - cn_keys, NEG_INF)
    sm.update(qk(knt_ref) + bias_new, pv(vnt_ref))

    j_idx = lax.broadcasted_iota(jnp.int32, (page, page), 0)
    k_idx = lax.broadcasted_iota(jnp.int32, (page, page), 1)
    later = jnp.where(j_idx > k_idx, 1.0, 0.0).astype(BF16)
    carry = jnp.zeros((H_FOX, 1), F32)
    for p in reversed(range(n_pages)):
        lfp = lf[p][...]
        suffix = _dot_sel_rhs(lfp, later) + carry
        carry = suffix[:, 0:1] + lfp[:, 0:1]
        bias = jnp.concatenate([suffix] * ts, axis=0) + cnb_ref[...]
        sm.update(qk(kt[p]) + bias, pv(vt[p]))

    accn = (sm.acc / sm.l).reshape(ts, H_FOX, W_BRANCH)
    col_head = lax.shift_right_logical(lax.broadcasted_iota(jnp.int32, (H_FOX, W_BRANCH), 1),
                                       HEAD_DIM.bit_length() - 1)
    own = col_head == lax.broadcasted_iota(jnp.int32, (H_FOX, W_BRANCH), 0)
    o_ref[...] = jnp.sum(jnp.where(own[None], accn, 0.0), axis=1)


def _fox_decode(page_table, qb, kt, vt, lft, knt, vnt, cnb, cnt, layer, ts):
    db, rows, w = qb.shape
    page = kt.shape[-1]
    n_pages = page_table.shape[1]
    per_seq = lambda r, c: pl.BlockSpec((None, r, c), lambda b, pt: (b, 0, 0))
    grid_spec = pltpu.PrefetchScalarGridSpec(
        num_scalar_prefetch=1,
        grid=(db,),
        in_specs=([per_seq(rows, w)] + _page_specs(layer, n_pages, w, page) * 2
                  + _page_specs(layer, n_pages, H_FOX, page)
                  + [per_seq(w, page), per_seq(w, page), per_seq(rows, page), per_seq(H_FOX, page)]),
        out_specs=per_seq(ts, w))
    return pl.pallas_call(
        functools.partial(_fox_dec_body, n_pages=n_pages, ts=ts),
        grid_spec=grid_spec,
        out_shape=jax.ShapeDtypeStruct((db, ts, w), F32),
        compiler_params=_cparams(1, 40),
        name="fox_decode",
    )(page_table, qb, *([kt] * n_pages), *([vt] * n_pages), *([lft] * n_pages), knt, vnt, cnb, cnt)


def _diff_dec_body(pt_ref, tab_ref, q_ref, *refs, n_pages, ts, page, lam_init):
    kp = refs[:n_pages]
    vp = refs[n_pages:2 * n_pages]
    (kn_ref, vn_ref, lq1_ref, lk1_ref, lq2_ref, lk2_ref, g_ref, o_ref, bias_ref) = refs[2 * n_pages:]
    rows = 2 * H_DIFF * ts
    cols = page * H_DIFF
    past = n_pages * page

    @pl.when(pl.program_id(0) == 0)
    def _():
        lane = lax.broadcasted_iota(jnp.int32, (ts, cols), 1)
        pos = lax.shift_right_logical(lane, H_DIFF.bit_length() - 1)
        head = lane & (H_DIFF - 1)
        qpos = past + lax.broadcasted_iota(jnp.int32, (ts, cols), 0)

        def fill(tile, kpos, valid):
            bucket = _t5_bucket(jnp.maximum(qpos - kpos, 0))
            for rb in range(2 * H_DIFF):
                h, mp = rb // 2, rb % 2
                val = _t5_lookup(bucket, tab_ref, mp * H_DIFF + h)
                bias_ref[tile, rb * ts:(rb + 1) * ts, :] = jnp.where(valid & (head == h), val, NEG_INF)

        def past_tile(tile, carry):
            kpos = tile * page + pos
            fill(tile, kpos, kpos <= qpos)
            return carry

        lax.fori_loop(0, n_pages, past_tile, 0)
        kpos = past + pos
        fill(n_pages, kpos, (kpos <= qpos) & (pos < ts))

    q = q_ref[...]
    sm = _Softmax(rows, 2 * HEAD_DIM)
    for p in range(n_pages + 1):
        k_ref, v_ref = (kp[p], vp[p]) if p < n_pages else (kn_ref, vn_ref)
        s = _dot_nt(q, k_ref[...].astype(BF16)) + bias_ref[p]
        sm.update(s, lambda pr, v_ref=v_ref: jnp.dot(pr, v_ref[...].astype(BF16),
                                                      preferred_element_type=F32))

    lam = _lambda(lq1_ref, lk1_ref, lq2_ref, lk2_ref, lam_init)
    on = sm.acc / sm.l
    for h in range(H_DIFF):
        o1 = on[2 * h * ts:(2 * h + 1) * ts, :]
        o2 = on[(2 * h + 1) * ts:(2 * h + 2) * ts, :]
        o_ref[:, h * 2 * HEAD_DIM:(h + 1) * 2 * HEAD_DIM] = _subln(o1 - lam * o2, g_ref[...], lam_init)


def _diff_decode(page_table, table, q, kp, vp, kn, vn, lq1, lk1, lq2, lk2, g, layer, ts, lam_init):
    db, rows, d = q.shape
    cols = kp.shape[2]
    page = cols // H_DIFF
    n_pages = page_table.shape[1]
    per_seq = lambda r, c: pl.BlockSpec((None, r, c), lambda b, pt: (b, 0, 0))
    vec = pl.BlockSpec((1, HEAD_DIM), lambda b, pt: (0, 0))
    grid_spec = pltpu.PrefetchScalarGridSpec(
        num_scalar_prefetch=1,
        grid=(db,),
        in_specs=([pl.BlockSpec(memory_space=pltpu.SMEM), per_seq(rows, d)]
                  + _page_specs(layer, n_pages, cols, d) * 2
                  + [per_seq(cols, d), per_seq(cols, d), vec, vec, vec, vec,
                     pl.BlockSpec((1, d), lambda b, pt: (0, 0))]),
        out_specs=per_seq(ts, H_DIFF * d),
        scratch_shapes=[pltpu.VMEM((n_pages + 1, rows, cols), F32)])
    return pl.pallas_call(
        functools.partial(_diff_dec_body, n_pages=n_pages, ts=ts, page=page, lam_init=lam_init),
        grid_spec=grid_spec,
        out_shape=jax.ShapeDtypeStruct((db, ts, H_DIFF * d), F32),
        compiler_params=_cparams(1, 40),
        name="diff_decode",
    )(page_table, table, q, *([kp] * n_pages), *([vp] * n_pages), kn, vn, lq1, lk1, lq2, lk2, g)


def _lru_body(lx_ref, lg_ref, c0_ref, h0_ref, cw_ref, cb_ref, wa_ref, ba_ref, wx_ref, bx_ref,
              lam_ref, o_ref, hn_ref, cin_ref, a_ref, u_ref, hs_ref, hc_ref, *, tc):
    @pl.when(pl.program_id(1) == 0)
    def _():
        cin_ref[0:SUBLANES, :] = jnp.zeros((SUBLANES, W_BRANCH), F32)
        cin_ref[SUBLANES - (CONV_W - 1):SUBLANES, :] = c0_ref[...]
        hc_ref[...] = h0_ref[...]

    lx = lx_ref[...]
    cin_ref[SUBLANES:SUBLANES + tc, :] = lx
    base = SUBLANES - (CONV_W - 1)
    acc = cw_ref[0:1, :] * cin_ref[base:base + tc, :]
    for j in range(1, CONV_W):
        acc = acc + cw_ref[j:j + 1, :] * cin_ref[base + j:base + j + tc, :]
    xc = cb_ref[...] + acc
    xb = xc.astype(BF16)
    r_gate = jax.nn.sigmoid(jnp.dot(xb, wa_ref[...], preferred_element_type=F32) + ba_ref[...])
    i_gate = jax.nn.sigmoid(jnp.dot(xb, wx_ref[...], preferred_element_type=F32) + bx_ref[...])
    log_a = -LRU_C * r_gate * _softplus(-lam_ref[...])
    a_ref[...] = jnp.exp(log_a)
    u_ref[...] = jnp.sqrt(1.0 - jnp.exp(2.0 * log_a)) * (i_gate * xc)

    def step(t, h):
        h = a_ref[pl.ds(t, 1), :] * h + u_ref[pl.ds(t, 1), :]
        hs_ref[pl.ds(t, 1), :] = h
        return h

    h = lax.fori_loop(0, tc, step, hc_ref[...], unroll=8)
    hc_ref[...] = h
    hn_ref[...] = h
    o_ref[...] = hs_ref[...] * jax.nn.gelu(lg_ref[...])
    cin_ref[0:SUBLANES, :] = lx[tc - SUBLANES:tc, :]


def _lru(proj, row0, nseq, t, conv0, h0, layer, cw, cb, wa, ba, wx, bx, lam):
    m_total = proj.shape[0]
    tc = _tile(t, 512, SUBLANES)
    nc = t // tc
    blk0 = row0 // tc
    col = lambda off: (lambda s, c: (blk0 + s * nc + c, off // W_BRANCH))
    vec = pl.BlockSpec((1, W_BRANCH), lambda s, c: (0, 0))
    mat = pl.BlockSpec((W_BRANCH, W_BRANCH), lambda s, c: (0, 0))
    return pl.pallas_call(
        functools.partial(_lru_body, tc=tc),
        grid=(nseq, nc),
        in_specs=[pl.BlockSpec((tc, W_BRANCH), col(OFF_LX)),
                  pl.BlockSpec((tc, W_BRANCH), col(OFF_LG)),
                  pl.BlockSpec((None, None, CONV_W - 1, W_BRANCH), lambda s, c: (layer, s, 0, 0)),
                  pl.BlockSpec((None, None, 1, W_BRANCH), lambda s, c: (layer, s, 0, 0)),
                  pl.BlockSpec((CONV_W, W_BRANCH), lambda s, c: (0, 0)),
                  vec, mat, vec, mat, vec, vec],
        out_specs=[pl.BlockSpec((tc, W_BRANCH), lambda s, c: (s * nc + c, 0)),
                   pl.BlockSpec((None, 1, W_BRANCH), lambda s, c: (s, 0, 0))],
        out_shape=[jax.ShapeDtypeStruct((nseq * t, W_BRANCH), F32),
                   jax.ShapeDtypeStruct((nseq, 1, W_BRANCH), F32)],
        scratch_shapes=[pltpu.VMEM((tc + SUBLANES, W_BRANCH), F32),
                        pltpu.VMEM((tc, W_BRANCH), F32), pltpu.VMEM((tc, W_BRANCH), F32),
                        pltpu.VMEM((tc, W_BRANCH), F32), pltpu.VMEM((1, W_BRANCH), F32)],
        compiler_params=_cparams(2, 32),
        name="lru",
    )(proj, proj, conv0, h0, cw, cb, wa, ba, wx, bx, lam)


def _rwkv_prep_body(rr_ref, rk_ref, rv_ref, lo_ref, sh_ref, mu_ref, w0_ref, w2_ref, a0_ref, a2_ref,
                    g2_ref, kk_ref, ka_ref, rkk_ref, seg_ref,
                    r_o, w_o, k_o, v_o, a_o, b_o, g_o, bon_o, xs_ref, *, tc):
    wb = W_BRANCH

    @pl.when(pl.program_id(1) == 0)
    def _():
        xs_ref[0:SUBLANES, :] = jnp.zeros((SUBLANES, W_RWKV_PROJ), F32)
        xs_ref[SUBLANES - 1:SUBLANES, :] = sh_ref[...]

    xs_ref[SUBLANES:SUBLANES + tc, 0:wb] = rr_ref[...]
    xs_ref[SUBLANES:SUBLANES + tc, wb:2 * wb] = rk_ref[...]
    xs_ref[SUBLANES:SUBLANES + tc, 2 * wb:3 * wb] = rv_ref[...]
    xs_ref[SUBLANES:SUBLANES + tc, 3 * wb:W_RWKV_PROJ] = lo_ref[...]
    cur = xs_ref[SUBLANES:SUBLANES + tc, :]
    prev = xs_ref[SUBLANES - 1:SUBLANES - 1 + tc, :]
    rx = cur + (prev - cur) * mu_ref[...]
    xs_ref[0:SUBLANES, :] = cur[tc - SUBLANES:tc, :]

    rr = rx[:, 0:wb]
    rk = rx[:, wb:2 * wb]
    rv = rx[:, 2 * wb:3 * wb]
    wl_al = rx[:, 3 * wb:3 * wb + LANES]
    gl = rx[:, 3 * wb + LANES:W_RWKV_PROJ]
    dot = functools.partial(jnp.dot, preferred_element_type=F32)
    wd = -_softplus(-(w0_ref[...] + dot(jnp.tanh(wl_al).astype(BF16), w2_ref[...]))) - 0.5
    decay = jnp.exp(-jnp.exp(wd))
    aa = jax.nn.sigmoid(a0_ref[...] + dot(wl_al.astype(BF16), a2_ref[...]))
    gg = dot(jax.nn.sigmoid(gl).astype(BF16), g2_ref[...])
    kk = rk * kk_ref[...]
    norm = jnp.sqrt(_dot_sel_rhs(kk * kk, seg_ref[...]))
    kk = kk / jnp.maximum(norm, 1e-12)
    kh = rk * (1.0 + (aa - 1.0) * ka_ref[...])
    r_o[...] = rr
    w_o[...] = decay
    k_o[...] = kh
    v_o[...] = rv
    a_o[...] = -kk
    b_o[...] = kk * aa
    g_o[...] = gg
    bon_o[...] = _dot_sel_rhs(rr * kh * rkk_ref[...], seg_ref[...]) * rv


def _rwkv_prep(proj, row0, nseq, t, shift0, layer, mu, w0, w2p, a0, a2p, g2, k_k, k_a, r_k, seg):
    tc = _tile(t, 256, SUBLANES)
    nc = t // tc
    blk0 = row0 // tc
    wb = W_BRANCH
    col = lambda off, w: (lambda s, c: (blk0 + s * nc + c, off // w))
    vec = pl.BlockSpec((1, wb), lambda s, c: (0, 0))
    out_blk = pl.BlockSpec((tc, wb), lambda s, c: (s * nc + c, 0))
    lo_w = W_RWKV_PROJ - 3 * wb
    return pl.pallas_call(
        functools.partial(_rwkv_prep_body, tc=tc),
        grid=(nseq, nc),
        in_specs=[pl.BlockSpec((tc, wb), col(OFF_RW, wb)),
                  pl.BlockSpec((tc, wb), col(OFF_RW + wb, wb)),
                  pl.BlockSpec((tc, wb), col(OFF_RW + 2 * wb, wb)),
                  pl.BlockSpec((tc, lo_w), col(OFF_RW + 3 * wb, lo_w)),
                  pl.BlockSpec((None, None, 1, W_RWKV_PROJ), lambda s, c: (layer, s, 0, 0)),
                  pl.BlockSpec((1, W_RWKV_PROJ), lambda s, c: (0, 0)),
                  vec, pl.BlockSpec((LANES, wb), lambda s, c: (0, 0)),
                  vec, pl.BlockSpec((LANES, wb), lambda s, c: (0, 0)),
                  pl.BlockSpec((LANES, wb), lambda s, c: (0, 0)),
                  vec, vec, vec, pl.BlockSpec((wb, wb), lambda s, c: (0, 0))],
        out_specs=[out_blk] * 8,
        out_shape=[jax.ShapeDtypeStruct((nseq * t, wb), F32)] * 8,
        scratch_shapes=[pltpu.VMEM((tc + SUBLANES, W_RWKV_PROJ), F32)],
        compiler_params=_cparams(2, 40),
        name="rwkv_prep",
    )(proj, proj, proj, proj, shift0, mu, w0, w2p, a0, a2p, g2, k_k, k_a, r_k, seg)


def _rwkv_scan_body(r_ref, w_ref, k_ref, v_ref, a_ref, b_ref, s0_ref, yt_ref, sn_ref, st_ref,
                    *, nb, tc, n_chunks):
    c = pl.program_id(1)

    @pl.when(c == 0)
    def _():
        st_ref[...] = s0_ref[...]

    yt_ref[...] = jnp.zeros_like(yt_ref)
    lane = lax.broadcasted_iota(jnp.int32, (HEAD_DIM, tc), 1)
    eye = jnp.where(lax.broadcasted_iota(jnp.int32, (HEAD_DIM, HEAD_DIM), 0)
                    == lax.broadcasted_iota(jnp.int32, (HEAD_DIM, HEAD_DIM), 1), 1.0, 0.0)
    ones_k = jnp.ones((HEAD_DIM, HEAD_DIM), BF16)
    ones_y = jnp.ones((HEAD_DIM, max(tc, LANES)), BF16)

    def step(t, states):
        new_states = []
        for i, st in enumerate(states):
            s, h = divmod(i, H_RWKV)
            row = lambda ref: ref[s, t, pl.ds(h, 1), :]
            sa = jnp.dot((st * row(a_ref)).astype(BF16), ones_k, preferred_element_type=F32)
            vcol = jnp.sum(eye * row(v_ref), axis=1, keepdims=True)
            st = st * row(w_ref) + sa * row(b_ref) + vcol * row(k_ref)
            y = jnp.dot((st * row(r_ref)).astype(BF16), ones_y, preferred_element_type=F32)
            yt_ref[s, h] = jnp.where(lane == t, y[:, :tc], yt_ref[s, h])
            new_states.append(st)
        return tuple(new_states)

    init = tuple(st_ref[s, h] for s in range(nb) for h in range(H_RWKV))
    final = lax.fori_loop(0, tc, step, init, unroll=2)
    for i, st in enumerate(final):
        st_ref[i // H_RWKV, i % H_RWKV] = st

    @pl.when(c == n_chunks - 1)
    def _():
        sn_ref[...] = st_ref[...]


def _rwkv_scan(r, w, k, v, a, b, s0, layer):
    nseq, t, h, n = r.shape
    nb = 2 if nseq % 2 == 0 else 1
    tc = _tile(t, LANES, LANES)
    n_chunks = t // tc
    blk = pl.BlockSpec((nb, tc, h, n), lambda s, c: (s, c, 0, 0))
    st_blk = pl.BlockSpec((nb, h, n, n), lambda s, c: (s, 0, 0, 0))
    return pl.pallas_call(
        functools.partial(_rwkv_scan_body, nb=nb, tc=tc, n_chunks=n_chunks),
        grid=(nseq // nb, n_chunks),
        in_specs=[blk] * 6 + [pl.BlockSpec((None, nb, h, n, n), lambda s, c: (layer, s, 0, 0, 0))],
        out_specs=[pl.BlockSpec((nb, h, n, tc), lambda s, c: (s, 0, 0, c)), st_blk],
        out_shape=[jax.ShapeDtypeStruct((nseq, h, n, t), F32),
                   jax.ShapeDtypeStruct((nseq, h, n, n), F32)],
        scratch_shapes=[pltpu.VMEM((nb, h, n, n), F32)],
        compiler_params=_cparams(2, 40),
        name="rwkv_scan",
    )(r, w, k, v, a, b, s0)


def _rwkv_post_body(y_ref, g_ref, bon_ref, lg_ref, lb_ref, seg_ref, o_ref):
    y = y_ref[...]
    inv_n = 1.0 / HEAD_DIM
    mu = _dot_sel_rhs(y, seg_ref[...]) * inv_n
    d = y - mu
    var = _dot_sel_rhs(d * d, seg_ref[...]) * inv_n
    yn = d * lax.rsqrt(var + GN_EPS) * lg_ref[...] + lb_ref[...]
    o_ref[...] = (yn + bon_ref[...]) * g_ref[...]


def _rwkv_post(y, gg, bonus, lnx_g, lnx_b, seg):
    m, wb = y.shape
    tm = _tile(m, 512, SUBLANES)
    blk = pl.BlockSpec((tm, wb), lambda i: (i, 0))
    vec = pl.BlockSpec((1, wb), lambda i: (0, 0))
    return pl.pallas_call(
        _rwkv_post_body,
        grid=(m // tm,),
        in_specs=[blk, blk, blk, vec, vec, pl.BlockSpec((wb, wb), lambda i: (0, 0))],
        out_specs=blk,
        out_shape=jax.ShapeDtypeStruct((m, wb), F32),
        compiler_params=_cparams(1, 24),
        name="rwkv_post",
    )(y, gg, bonus, lnx_g, lnx_b, seg)


def _block_diag(blocks):
    n, bi, bj = blocks.shape
    eye = jnp.eye(n, dtype=blocks.dtype)
    return jnp.einsum('nij,nm->nimj', blocks, eye).reshape(n * bi, n * bj)


def _heads_first(x, b, t, h, d):
    return x.reshape(b, t, h, d).transpose(0, 2, 1, 3)


def kernel(x_prompt, x_sample, cache_fox_k, cache_fox_v, cache_fox_logf, cache_diff_k, cache_diff_v, state_lru_conv, state_lru_h, state_rwkv_shift, state_rwkv_wkv, page_table, norm1_g, w_in, fox_fb, lam_q1, lam_k1, lam_q2, lam_k2, diff_subln_g, t5_table, lru_conv_w, lru_conv_b, lru_wa, lru_ba, lru_wx, lru_bx, lru_lambda, rwkv_mu, rwkv_w0, rwkv_w2, rwkv_a0, rwkv_a2, rwkv_g2, rwkv_k_k, rwkv_k_a, rwkv_r_k, rwkv_lnx_g, rwkv_lnx_b, wb_fox, wb_diff, wb_lru, wb_rwkv, w_out, norm2_g, w_up, w_down, final_g):
    B, T, D = x_prompt.shape
    DB, TS, _ = x_sample.shape
    L = w_in.shape[0]
    n_pool, page = cache_fox_k.shape[1], cache_fox_k.shape[2]
    n_pages = page_table.shape[1]
    mp, ms = B * T, DB * TS
    wb = W_BRANCH
    assert TS % SUBLANES == 0 and TS <= page and T % SUBLANES == 0

    x = jnp.concatenate([x_prompt.reshape(mp, D), x_sample.reshape(ms, D)], axis=0)
    kt_fox = cache_fox_k.transpose(0, 1, 3, 4, 2).reshape(L, n_pool, wb, page)
    vt_fox = cache_fox_v.transpose(0, 1, 3, 4, 2).reshape(L, n_pool, wb, page)
    lft_fox = cache_fox_logf.transpose(0, 1, 3, 2)
    kp_diff = cache_diff_k.reshape(L, n_pool, page * H_DIFF, 2 * HEAD_DIM)
    vp_diff = cache_diff_v.reshape(L, n_pool, page * H_DIFF, 2 * HEAD_DIM)
    seg = _block_diag(jnp.ones((H_RWKV, HEAD_DIM, HEAD_DIM), BF16))
    zeros_conv = jnp.zeros((L, B, CONV_W - 1, wb), F32)
    zeros_h = jnp.zeros((L, B, 1, wb), F32)
    zeros_shift = jnp.zeros((L, B, 1, W_RWKV_PROJ), F32)
    zeros_wkv = jnp.zeros((L, B, H_RWKV, HEAD_DIM, HEAD_DIM), F32)
    eye_f = jnp.eye(H_FOX, dtype=F32)
    eye_d = jnp.eye(H_DIFF, dtype=F32)
    eye_2 = jnp.eye(2, dtype=F32)
    row = lambda v: v.reshape(1, -1)

    def pad_rows(z, n):
        return jnp.pad(z, ((0, 0), (0, n - z.shape[1]), (0, 0)))

    def pad_lanes(z):
        return jnp.pad(z, ((0, 0), (0, 0), (0, page - z.shape[2])))

    p_states, s_states = [], []
    y_norm = None
    for l in range(L):
        lam_init = 0.8 - 0.6 * math.exp(-0.3 * l)
        w = w_in[l]
        n_a = OFF_FF
        split_ff = 3 * wb
        w_r = jnp.concatenate(
            [w[:, :split_ff], w[:, split_ff + H_FOX:split_ff + H_FOX + (n_a - split_ff)],
             w[:, split_ff:split_ff + H_FOX], jnp.zeros((D, FF_PAD), F32),
             w[:, split_ff + H_FOX + (n_a - split_ff):]], axis=1).astype(BF16)
        proj = _rms_matmul(x, row(norm1_g[l]), w_r)
        pp, ps = proj[:mp], proj[mp:]

        lf_p, c_p = _logf_cumsum(pp[:, OFF_FF:OFF_FF + H_FOX].reshape(B, T, H_FOX), row(fox_fb[l]),
                                 jnp.zeros((B, 1, H_FOX), F32))
        lf_s, cn = _logf_cumsum(ps[:, OFF_FF:OFF_FF + H_FOX].reshape(DB, TS, H_FOX), row(fox_fb[l]),
                                jnp.zeros((DB, 1, H_FOX), F32))

        fq = pp[:, OFF_FQ:OFF_FQ + wb].reshape(B, T, H_FOX, HEAD_DIM)
        fk = pp[:, OFF_FK:OFF_FK + wb].reshape(B, T, H_FOX, HEAD_DIM)
        fv = pp[:, OFF_FV:OFF_FV + wb].reshape(B, T, H_FOX, HEAD_DIM)
        c1, c2, c3 = (z[..., None] for z in _split3(c_p))
        ones = jnp.ones((B, T, H_FOX, 3), BF16)
        zpad = jnp.zeros((B, T, H_FOX, 2 * HEAD_DIM - HEAD_DIM - 6), BF16)
        qa = jnp.concatenate([(fq * QK_SCALE).astype(BF16), c1, c2, c3, ones, zpad], axis=-1)
        ka = jnp.concatenate([fk.astype(BF16), ones, -c1, -c2, -c3, zpad], axis=-1)
        o_fox_p = _fox_flash(qa.transpose(0, 2, 1, 3), ka.transpose(0, 2, 1, 3),
                             fv.astype(BF16).transpose(0, 2, 1, 3))
        o_fox_p = o_fox_p.transpose(0, 2, 1, 3).reshape(mp, wb)

        fq_s = ps[:, OFF_FQ:OFF_FQ + wb].reshape(DB, TS, H_FOX, HEAD_DIM) * QK_SCALE
        qb = jnp.einsum('bthd,hg->bthgd', fq_s, eye_f).reshape(DB, TS * H_FOX, wb).astype(BF16)
        cnb = jnp.broadcast_to(cn.reshape(DB, TS * H_FOX, 1), (DB, TS * H_FOX, page))
        cnt = pad_lanes(cn.transpose(0, 2, 1))
        knt = pad_lanes(ps[:, OFF_FK:OFF_FK + wb].reshape(DB, TS, wb).transpose(0, 2, 1))
        vnt = pad_lanes(ps[:, OFF_FV:OFF_FV + wb].reshape(DB, TS, wb).transpose(0, 2, 1))
        o_fox_s = _fox_decode(page_table, qb, kt_fox, vt_fox, lft_fox, knt, vnt, cnb, cnt, l, TS).reshape(ms, wb)

        lam_vecs = (row(lam_q1[l]), row(lam_k1[l]), row(lam_q2[l]), row(lam_k2[l]))
        g_sub = row(diff_subln_g[l])
        dq = _heads_first(pp[:, OFF_DQ:OFF_DQ + wb] * QK_SCALE, B, T, H_DIFF, 2 * HEAD_DIM).astype(BF16)
        dk = _heads_first(pp[:, OFF_DK:OFF_DK + wb], B, T, H_DIFF, 2 * HEAD_DIM).astype(BF16)
        dv = _heads_first(pp[:, OFF_DV:OFF_DV + wb], B, T, H_DIFF, 2 * HEAD_DIM).astype(BF16)
        o_diff_p = _diff_flash(t5_table, dq, dk, dv, *lam_vecs, g_sub, lam_init)
        o_diff_p = o_diff_p.transpose(0, 2, 1, 3).reshape(mp, wb)

        dq_s = ps[:, OFF_DQ:OFF_DQ + wb].reshape(DB, TS, H_DIFF, 2, HEAD_DIM) * QK_SCALE
        qb_d = jnp.einsum('bthmd,mn->bhmtnd', dq_s, eye_2).reshape(
            DB, 2 * H_DIFF * TS, 2 * HEAD_DIM).astype(BF16)
        kn_d = pad_rows(ps[:, OFF_DK:OFF_DK + wb].reshape(DB, TS * H_DIFF, 2 * HEAD_DIM), page * H_DIFF)
        vn_d = pad_rows(ps[:, OFF_DV:OFF_DV + wb].reshape(DB, TS * H_DIFF, 2 * HEAD_DIM), page * H_DIFF)
        o_diff_s = _diff_decode(page_table, t5_table, qb_d, kp_diff, vp_diff, kn_d, vn_d,
                                *lam_vecs, g_sub, l, TS, lam_init).reshape(ms, wb)

        lru_w = (lru_conv_w[l], row(lru_conv_b[l]), _block_diag(lru_wa[l]).astype(BF16), row(lru_ba[l]),
                 _block_diag(lru_wx[l]).astype(BF16), row(lru_bx[l]), row(lru_lambda[l]))
        o_lru_p, h_p = _lru(proj, 0, B, T, zeros_conv, zeros_h, l, *lru_w)
        o_lru_s, h_s = _lru(proj, mp, DB, TS, state_lru_conv, state_lru_h.reshape(L, DB, 1, wb), l, *lru_w)

        rank = rwkv_w2.shape[1]
        w2p = jnp.concatenate([rwkv_w2[l], jnp.zeros((LANES - rank, wb), F32)], axis=0).astype(BF16)
        a2p = jnp.concatenate([jnp.zeros((LANES - rwkv_a2.shape[1], wb), F32), rwkv_a2[l]], axis=0).astype(BF16)
        rw_w = (row(rwkv_mu[l]), row(rwkv_w0[l]), w2p, row(rwkv_a0[l]), a2p, rwkv_g2[l].astype(BF16),
                row(rwkv_k_k[l]), row(rwkv_k_a[l]), row(rwkv_r_k[l]), seg)
        prep_p = _rwkv_prep(proj, 0, B, T, zeros_shift, l, *rw_w)
        prep_s = _rwkv_prep(proj, mp, DB, TS, state_rwkv_shift.reshape(L, DB, 1, W_RWKV_PROJ), l, *rw_w)
        hs = lambda z, n, t: z.reshape(n, t, H_RWKV, HEAD_DIM)
        yt_p, wkv_p = _rwkv_scan(*(hs(z, B, T) for z in prep_p[:6]), zeros_wkv, l)
        yt_s, wkv_s = _rwkv_scan(*(hs(z, DB, TS) for z in prep_s[:6]), state_rwkv_wkv, l)
        y_rw = jnp.concatenate([yt_p.transpose(0, 3, 1, 2).reshape(mp, wb),
                                yt_s.transpose(0, 3, 1, 2).reshape(ms, wb)], axis=0)
        gg = jnp.concatenate([prep_p[6], prep_s[6]], axis=0)
        bonus = jnp.concatenate([prep_p[7], prep_s[7]], axis=0)
        o_rwkv = _rwkv_post(y_rw, gg, bonus, row(rwkv_lnx_g[l]), row(rwkv_lnx_b[l]), seg)

        outs = (jnp.concatenate([o_fox_p, o_fox_s], axis=0), jnp.concatenate([o_diff_p, o_diff_s], axis=0),
                jnp.concatenate([o_lru_p, o_lru_s], axis=0), o_rwkv)
        wbs = (wb_fox[l].astype(BF16), wb_diff[l].astype(BF16), wb_lru[l].astype(BF16), wb_rwkv[l].astype(BF16))
        mixed = _merge(outs, wbs, proj, D)
        x = _matmul_res(mixed, w_out[l].astype(BF16), x)
        x, y_norm = _mlp(x, row(norm2_g[l]), w_up[l].astype(BF16), w_down[l].astype(BF16), row(final_g))

        def states(pz, n, t, lf, conv0, h_last, wkv):
            lx = pz[:, OFF_LX:OFF_LX + wb].reshape(n, t, wb)
            conv_in = jnp.concatenate([conv0, lx], axis=1)
            rw = pz[:, OFF_RW:OFF_RW + W_RWKV_PROJ].reshape(n, t, W_RWKV_PROJ)
            return (pz[:, OFF_FK:OFF_FK + wb].reshape(n, t, H_FOX, HEAD_DIM),
                    pz[:, OFF_FV:OFF_FV + wb].reshape(n, t, H_FOX, HEAD_DIM),
                    lf,
                    pz[:, OFF_DK:OFF_DK + wb].reshape(n, t, H_DIFF, 2 * HEAD_DIM),
                    pz[:, OFF_DV:OFF_DV + wb].reshape(n, t, H_DIFF, 2 * HEAD_DIM),
                    conv_in[:, -(CONV_W - 1):], h_last.reshape(n, wb), rw[:, -1], wkv)

        p_states.append(states(pp, B, T, lf_p, zeros_conv[0], h_p, wkv_p))
        s_states.append(states(ps, DB, TS, lf_s, state_lru_conv[l], h_s, wkv_s))

    y_prompt = y_norm[:mp].reshape(B, T, D)
    y_sample = y_norm[mp:].reshape(DB, TS, D)
    p_out = [jnp.stack(z) for z in zip(*p_states)]
    s_out = [jnp.stack(z) for z in zip(*s_states)]
    return (y_prompt, y_sample, *p_out, *s_out)
```

```python
import functools
import math

import jax
import jax.numpy as jnp
from jax import lax
from jax.experimental import pallas as pl
from jax.experimental.pallas import tpu as pltpu

F32 = jnp.float32
BF16 = jnp.bfloat16

HEAD_DIM = 64
W_BRANCH = 512
H_FOX = 8
H_DIFF = 4
H_RWKV = 8
CONV_W = 4
LRU_C = 8.0
NUM_BUCKETS = 32
MAX_DISTANCE = 128
RMS_EPS = 1e-6
SUBLN_EPS = 1e-5
GN_EPS = 64e-5
NEG_INF = -1e30
QK_SCALE = HEAD_DIM ** -0.5

LANES = 128
SUBLANES = 8
VMEM_LIMIT_CAP = 56 * 1024 * 1024

OFF_FQ, OFF_FK, OFF_FV = 0, 512, 1024
OFF_DQ, OFF_DK, OFF_DV = 1536, 2048, 2560
OFF_LX, OFF_LG = 3072, 3584
OFF_RW = 4096
W_RWKV_PROJ = 1792
OFF_FF = 5888
OFF_GATE = 6144
FF_PAD = OFF_GATE - OFF_FF - H_FOX


def _tile(n, pref, mult):
    best = None
    t = mult
    while t <= min(n, pref):
        if n % t == 0:
            best = t
        t += mult
    return best if best is not None else n


def _cparams(n_axes, vmem_mb):
    return pltpu.CompilerParams(
        dimension_semantics=("arbitrary",) * n_axes,
        vmem_limit_bytes=min(int(vmem_mb * 1024 * 1024), VMEM_LIMIT_CAP))


def _softplus(x):
    return jnp.maximum(x, 0.0) + jnp.log1p(jnp.exp(-jnp.abs(x)))


def _split3(x):
    def top(v):
        bits = lax.bitcast_convert_type(v, jnp.uint32) & jnp.uint32(0xFFFF0000)
        return lax.bitcast_convert_type(bits, F32)

    x1 = top(x)
    r1 = x - x1
    x2 = top(r1)
    x3 = r1 - x2
    return x1.astype(BF16), x2.astype(BF16), x3.astype(BF16)


def _dot_sel_rhs(x, sel):
    x1, x2, x3 = _split3(x)
    d = functools.partial(jnp.dot, preferred_element_type=F32)
    return d(x1, sel) + d(x2, sel) + d(x3, sel)


def _dot_sel_lhs(sel, x):
    x1, x2, x3 = _split3(x)
    d = functools.partial(jnp.dot, preferred_element_type=F32)
    return d(sel, x1) + d(sel, x2) + d(sel, x3)


def _dot_nt(a, b):
    return lax.dot_general(a, b, (((1,), (1,)), ((), ())), preferred_element_type=F32)


def _rms_matmul_body(x_ref, g_ref, w_ref, o_ref, xn_ref):
    @pl.when(pl.program_id(1) == 0)
    def _():
        x = x_ref[...]
        ms = jnp.mean(x * x, axis=-1, keepdims=True)
        xn_ref[...] = (x * lax.rsqrt(ms + RMS_EPS) * g_ref[...]).astype(BF16)

    o_ref[...] = jnp.dot(xn_ref[...], w_ref[...], preferred_element_type=F32)


def _rms_matmul(x, g, w):
    m, k = x.shape
    n = w.shape[1]
    tm = _tile(m, 1024, SUBLANES)
    tn = _tile(n, 1024, LANES)
    vmem = (2 * tm * k * 4 + 2 * k * tn * 2 + 2 * tm * tn * 4 + tm * k * 2) / 2**20 + 8
    return pl.pallas_call(
        _rms_matmul_body,
        grid=(m // tm, n // tn),
        in_specs=[pl.BlockSpec((tm, k), lambda i, j: (i, 0)),
                  pl.BlockSpec((1, k), lambda i, j: (0, 0)),
                  pl.BlockSpec((k, tn), lambda i, j: (0, j))],
        out_specs=pl.BlockSpec((tm, tn), lambda i, j: (i, j)),
        out_shape=jax.ShapeDtypeStruct((m, n), F32),
        scratch_shapes=[pltpu.VMEM((tm, k), BF16)],
        compiler_params=_cparams(2, vmem),
        name="in_proj",
    )(x, g, w)


def _merge_body(of_ref, od_ref, ol_ref, or_ref, wf_ref, wd_ref, wl_ref, wr_ref,
                gf_ref, gd_ref, gl_ref, gr_ref, o_ref):
    def branch(o_r, w_r, g_r):
        y = jnp.dot(o_r[...].astype(BF16), w_r[...], preferred_element_type=F32)
        return jax.nn.sigmoid(g_r[...]) * y

    acc = branch(of_ref, wf_ref, gf_ref)
    acc = acc + branch(od_ref, wd_ref, gd_ref)
    acc = acc + branch(ol_ref, wl_ref, gl_ref)
    acc = acc + branch(or_ref, wr_ref, gr_ref)
    o_ref[...] = acc.astype(BF16)


def _merge(outs, wbs, proj, d_model):
    m, w = outs[0].shape
    tm = _tile(m, 512, 16)
    tn = _tile(d_model, 512, LANES)
    o_specs = [pl.BlockSpec((tm, w), lambda i, j: (i, 0)) for _ in range(4)]
    w_specs = [pl.BlockSpec((w, tn), lambda i, j: (0, j)) for _ in range(4)]
    g_specs = [pl.BlockSpec((tm, tn), functools.partial(
        lambda i, j, base: (i, base + j), base=(OFF_GATE + b * d_model) // tn)) for b in range(4)]
    return pl.pallas_call(
        _merge_body,
        grid=(m // tm, d_model // tn),
        in_specs=o_specs + w_specs + g_specs,
        out_specs=pl.BlockSpec((tm, tn), lambda i, j: (i, j)),
        out_shape=jax.ShapeDtypeStruct((m, d_model), BF16),
        compiler_params=_cparams(2, 40),
        name="merge",
    )(*outs, *wbs, proj, proj, proj, proj)


def _matmul_res_body(a_ref, w_ref, r_ref, o_ref):
    o_ref[...] = r_ref[...] + jnp.dot(a_ref[...], w_ref[...], preferred_element_type=F32)


def _matmul_res(a, w, res):
    m, k = a.shape
    n = w.shape[1]
    tm = _tile(m, 512, 16)
    tn = _tile(n, 1024, LANES)
    return pl.pallas_call(
        _matmul_res_body,
        grid=(m // tm, n // tn),
        in_specs=[pl.BlockSpec((tm, k), lambda i, j: (i, 0)),
                  pl.BlockSpec((k, tn), lambda i, j: (0, j)),
                  pl.BlockSpec((tm, tn), lambda i, j: (i, j))],
        out_specs=pl.BlockSpec((tm, tn), lambda i, j: (i, j)),
        out_shape=jax.ShapeDtypeStruct((m, n), F32),
        compiler_params=_cparams(2, 40),
        name="out_proj",
    )(a, w, res)


def _mlp_body(x_ref, g_ref, wu_ref, wd_ref, fg_ref, o_ref, y_ref, xn_ref, acc_ref, *, n_chunks):
    c = pl.program_id(1)

    @pl.when(c == 0)
    def _():
        x = x_ref[...]
        ms = jnp.mean(x * x, axis=-1, keepdims=True)
        xn_ref[...] = (x * lax.rsqrt(ms + RMS_EPS) * g_ref[...]).astype(BF16)
        acc_ref[...] = jnp.zeros_like(acc_ref)

    h = jnp.dot(xn_ref[...], wu_ref[...], preferred_element_type=F32)
    h = jnp.square(jnp.maximum(h, 0.0))
    acc_ref[...] += jnp.dot(h.astype(BF16), wd_ref[...], preferred_element_type=F32)

    @pl.when(c == n_chunks - 1)
    def _():
        xo = x_ref[...] + acc_ref[...]
        o_ref[...] = xo
        ms = jnp.mean(xo * xo, axis=-1, keepdims=True)
        y_ref[...] = xo * lax.rsqrt(ms + RMS_EPS) * fg_ref[...]


def _mlp(x, g, wu, wd, final_g):
    m, d = x.shape
    f = wu.shape[1]
    tm = _tile(m, 512, SUBLANES)
    tc = _tile(f, 512, LANES)
    n_chunks = f // tc
    vmem = (2 * tm * d * 4 + 4 * d * tc * 2 + 4 * tm * d * 4 + tm * d * 2 + tm * d * 4) / 2**20 + 10
    return pl.pallas_call(
        functools.partial(_mlp_body, n_chunks=n_chunks),
        grid=(m // tm, n_chunks),
        in_specs=[pl.BlockSpec((tm, d), lambda i, c: (i, 0)),
                  pl.BlockSpec((1, d), lambda i, c: (0, 0)),
                  pl.BlockSpec((d, tc), lambda i, c: (0, c)),
                  pl.BlockSpec((tc, d), lambda i, c: (c, 0)),
                  pl.BlockSpec((1, d), lambda i, c: (0, 0))],
        out_specs=[pl.BlockSpec((tm, d), lambda i, c: (i, 0)),
                   pl.BlockSpec((tm, d), lambda i, c: (i, 0))],
        out_shape=[jax.ShapeDtypeStruct((m, d), F32), jax.ShapeDtypeStruct((m, d), F32)],
        scratch_shapes=[pltpu.VMEM((tm, d), BF16), pltpu.VMEM((tm, d), F32)],
        compiler_params=_cparams(2, vmem),
        name="mlp",
    )(x, g, wu, wd, final_g)


def _cumsum_rows(x, tc):
    if tc >= LANES:
        r = lax.broadcasted_iota(jnp.int32, (tc, tc), 0)
        c = lax.broadcasted_iota(jnp.int32, (tc, tc), 1)
        tri = jnp.where(c <= r, 1.0, 0.0).astype(BF16)
        return _dot_sel_lhs(tri, x)
    rows = lax.broadcasted_iota(jnp.int32, x.shape, 0)
    parts = [jnp.sum(jnp.where(rows <= t, x, 0.0), axis=0, keepdims=True) for t in range(tc)]
    return jnp.concatenate(parts, axis=0)


def _logf_cumsum_body(ff_ref, fb_ref, c0_ref, lf_ref, c_ref, carry_ref, *, tc):
    @pl.when(pl.program_id(1) == 0)
    def _():
        carry_ref[...] = c0_ref[...]

    lf = -_softplus(-(ff_ref[...] + fb_ref[...]))
    lf_ref[...] = lf
    cs = _cumsum_rows(lf, tc) + carry_ref[...]
    c_ref[...] = cs
    carry_ref[...] = cs[tc - 1:tc, :]


def _logf_cumsum(ff, fb, c0):
    nseq, t, h = ff.shape
    tc = _tile(t, 512, SUBLANES)
    blk = pl.BlockSpec((None, tc, h), lambda s, c: (s, c, 0))
    return pl.pallas_call(
        functools.partial(_logf_cumsum_body, tc=tc),
        grid=(nseq, t // tc),
        in_specs=[blk, pl.BlockSpec((1, h), lambda s, c: (0, 0)),
                  pl.BlockSpec((None, 1, h), lambda s, c: (s, 0, 0))],
        out_specs=[blk, blk],
        out_shape=[jax.ShapeDtypeStruct((nseq, t, h), F32)] * 2,
        scratch_shapes=[pltpu.VMEM((1, h), F32)],
        compiler_params=_cparams(2, 24),
        name="logf_cumsum",
    )(ff, fb, c0)


def _t5_bucket(n):
    max_exact = NUM_BUCKETS // 2
    nf = jnp.maximum(n, 1).astype(F32)
    large = max_exact + (jnp.log(nf / max_exact) / math.log(MAX_DISTANCE / max_exact)
                         * (NUM_BUCKETS - max_exact)).astype(jnp.int32)
    return jnp.where(n < max_exact, n, jnp.minimum(large, NUM_BUCKETS - 1))


def _t5_lookup(bucket, tab_ref, col):
    acc = jnp.zeros(bucket.shape, F32)
    for j in range(NUM_BUCKETS):
        acc = jnp.where(bucket == j, tab_ref[j, col], acc)
    return acc


def _softmax_update(s, v, m_ref, l_ref, acc_ref, idx):
    m_prev = m_ref[idx]
    m_new = jnp.maximum(m_prev, jnp.max(s, axis=-1, keepdims=True))
    alpha = jnp.exp(m_prev - m_new)
    p = jnp.exp(s - pltpu.repeat(m_new, s.shape[-1] // LANES, axis=1))
    l_ref[idx] = alpha * l_ref[idx] + jnp.sum(p, axis=-1, keepdims=True)
    acc_ref[idx] = (alpha[:, :v.shape[-1]] * acc_ref[idx]
                    + jnp.dot(p.astype(BF16), v, preferred_element_type=F32))
    m_ref[idx] = m_new


def _fox_flash_body(q_ref, k_ref, v_ref, c_ref, ct_ref, o_ref, m_ref, l_ref, acc_ref, *, tq):
    qi = pl.program_id(1)
    ki = pl.program_id(2)

    @pl.when(ki == 0)
    def _():
        m_ref[...] = jnp.full(m_ref.shape, NEG_INF, F32)
        l_ref[...] = jnp.zeros_like(l_ref)
        acc_ref[...] = jnp.zeros_like(acc_ref)

    first = lax.broadcasted_iota(jnp.int32, (tq, LANES), 1) < HEAD_DIM

    def tiles(masked):
        if masked:
            row = lax.broadcasted_iota(jnp.int32, (tq, tq), 0)
            col = lax.broadcasted_iota(jnp.int32, (tq, tq), 1)
            keep = col <= row
        for g in range(H_FOX // 2):
            grp = slice(g * LANES, (g + 1) * LANES)
            q2 = q_ref[:, grp] * QK_SCALE
            k2 = k_ref[:, grp].astype(BF16)
            v2 = v_ref[:, grp].astype(BF16)
            for j in range(2):
                h = 2 * g + j
                own = first if j == 0 else jnp.logical_not(first)
                s = _dot_nt(jnp.where(own, q2, 0.0).astype(BF16), k2)
                s = s + c_ref[:, h:h + 1] - ct_ref[h:h + 1, :]
                if masked:
                    s = jnp.where(keep, s, NEG_INF)
                m_prev = m_ref[h]
                m_new = jnp.maximum(m_prev, jnp.max(s, axis=-1, keepdims=True))
                alpha = jnp.exp(m_prev - m_new)
                p = jnp.exp(s - pltpu.repeat(m_new, tq // LANES, axis=1))
                l_ref[h] = alpha * l_ref[h] + jnp.sum(p, axis=-1, keepdims=True)
                pv = jnp.dot(p.astype(BF16), v2, preferred_element_type=F32)
                acc_ref[g] = jnp.where(own, alpha * acc_ref[g] + pv, acc_ref[g])
                m_ref[h] = m_new

    @pl.when(ki < qi)
    def _():
        tiles(False)

    @pl.when(ki == qi)
    def _():
        tiles(True)
        for g in range(H_FOX // 2):
            o_ref[:, g * LANES:(g + 1) * LANES] = acc_ref[g] / jnp.where(first, l_ref[2 * g], l_ref[2 * g + 1])


def _qkv_specs(t, tq, col0, w):
    nq = t // tq
    c = col0 // w
    return [pl.BlockSpec((tq, w), lambda bi, qi, ki: (bi * nq + qi, c)),
            pl.BlockSpec((tq, w), lambda bi, qi, ki: (bi * nq + jnp.minimum(ki, qi), c + 1)),
            pl.BlockSpec((tq, w), lambda bi, qi, ki: (bi * nq + jnp.minimum(ki, qi), c + 2))]


def _fox_flash(proj, c, ct, b, t):
    w = W_BRANCH
    tq = _tile(t, 512, LANES)
    nq = t // tq
    return pl.pallas_call(
        functools.partial(_fox_flash_body, tq=tq),
        grid=(b, nq, nq),
        in_specs=_qkv_specs(t, tq, OFF_FQ, w) + [
            pl.BlockSpec((None, tq, H_FOX), lambda bi, qi, ki: (bi, qi, 0)),
            pl.BlockSpec((None, H_FOX, tq), lambda bi, qi, ki: (bi, 0, jnp.minimum(ki, qi)))],
        out_specs=pl.BlockSpec((tq, w), lambda bi, qi, ki: (bi * nq + qi, 0)),
        out_shape=jax.ShapeDtypeStruct((b * t, w), F32),
        scratch_shapes=[pltpu.VMEM((H_FOX, tq, LANES), F32), pltpu.VMEM((H_FOX, tq, LANES), F32),
                        pltpu.VMEM((H_FOX // 2, tq, LANES), F32)],
        compiler_params=_cparams(3, 48),
        name="fox_flash",
    )(proj, proj, proj, c, ct)


def _lambda(lq1_ref, lk1_ref, lq2_ref, lk2_ref, lam_init):
    s1 = jnp.sum(lq1_ref[...] * lk1_ref[...], axis=-1, keepdims=True)
    s2 = jnp.sum(lq2_ref[...] * lk2_ref[...], axis=-1, keepdims=True)
    return jnp.exp(s1) - jnp.exp(s2) + lam_init


def _subln(o, g, lam_init):
    ms = jnp.mean(o * o, axis=-1, keepdims=True)
    return o * lax.rsqrt(ms + SUBLN_EPS) * g * (1.0 - lam_init)


def _diff_flash_body(tab_ref, q_ref, k_ref, v_ref, lq1_ref, lk1_ref, lq2_ref, lk2_ref, g_ref,
                     o_ref, bias_ref, m_ref, l_ref, acc_ref, *, tq, lam_init):
    bi = pl.program_id(0)
    qi = pl.program_id(1)
    ki = pl.program_id(2)

    @pl.when((bi == 0) & (qi == 0) & (ki == 0))
    def _():
        def fill(rb, carry):
            r0 = pl.multiple_of(rb * SUBLANES, SUBLANES)
            rows = r0 + lax.broadcasted_iota(jnp.int32, (SUBLANES, tq), 0)
            cols = lax.broadcasted_iota(jnp.int32, (SUBLANES, tq), 1)
            for which in range(2):
                bucket = _t5_bucket(jnp.maximum(rows - cols + which * tq, 0))
                for c in range(2 * H_DIFF):
                    bias_ref[c, which, pl.ds(r0, SUBLANES), :] = _t5_lookup(bucket, tab_ref, c)
            return carry
        lax.fori_loop(0, tq // SUBLANES, fill, 0)

    @pl.when(ki == 0)
    def _():
        m_ref[...] = jnp.full(m_ref.shape, NEG_INF, F32)
        l_ref[...] = jnp.zeros_like(l_ref)
        acc_ref[...] = jnp.zeros_like(acc_ref)

    def tiles(mode):
        lane = lax.broadcasted_iota(jnp.int32, (tq, 2 * HEAD_DIM), 1)
        if mode == 0:
            row = lax.broadcasted_iota(jnp.int32, (tq, tq), 0)
            col = lax.broadcasted_iota(jnp.int32, (tq, tq), 1)
            keep = col <= row
        for h in range(H_DIFF):
            grp = slice(h * 2 * HEAD_DIM, (h + 1) * 2 * HEAD_DIM)
            q = q_ref[:, grp] * QK_SCALE
            k = k_ref[:, grp].astype(BF16)
            v = v_ref[:, grp].astype(BF16)
            for mp in range(2):
                c = mp * H_DIFF + h
                qm = jnp.where((lane >= mp * HEAD_DIM) & (lane < (mp + 1) * HEAD_DIM), q, 0.0)
                s = _dot_nt(qm.astype(BF16), k)
                if mode == 0:
                    s = jnp.where(keep, s + bias_ref[c, 0], NEG_INF)
                elif mode == 1:
                    s = s + bias_ref[c, 1]
                else:
                    s = s + tab_ref[NUM_BUCKETS - 1, c]
                _softmax_update(s, v, m_ref, l_ref, acc_ref, c)

    @pl.when(ki < qi - 1)
    def _():
        tiles(2)

    @pl.when(ki == qi - 1)
    def _():
        tiles(1)

    @pl.when(ki == qi)
    def _():
        tiles(0)
        lam = _lambda(lq1_ref, lk1_ref, lq2_ref, lk2_ref, lam_init)
        for h in range(H_DIFF):
            o = acc_ref[h] / l_ref[h] - lam * (acc_ref[H_DIFF + h] / l_ref[H_DIFF + h])
            o_ref[:, h * 2 * HEAD_DIM:(h + 1) * 2 * HEAD_DIM] = _subln(o, g_ref[...], lam_init)


def _diff_flash(table, proj, lq1, lk1, lq2, lk2, g, lam_init, b, t):
    w = W_BRANCH
    d = 2 * HEAD_DIM
    h = H_DIFF
    tq = _tile(t, 512, LANES)
    assert tq >= MAX_DISTANCE or tq == t, "far blocks must lie in the last bucket"
    nq = t // tq
    vec = pl.BlockSpec((1, HEAD_DIM), lambda bi, qi, ki: (0, 0))
    return pl.pallas_call(
        functools.partial(_diff_flash_body, tq=tq, lam_init=lam_init),
        grid=(b, nq, nq),
        in_specs=[pl.BlockSpec(memory_space=pltpu.SMEM)] + _qkv_specs(t, tq, OFF_DQ, w) + [
            vec, vec, vec, vec, pl.BlockSpec((1, d), lambda bi, qi, ki: (0, 0))],
        out_specs=pl.BlockSpec((tq, w), lambda bi, qi, ki: (bi * nq + qi, 0)),
        out_shape=jax.ShapeDtypeStruct((b * t, w), F32),
        scratch_shapes=[pltpu.VMEM((2 * h, 2, tq, tq), F32),
                        pltpu.VMEM((2 * h, tq, LANES), F32), pltpu.VMEM((2 * h, tq, LANES), F32),
                        pltpu.VMEM((2 * h, tq, d), F32)],
        compiler_params=_cparams(3, 54),
        name="diff_flash",
    )(table, proj, proj, proj, lq1, lk1, lq2, lk2, g)


def _softmax_tiles(tiles):
    m = tiles[0][0].max(axis=-1, keepdims=True)
    for s, _ in tiles[1:]:
        m = jnp.maximum(m, s.max(axis=-1, keepdims=True))
    l, acc = None, None
    for s, pv in tiles:
        p = jnp.exp(s - m)
        ps, pa = jnp.sum(p, axis=-1, keepdims=True), pv(p.astype(BF16))
        l, acc = (ps, pa) if l is None else (l + ps, acc + pa)
    return acc / l


def _page_specs(layer, n_pages, rows, cols):
    return [pl.BlockSpec((None, None, rows, cols),
                         functools.partial(lambda b, pt, p: (layer, pt[b, p], 0, 0), p=p))
            for p in range(n_pages)]


def _fox_dec_body(pt_ref, qb_ref, *refs, n_pages, ts):
    kt = refs[:n_pages]
    vt = refs[n_pages:2 * n_pages]
    lf = refs[2 * n_pages:3 * n_pages]
    knt_ref, vnt_ref, cnb_ref, cnt_ref, o_ref = refs[3 * n_pages:]
    rows = H_FOX * ts
    page = knt_ref.shape[-1]
    qb = qb_ref[...]
    qk = lambda k_ref: jnp.dot(qb, k_ref[...].astype(BF16), preferred_element_type=F32)
    pv = lambda v_ref: (lambda p: _dot_nt(p, v_ref[...].astype(BF16)))

    lane = lax.broadcasted_iota(jnp.int32, (ts, H_FOX, page), 2).reshape(rows, page)
    tok = lax.broadcasted_iota(jnp.int32, (ts, H_FOX, page), 0).reshape(rows, page)
    cn_keys = jnp.concatenate([cnt_ref[...]] * ts, axis=0)
    bias_new = jnp.where((lane <= tok) & (lane < ts), cnb_ref[...] - cn_keys, NEG_INF)
    tiles = [(qk(knt_ref) + bias_new, pv(vnt_ref))]

    j_idx = lax.broadcasted_iota(jnp.int32, (page, page), 0)
    k_idx = lax.broadcasted_iota(jnp.int32, (page, page), 1)
    later = jnp.where(j_idx > k_idx, 1.0, 0.0).astype(BF16)
    in_page = [_dot_sel_rhs(lf[p][...], later) for p in range(n_pages)]
    carry = jnp.zeros((H_FOX, 1), F32)
    for p in reversed(range(n_pages)):
        suffix = in_page[p] + carry
        carry = suffix[:, 0:1] + lf[p][:, 0:1]
        bias = jnp.concatenate([suffix] * ts, axis=0) + cnb_ref[...]
        tiles.append((qk(kt[p]) + bias, pv(vt[p])))

    accn = _softmax_tiles(tiles).reshape(ts, H_FOX, W_BRANCH)
    col_head = lax.shift_right_logical(lax.broadcasted_iota(jnp.int32, (H_FOX, W_BRANCH), 1),
                                       HEAD_DIM.bit_length() - 1)
    own = col_head == lax.broadcasted_iota(jnp.int32, (H_FOX, W_BRANCH), 0)
    o_ref[...] = jnp.sum(jnp.where(own[None], accn, 0.0), axis=1)


def _fox_decode(page_table, qb, kt, vt, lft, knt, vnt, cnb, cnt, layer, ts):
    db, rows, w = qb.shape
    page = kt.shape[-1]
    n_pages = page_table.shape[1]
    per_seq = lambda r, c: pl.BlockSpec((None, r, c), lambda b, pt: (b, 0, 0))
    grid_spec = pltpu.PrefetchScalarGridSpec(
        num_scalar_prefetch=1,
        grid=(db,),
        in_specs=([per_seq(rows, w)] + _page_specs(layer, n_pages, w, page) * 2
                  + _page_specs(layer, n_pages, H_FOX, page)
                  + [per_seq(w, page), per_seq(w, page), per_seq(rows, page), per_seq(H_FOX, page)]),
        out_specs=per_seq(ts, w))
    return pl.pallas_call(
        functools.partial(_fox_dec_body, n_pages=n_pages, ts=ts),
        grid_spec=grid_spec,
        out_shape=jax.ShapeDtypeStruct((db, ts, w), F32),
        compiler_params=_cparams(1, 40),
        name="fox_decode",
    )(page_table, qb, *([kt] * n_pages), *([vt] * n_pages), *([lft] * n_pages), knt, vnt, cnb, cnt)


def _diff_dec_body(pt_ref, tab_ref, q_ref, *refs, n_pages, ts, page, lam_init):
    kp = refs[:n_pages]
    vp = refs[n_pages:2 * n_pages]
    (kn_ref, vn_ref, lq1_ref, lk1_ref, lq2_ref, lk2_ref, g_ref, o_ref, bias_ref) = refs[2 * n_pages:]
    rows = 2 * H_DIFF * ts
    cols = page * H_DIFF
    past = n_pages * page

    @pl.when(pl.program_id(0) == 0)
    def _():
        lane = lax.broadcasted_iota(jnp.int32, (ts, cols), 1)
        pos = lax.shift_right_logical(lane, H_DIFF.bit_length() - 1)
        head = lane & (H_DIFF - 1)
        qpos = past + lax.broadcasted_iota(jnp.int32, (ts, cols), 0)

        def fill(tile, kpos, valid):
            bucket = _t5_bucket(jnp.maximum(qpos - kpos, 0))
            for rb in range(2 * H_DIFF):
                h, mp = rb // 2, rb % 2
                val = _t5_lookup(bucket, tab_ref, mp * H_DIFF + h)
                bias_ref[tile, rb * ts:(rb + 1) * ts, :] = jnp.where(valid & (head == h), val, NEG_INF)

        def past_tile(tile, carry):
            kpos = tile * page + pos
            fill(tile, kpos, kpos <= qpos)
            return carry

        lax.fori_loop(0, n_pages, past_tile, 0)
        kpos = past + pos
        fill(n_pages, kpos, (kpos <= qpos) & (pos < ts))

    q = q_ref[...]
    tiles = []
    for p in range(n_pages + 1):
        k_ref, v_ref = (kp[p], vp[p]) if p < n_pages else (kn_ref, vn_ref)
        s = _dot_nt(q, k_ref[...].astype(BF16)) + bias_ref[p]
        tiles.append((s, lambda pr, v_ref=v_ref: jnp.dot(pr, v_ref[...].astype(BF16),
                                                         preferred_element_type=F32)))

    lam = _lambda(lq1_ref, lk1_ref, lq2_ref, lk2_ref, lam_init)
    on = _softmax_tiles(tiles)
    for h in range(H_DIFF):
        o1 = on[2 * h * ts:(2 * h + 1) * ts, :]
        o2 = on[(2 * h + 1) * ts:(2 * h + 2) * ts, :]
        o_ref[:, h * 2 * HEAD_DIM:(h + 1) * 2 * HEAD_DIM] = _subln(o1 - lam * o2, g_ref[...], lam_init)


def _diff_decode(page_table, table, q, kp, vp, kn, vn, lq1, lk1, lq2, lk2, g, layer, ts, lam_init):
    db, rows, d = q.shape
    cols = kp.shape[2]
    page = cols // H_DIFF
    n_pages = page_table.shape[1]
    per_seq = lambda r, c: pl.BlockSpec((None, r, c), lambda b, pt: (b, 0, 0))
    vec = pl.BlockSpec((1, HEAD_DIM), lambda b, pt: (0, 0))
    grid_spec = pltpu.PrefetchScalarGridSpec(
        num_scalar_prefetch=1,
        grid=(db,),
        in_specs=([pl.BlockSpec(memory_space=pltpu.SMEM), per_seq(rows, d)]
                  + _page_specs(layer, n_pages, cols, d) * 2
                  + [per_seq(cols, d), per_seq(cols, d), vec, vec, vec, vec,
                     pl.BlockSpec((1, d), lambda b, pt: (0, 0))]),
        out_specs=per_seq(ts, H_DIFF * d),
        scratch_shapes=[pltpu.VMEM((n_pages + 1, rows, cols), F32)])
    return pl.pallas_call(
        functools.partial(_diff_dec_body, n_pages=n_pages, ts=ts, page=page, lam_init=lam_init),
        grid_spec=grid_spec,
        out_shape=jax.ShapeDtypeStruct((db, ts, H_DIFF * d), F32),
        compiler_params=_cparams(1, 40),
        name="diff_decode",
    )(page_table, table, q, *([kp] * n_pages), *([vp] * n_pages), kn, vn, lq1, lk1, lq2, lk2, g)


def _lru_body(lx_ref, lg_ref, c0_ref, h0_ref, cw_ref, cb_ref, wa_ref, ba_ref, wx_ref, bx_ref,
              lam_ref, o_ref, hn_ref, cin_ref, a_ref, u_ref, hs_ref, hc_ref, *, tc):
    @pl.when(pl.program_id(1) == 0)
    def _():
        cin_ref[0:SUBLANES, :] = jnp.zeros((SUBLANES, W_BRANCH), F32)
        cin_ref[SUBLANES - (CONV_W - 1):SUBLANES, :] = c0_ref[...]
        hc_ref[...] = h0_ref[...]

    lx = lx_ref[...]
    cin_ref[SUBLANES:SUBLANES + tc, :] = lx
    base = SUBLANES - (CONV_W - 1)
    acc = cw_ref[0:1, :] * cin_ref[base:base + tc, :]
    for j in range(1, CONV_W):
        acc = acc + cw_ref[j:j + 1, :] * cin_ref[base + j:base + j + tc, :]
    xc = cb_ref[...] + acc
    xb = xc.astype(BF16)
    r_gate = jax.nn.sigmoid(jnp.dot(xb, wa_ref[...], preferred_element_type=F32) + ba_ref[...])
    i_gate = jax.nn.sigmoid(jnp.dot(xb, wx_ref[...], preferred_element_type=F32) + bx_ref[...])
    log_a = -LRU_C * r_gate * _softplus(-lam_ref[...])
    a_ref[...] = jnp.exp(log_a)
    u_ref[...] = jnp.sqrt(1.0 - jnp.exp(2.0 * log_a)) * (i_gate * xc)

    def step(t, h):
        h = a_ref[pl.ds(t, 1), :] * h + u_ref[pl.ds(t, 1), :]
        hs_ref[pl.ds(t, 1), :] = h
        return h

    h = lax.fori_loop(0, tc, step, hc_ref[...], unroll=8)
    hc_ref[...] = h
    hn_ref[...] = h
    o_ref[...] = hs_ref[...] * jax.nn.gelu(lg_ref[...])
    cin_ref[0:SUBLANES, :] = lx[tc - SUBLANES:tc, :]


def _lru(proj, row0, nseq, t, conv0, h0, layer, cw, cb, wa, ba, wx, bx, lam):
    m_total = proj.shape[0]
    tc = _tile(t, 512, SUBLANES)
    nc = t // tc
    blk0 = row0 // tc
    col = lambda off: (lambda s, c: (blk0 + s * nc + c, off // W_BRANCH))
    vec = pl.BlockSpec((1, W_BRANCH), lambda s, c: (0, 0))
    mat = pl.BlockSpec((W_BRANCH, W_BRANCH), lambda s, c: (0, 0))
    return pl.pallas_call(
        functools.partial(_lru_body, tc=tc),
        grid=(nseq, nc),
        in_specs=[pl.BlockSpec((tc, W_BRANCH), col(OFF_LX)),
                  pl.BlockSpec((tc, W_BRANCH), col(OFF_LG)),
                  pl.BlockSpec((None, None, CONV_W - 1, W_BRANCH), lambda s, c: (layer, s, 0, 0)),
                  pl.BlockSpec((None, None, 1, W_BRANCH), lambda s, c: (layer, s, 0, 0)),
                  pl.BlockSpec((CONV_W, W_BRANCH), lambda s, c: (0, 0)),
                  vec, mat, vec, mat, vec, vec],
        out_specs=[pl.BlockSpec((tc, W_BRANCH), lambda s, c: (s * nc + c, 0)),
                   pl.BlockSpec((None, 1, W_BRANCH), lambda s, c: (s, 0, 0))],
        out_shape=[jax.ShapeDtypeStruct((nseq * t, W_BRANCH), F32),
                   jax.ShapeDtypeStruct((nseq, 1, W_BRANCH), F32)],
        scratch_shapes=[pltpu.VMEM((tc + SUBLANES, W_BRANCH), F32),
                        pltpu.VMEM((tc, W_BRANCH), F32), pltpu.VMEM((tc, W_BRANCH), F32),
                        pltpu.VMEM((tc, W_BRANCH), F32), pltpu.VMEM((1, W_BRANCH), F32)],
        compiler_params=_cparams(2, 32),
        name="lru",
    )(proj, proj, conv0, h0, cw, cb, wa, ba, wx, bx, lam)


def _rwkv_prep_body(rr_ref, rk_ref, rv_ref, lo_ref, sh_ref, mu_ref, w0_ref, w2_ref, a0_ref, a2_ref,
                    g2_ref, kk_ref, ka_ref, rkk_ref, seg_ref,
                    r_o, w_o, k_o, v_o, a_o, b_o, g_o, bon_o, xs_ref, *, tc):
    wb = W_BRANCH

    @pl.when(pl.program_id(1) == 0)
    def _():
        xs_ref[0:SUBLANES, :] = jnp.zeros((SUBLANES, W_RWKV_PROJ), F32)
        xs_ref[SUBLANES - 1:SUBLANES, :] = sh_ref[...]

    xs_ref[SUBLANES:SUBLANES + tc, 0:wb] = rr_ref[...]
    xs_ref[SUBLANES:SUBLANES + tc, wb:2 * wb] = rk_ref[...]
    xs_ref[SUBLANES:SUBLANES + tc, 2 * wb:3 * wb] = rv_ref[...]
    xs_ref[SUBLANES:SUBLANES + tc, 3 * wb:W_RWKV_PROJ] = lo_ref[...]
    cur = xs_ref[SUBLANES:SUBLANES + tc, :]
    prev = xs_ref[SUBLANES - 1:SUBLANES - 1 + tc, :]
    rx = cur + (prev - cur) * mu_ref[...]
    xs_ref[0:SUBLANES, :] = cur[tc - SUBLANES:tc, :]

    rr = rx[:, 0:wb]
    rk = rx[:, wb:2 * wb]
    rv = rx[:, 2 * wb:3 * wb]
    wl_al = rx[:, 3 * wb:3 * wb + LANES]
    gl = rx[:, 3 * wb + LANES:W_RWKV_PROJ]
    dot = functools.partial(jnp.dot, preferred_element_type=F32)
    wd = -_softplus(-(w0_ref[...] + dot(jnp.tanh(wl_al).astype(BF16), w2_ref[...]))) - 0.5
    decay = jnp.exp(-jnp.exp(wd))
    aa = jax.nn.sigmoid(a0_ref[...] + dot(wl_al.astype(BF16), a2_ref[...]))
    gg = dot(jax.nn.sigmoid(gl).astype(BF16), g2_ref[...])
    kk = rk * kk_ref[...]
    norm = jnp.sqrt(_dot_sel_rhs(kk * kk, seg_ref[...]))
    kk = kk / jnp.maximum(norm, 1e-12)
    kh = rk * (1.0 + (aa - 1.0) * ka_ref[...])
    r_o[...] = rr
    w_o[...] = decay
    k_o[...] = kh
    v_o[...] = rv
    a_o[...] = -kk
    b_o[...] = kk * aa
    g_o[...] = gg
    bon_o[...] = _dot_sel_rhs(rr * kh * rkk_ref[...], seg_ref[...]) * rv


def _rwkv_prep(proj, row0, nseq, t, shift0, layer, mu, w0, w2p, a0, a2p, g2, k_k, k_a, r_k, seg):
    tc = _tile(t, 256, SUBLANES)
    nc = t // tc
    blk0 = row0 // tc
    wb = W_BRANCH
    col = lambda off, w: (lambda s, c: (blk0 + s * nc + c, off // w))
    vec = pl.BlockSpec((1, wb), lambda s, c: (0, 0))
    out_blk = pl.BlockSpec((tc, wb), lambda s, c: (s * nc + c, 0))
    lo_w = W_RWKV_PROJ - 3 * wb
    return pl.pallas_call(
        functools.partial(_rwkv_prep_body, tc=tc),
        grid=(nseq, nc),
        in_specs=[pl.BlockSpec((tc, wb), col(OFF_RW, wb)),
                  pl.BlockSpec((tc, wb), col(OFF_RW + wb, wb)),
                  pl.BlockSpec((tc, wb), col(OFF_RW + 2 * wb, wb)),
                  pl.BlockSpec((tc, lo_w), col(OFF_RW + 3 * wb, lo_w)),
                  pl.BlockSpec((None, None, 1, W_RWKV_PROJ), lambda s, c: (layer, s, 0, 0)),
                  pl.BlockSpec((1, W_RWKV_PROJ), lambda s, c: (0, 0)),
                  vec, pl.BlockSpec((LANES, wb), lambda s, c: (0, 0)),
                  vec, pl.BlockSpec((LANES, wb), lambda s, c: (0, 0)),
                  pl.BlockSpec((LANES, wb), lambda s, c: (0, 0)),
                  vec, vec, vec, pl.BlockSpec((wb, wb), lambda s, c: (0, 0))],
        out_specs=[out_blk] * 8,
        out_shape=[jax.ShapeDtypeStruct((nseq * t, wb), F32)] * 8,
        scratch_shapes=[pltpu.VMEM((tc + SUBLANES, W_RWKV_PROJ), F32)],
        compiler_params=_cparams(2, 40),
        name="rwkv_prep",
    )(proj, proj, proj, proj, shift0, mu, w0, w2p, a0, a2p, g2, k_k, k_a, r_k, seg)


def _rwkv_scan_body(r_ref, w_ref, k_ref, v_ref, a_ref, b_ref, s0_ref, y_ref, sn_ref, st_ref,
                    *, nb, tc, n_chunks):
    c = pl.program_id(1)

    @pl.when(c == 0)
    def _():
        st_ref[...] = s0_ref[...]

    n_pairs = H_RWKV // 2
    shape = (HEAD_DIM, LANES)
    lane = lax.broadcasted_iota(jnp.int32, shape, 1)
    first = lane < HEAD_DIM
    eye2 = jnp.where((lane & (HEAD_DIM - 1)) == lax.broadcasted_iota(jnp.int32, shape, 0), 1.0, 0.0)
    half = lambda d: lax.shift_right_logical(lax.broadcasted_iota(jnp.int32, (LANES, LANES), d),
                                             HEAD_DIM.bit_length() - 1)
    ones_bd = jnp.where(half(0) == half(1), 1.0, 0.0).astype(BF16)

    def step(t, states):
        new_states = []
        for s in range(nb):
            rows = [ref[s, pl.ds(t, 1), :] for ref in (r_ref, w_ref, k_ref, v_ref, a_ref, b_ref)]
            y_parts = []
            for p in range(n_pairs):
                r_t, w_t, k_t, v_t, a_t, b_t = (z[:, p * LANES:(p + 1) * LANES] for z in rows)
                st = states[s * n_pairs + p]
                sa = jnp.dot((st * a_t).astype(BF16), ones_bd, preferred_element_type=F32)
                d = eye2 * v_t
                v1 = jnp.sum(jnp.where(first, d, 0.0), axis=1, keepdims=True)
                v2 = jnp.sum(jnp.where(first, 0.0, d), axis=1, keepdims=True)
                st = st * w_t + sa * b_t + jnp.where(first, v1, v2) * k_t
                y = jnp.dot((st * r_t).astype(BF16), ones_bd, preferred_element_type=F32)
                y_parts.append(jnp.sum(eye2 * y, axis=0, keepdims=True))
                new_states.append(st)
            y_ref[s, pl.ds(t, 1), :] = jnp.concatenate(y_parts, axis=1)
        return tuple(new_states)

    init = tuple(st_ref[s, p] for s in range(nb) for p in range(n_pairs))
    final = lax.fori_loop(0, tc, step, init, unroll=2)
    for i, st in enumerate(final):
        st_ref[i // n_pairs, i % n_pairs] = st

    @pl.when(c == n_chunks - 1)
    def _():
        sn_ref[...] = st_ref[...]


def _pack_pairs(s):
    n = s.shape[0]
    s = s.reshape(n, H_RWKV // 2, 2, HEAD_DIM, HEAD_DIM).transpose(0, 1, 3, 2, 4)
    return s.reshape(n, H_RWKV // 2, HEAD_DIM, 2 * HEAD_DIM)


def _unpack_pairs(s):
    n = s.shape[0]
    s = s.reshape(n, H_RWKV // 2, HEAD_DIM, 2, HEAD_DIM).transpose(0, 1, 3, 2, 4)
    return s.reshape(n, H_RWKV, HEAD_DIM, HEAD_DIM)


def _rwkv_scan(r, w, k, v, a, b, s0):
    nseq, t, wd = r.shape
    nb = 2 if nseq % 2 == 0 else 1
    tc = _tile(t, LANES, SUBLANES)
    n_chunks = t // tc
    n_pairs = H_RWKV // 2
    blk = pl.BlockSpec((nb, tc, wd), lambda s, c: (s, c, 0))
    st_blk = pl.BlockSpec((nb, n_pairs, HEAD_DIM, LANES), lambda s, c: (s, 0, 0, 0))
    y, sn = pl.pallas_call(
        functools.partial(_rwkv_scan_body, nb=nb, tc=tc, n_chunks=n_chunks),
        grid=(nseq // nb, n_chunks),
        in_specs=[blk] * 6 + [st_blk],
        out_specs=[blk, st_blk],
        out_shape=[jax.ShapeDtypeStruct((nseq, t, wd), F32),
                   jax.ShapeDtypeStruct((nseq, n_pairs, HEAD_DIM, LANES), F32)],
        scratch_shapes=[pltpu.VMEM((nb, n_pairs, HEAD_DIM, LANES), F32)],
        compiler_params=_cparams(2, 40),
        name="rwkv_scan",
    )(r, w, k, v, a, b, _pack_pairs(s0))
    return y, _unpack_pairs(sn)


def _rwkv_post_body(y_ref, g_ref, bon_ref, lg_ref, lb_ref, seg_ref, o_ref):
    y = y_ref[...]
    inv_n = 1.0 / HEAD_DIM
    mu = _dot_sel_rhs(y, seg_ref[...]) * inv_n
    d = y - mu
    var = _dot_sel_rhs(d * d, seg_ref[...]) * inv_n
    yn = d * lax.rsqrt(var + GN_EPS) * lg_ref[...] + lb_ref[...]
    o_ref[...] = (yn + bon_ref[...]) * g_ref[...]


def _rwkv_post(y, gg, bonus, lnx_g, lnx_b, seg):
    m, wb = y.shape
    tm = _tile(m, 512, SUBLANES)
    blk = pl.BlockSpec((tm, wb), lambda i: (i, 0))
    vec = pl.BlockSpec((1, wb), lambda i: (0, 0))
    return pl.pallas_call(
        _rwkv_post_body,
        grid=(m // tm,),
        in_specs=[blk, blk, blk, vec, vec, pl.BlockSpec((wb, wb), lambda i: (0, 0))],
        out_specs=blk,
        out_shape=jax.ShapeDtypeStruct((m, wb), F32),
        compiler_params=_cparams(1, 24),
        name="rwkv_post",
    )(y, gg, bonus, lnx_g, lnx_b, seg)


def _block_diag(blocks):
    n, bi, bj = blocks.shape
    eye = jnp.eye(n, dtype=blocks.dtype)
    return jnp.einsum('nij,nm->nimj', blocks, eye).reshape(n * bi, n * bj)


def kernel(x_prompt, x_sample, cache_fox_k, cache_fox_v, cache_fox_logf, cache_diff_k, cache_diff_v, state_lru_conv, state_lru_h, state_rwkv_shift, state_rwkv_wkv, page_table, norm1_g, w_in, fox_fb, lam_q1, lam_k1, lam_q2, lam_k2, diff_subln_g, t5_table, lru_conv_w, lru_conv_b, lru_wa, lru_ba, lru_wx, lru_bx, lru_lambda, rwkv_mu, rwkv_w0, rwkv_w2, rwkv_a0, rwkv_a2, rwkv_g2, rwkv_k_k, rwkv_k_a, rwkv_r_k, rwkv_lnx_g, rwkv_lnx_b, wb_fox, wb_diff, wb_lru, wb_rwkv, w_out, norm2_g, w_up, w_down, final_g):
    B, T, D = x_prompt.shape
    DB, TS, _ = x_sample.shape
    L = w_in.shape[0]
    n_pool, page = cache_fox_k.shape[1], cache_fox_k.shape[2]
    n_pages = page_table.shape[1]
    mp, ms = B * T, DB * TS
    wb = W_BRANCH
    assert TS % SUBLANES == 0 and TS <= page and T % SUBLANES == 0

    x = jnp.concatenate([x_prompt.reshape(mp, D), x_sample.reshape(ms, D)], axis=0)
    kt_fox = cache_fox_k.transpose(0, 1, 3, 4, 2).reshape(L, n_pool, wb, page)
    vt_fox = cache_fox_v.transpose(0, 1, 3, 4, 2).reshape(L, n_pool, wb, page)
    lft_fox = cache_fox_logf.transpose(0, 1, 3, 2)
    kp_diff = cache_diff_k.reshape(L, n_pool, page * H_DIFF, 2 * HEAD_DIM)
    vp_diff = cache_diff_v.reshape(L, n_pool, page * H_DIFF, 2 * HEAD_DIM)
    seg = _block_diag(jnp.ones((H_RWKV, HEAD_DIM, HEAD_DIM), BF16))
    zeros_conv = jnp.zeros((L, B, CONV_W - 1, wb), F32)
    zeros_h = jnp.zeros((L, B, 1, wb), F32)
    zeros_shift = jnp.zeros((L, B, 1, W_RWKV_PROJ), F32)
    zeros_wkv = jnp.zeros((B, H_RWKV, HEAD_DIM, HEAD_DIM), F32)
    eye_f = jnp.eye(H_FOX, dtype=F32)
    eye_2 = jnp.eye(2, dtype=F32)
    row = lambda v: v.reshape(1, -1)

    def pad_rows(z, n):
        return jnp.pad(z, ((0, 0), (0, n - z.shape[1]), (0, 0)))

    def pad_lanes(z):
        return jnp.pad(z, ((0, 0), (0, 0), (0, page - z.shape[2])))

    p_states, s_states = [], []
    y_norm = None
    for l in range(L):
        lam_init = 0.8 - 0.6 * math.exp(-0.3 * l)
        w = w_in[l]
        n_a = OFF_FF
        split_ff = 3 * wb
        w_r = jnp.concatenate(
            [w[:, :split_ff], w[:, split_ff + H_FOX:split_ff + H_FOX + (n_a - split_ff)],
             w[:, split_ff:split_ff + H_FOX], jnp.zeros((D, FF_PAD), F32),
             w[:, split_ff + H_FOX + (n_a - split_ff):]], axis=1).astype(BF16)
        proj = _rms_matmul(x, row(norm1_g[l]), w_r)
        pp, ps = proj[:mp], proj[mp:]

        lf_p, c_p = _logf_cumsum(pp[:, OFF_FF:OFF_FF + H_FOX].reshape(B, T, H_FOX), row(fox_fb[l]),
                                 jnp.zeros((B, 1, H_FOX), F32))
        lf_s, cn = _logf_cumsum(ps[:, OFF_FF:OFF_FF + H_FOX].reshape(DB, TS, H_FOX), row(fox_fb[l]),
                                jnp.zeros((DB, 1, H_FOX), F32))

        o_fox_p = _fox_flash(proj, c_p, c_p.transpose(0, 2, 1), B, T)

        fq_s = ps[:, OFF_FQ:OFF_FQ + wb].reshape(DB, TS, H_FOX, HEAD_DIM) * QK_SCALE
        qb = jnp.einsum('bthd,hg->bthgd', fq_s, eye_f).reshape(DB, TS * H_FOX, wb).astype(BF16)
        cnb = jnp.broadcast_to(cn.reshape(DB, TS * H_FOX, 1), (DB, TS * H_FOX, page))
        cnt = pad_lanes(cn.transpose(0, 2, 1))
        knt = pad_lanes(ps[:, OFF_FK:OFF_FK + wb].reshape(DB, TS, wb).transpose(0, 2, 1))
        vnt = pad_lanes(ps[:, OFF_FV:OFF_FV + wb].reshape(DB, TS, wb).transpose(0, 2, 1))
        o_fox_s = _fox_decode(page_table, qb, kt_fox, vt_fox, lft_fox, knt, vnt, cnb, cnt, l, TS).reshape(ms, wb)

        lam_vecs = (row(lam_q1[l]), row(lam_k1[l]), row(lam_q2[l]), row(lam_k2[l]))
        g_sub = row(diff_subln_g[l])
        o_diff_p = _diff_flash(t5_table, proj, *lam_vecs, g_sub, lam_init, B, T)

        dq_s = ps[:, OFF_DQ:OFF_DQ + wb].reshape(DB, TS, H_DIFF, 2, HEAD_DIM) * QK_SCALE
        qb_d = jnp.einsum('bthmd,mn->bhmtnd', dq_s, eye_2).reshape(
            DB, 2 * H_DIFF * TS, 2 * HEAD_DIM).astype(BF16)
        kn_d = pad_rows(ps[:, OFF_DK:OFF_DK + wb].reshape(DB, TS * H_DIFF, 2 * HEAD_DIM), page * H_DIFF)
        vn_d = pad_rows(ps[:, OFF_DV:OFF_DV + wb].reshape(DB, TS * H_DIFF, 2 * HEAD_DIM), page * H_DIFF)
        o_diff_s = _diff_decode(page_table, t5_table, qb_d, kp_diff, vp_diff, kn_d, vn_d,
                                *lam_vecs, g_sub, l, TS, lam_init).reshape(ms, wb)

        lru_w = (lru_conv_w[l], row(lru_conv_b[l]), _block_diag(lru_wa[l]).astype(BF16), row(lru_ba[l]),
                 _block_diag(lru_wx[l]).astype(BF16), row(lru_bx[l]), row(lru_lambda[l]))
        o_lru_p, h_p = _lru(proj, 0, B, T, zeros_conv, zeros_h, l, *lru_w)
        o_lru_s, h_s = _lru(proj, mp, DB, TS, state_lru_conv, state_lru_h.reshape(L, DB, 1, wb), l, *lru_w)

        rank = rwkv_w2.shape[1]
        w2p = jnp.concatenate([rwkv_w2[l], jnp.zeros((LANES - rank, wb), F32)], axis=0).astype(BF16)
        a2p = jnp.concatenate([jnp.zeros((LANES - rwkv_a2.shape[1], wb), F32), rwkv_a2[l]], axis=0).astype(BF16)
        rw_w = (row(rwkv_mu[l]), row(rwkv_w0[l]), w2p, row(rwkv_a0[l]), a2p, rwkv_g2[l].astype(BF16),
                row(rwkv_k_k[l]), row(rwkv_k_a[l]), row(rwkv_r_k[l]), seg)
        prep_p = _rwkv_prep(proj, 0, B, T, zeros_shift, l, *rw_w)
        prep_s = _rwkv_prep(proj, mp, DB, TS, state_rwkv_shift.reshape(L, DB, 1, W_RWKV_PROJ), l, *rw_w)
        y_p, wkv_p = _rwkv_scan(*(z.reshape(B, T, wb) for z in prep_p[:6]), zeros_wkv)
        y_s, wkv_s = _rwkv_scan(*(z.reshape(DB, TS, wb) for z in prep_s[:6]), state_rwkv_wkv[l])
        y_rw = jnp.concatenate([y_p.reshape(mp, wb), y_s.reshape(ms, wb)], axis=0)
        gg = jnp.concatenate([prep_p[6], prep_s[6]], axis=0)
        bonus = jnp.concatenate([prep_p[7], prep_s[7]], axis=0)
        o_rwkv = _rwkv_post(y_rw, gg, bonus, row(rwkv_lnx_g[l]), row(rwkv_lnx_b[l]), seg)

        outs = (jnp.concatenate([o_fox_p, o_fox_s], axis=0), jnp.concatenate([o_diff_p, o_diff_s], axis=0),
                jnp.concatenate([o_lru_p, o_lru_s], axis=0), o_rwkv)
        wbs = (wb_fox[l].astype(BF16), wb_diff[l].astype(BF16), wb_lru[l].astype(BF16), wb_rwkv[l].astype(BF16))
        mixed = _merge(outs, wbs, proj, D)
        x = _matmul_res(mixed, w_out[l].astype(BF16), x)
        x, y_norm = _mlp(x, row(norm2_g[l]), w_up[l].astype(BF16), w_down[l].astype(BF16), row(final_g))

        def states(pz, n, t, lf, conv0, h_last, wkv):
            lx = pz[:, OFF_LX:OFF_LX + wb].reshape(n, t, wb)
            conv_in = jnp.concatenate([conv0, lx], axis=1)
            rw = pz[:, OFF_RW:OFF_RW + W_RWKV_PROJ].reshape(n, t, W_RWKV_PROJ)
            return (pz[:, OFF_FK:OFF_FK + wb].reshape(n, t, H_FOX, HEAD_DIM),
                    pz[:, OFF_FV:OFF_FV + wb].reshape(n, t, H_FOX, HEAD_DIM),
                    lf,
                    pz[:, OFF_DK:OFF_DK + wb].reshape(n, t, H_DIFF, 2 * HEAD_DIM),
                    pz[:, OFF_DV:OFF_DV + wb].reshape(n, t, H_DIFF, 2 * HEAD_DIM),
                    conv_in[:, -(CONV_W - 1):], h_last.reshape(n, wb), rw[:, -1], wkv)

        p_states.append(states(pp, B, T, lf_p, zeros_conv[0], h_p, wkv_p))
        s_states.append(states(ps, DB, TS, lf_s, state_lru_conv[l], h_s, wkv_s))

    y_prompt = y_norm[:mp].reshape(B, T, D)
    y_sample = y_norm[mp:].reshape(DB, TS, D)
    p_out = [jnp.stack(z) for z in zip(*p_states)]
    s_out = [jnp.stack(z) for z in zip(*s_states)]
    return (y_prompt, y_sample, *p_out, *s_out)
```

```python
import functools
import math

import jax
import jax.numpy as jnp
from jax import lax
from jax.experimental import pallas as pl
from jax.experimental.pallas import tpu as pltpu

F32 = jnp.float32
BF16 = jnp.bfloat16

HEAD_DIM = 64
W_BRANCH = 512
H_FOX = 8
H_DIFF = 4
H_RWKV = 8
CONV_W = 4
LRU_C = 8.0
NUM_BUCKETS = 32
MAX_DISTANCE = 128
RMS_EPS = 1e-6
SUBLN_EPS = 1e-5
GN_EPS = 64e-5
NEG_INF = -1e30
QK_SCALE = HEAD_DIM ** -0.5

LANES = 128
SUBLANES = 8
VMEM_LIMIT_CAP = 56 * 1024 * 1024

OFF_FQ, OFF_FK, OFF_FV = 0, 512, 1024
OFF_DQ, OFF_DK, OFF_DV = 1536, 2048, 2560
OFF_LX, OFF_LG = 3072, 3584
OFF_RW = 4096
W_RWKV_PROJ = 1792
OFF_FF = 5888
OFF_GATE = 6144
FF_PAD = OFF_GATE - OFF_FF - H_FOX


def _tile(n, pref, mult):
    best = None
    t = mult
    while t <= min(n, pref):
        if n % t == 0:
            best = t
        t += mult
    return best if best is not None else n


def _cparams(n_axes, vmem_mb):
    return pltpu.CompilerParams(
        dimension_semantics=("arbitrary",) * n_axes,
        vmem_limit_bytes=min(int(vmem_mb * 1024 * 1024), VMEM_LIMIT_CAP))


def _softplus(x):
    return jnp.maximum(x, 0.0) + jnp.log1p(jnp.exp(-jnp.abs(x)))


def _split3(x):
    def top(v):
        bits = lax.bitcast_convert_type(v, jnp.uint32) & jnp.uint32(0xFFFF0000)
        return lax.bitcast_convert_type(bits, F32)

    x1 = top(x)
    r1 = x - x1
    x2 = top(r1)
    x3 = r1 - x2
    return x1.astype(BF16), x2.astype(BF16), x3.astype(BF16)


def _dot_sel_rhs(x, sel):
    x1, x2, x3 = _split3(x)
    d = functools.partial(jnp.dot, preferred_element_type=F32)
    return d(x1, sel) + d(x2, sel) + d(x3, sel)


def _dot_sel_lhs(sel, x):
    x1, x2, x3 = _split3(x)
    d = functools.partial(jnp.dot, preferred_element_type=F32)
    return d(sel, x1) + d(sel, x2) + d(sel, x3)


def _dot_nt(a, b):
    return lax.dot_general(a, b, (((1,), (1,)), ((), ())), preferred_element_type=F32)


def _rms_matmul_body(x_ref, g_ref, w_ref, o_ref, xn_ref):
    @pl.when(pl.program_id(1) == 0)
    def _():
        x = x_ref[...]
        ms = jnp.mean(x * x, axis=-1, keepdims=True)
        xn_ref[...] = (x * lax.rsqrt(ms + RMS_EPS) * g_ref[...]).astype(BF16)

    o_ref[...] = jnp.dot(xn_ref[...], w_ref[...], preferred_element_type=F32)


def _rms_matmul(x, g, w):
    m, k = x.shape
    n = w.shape[1]
    tm = _tile(m, 1024, SUBLANES)
    tn = _tile(n, 1024, LANES)
    vmem = (2 * tm * k * 4 + 2 * k * tn * 2 + 2 * tm * tn * 4 + tm * k * 2) / 2**20 + 8
    return pl.pallas_call(
        _rms_matmul_body,
        grid=(m // tm, n // tn),
        in_specs=[pl.BlockSpec((tm, k), lambda i, j: (i, 0)),
                  pl.BlockSpec((1, k), lambda i, j: (0, 0)),
                  pl.BlockSpec((k, tn), lambda i, j: (0, j))],
        out_specs=pl.BlockSpec((tm, tn), lambda i, j: (i, j)),
        out_shape=jax.ShapeDtypeStruct((m, n), F32),
        scratch_shapes=[pltpu.VMEM((tm, k), BF16)],
        compiler_params=_cparams(2, vmem),
        name="in_proj",
    )(x, g, w)


def _merge_body(*refs, n_first):
    first, second, ws, gs, o_ref = refs[0:4], refs[4:8], refs[8:12], refs[12:16], refs[16]

    def run(branches):
        acc = None
        for o_r, w_r, g_r in zip(branches, ws, gs):
            y = jax.nn.sigmoid(g_r[...]) * jnp.dot(o_r[...].astype(BF16), w_r[...],
                                                   preferred_element_type=F32)
            acc = y if acc is None else acc + y
        o_ref[...] = acc.astype(BF16)

    @pl.when(pl.program_id(0) < n_first)
    def _():
        run(first)

    @pl.when(pl.program_id(0) >= n_first)
    def _():
        run(second)


def _merge(outs_p, outs_s, wbs, proj, d_model):
    mp, w = outs_p[0].shape
    ms = outs_s[0].shape[0]
    tm = _tile(math.gcd(mp, ms), 512, 16)
    tn = _tile(d_model, 512, LANES)
    n_first = mp // tm
    p_specs = [pl.BlockSpec((tm, w), lambda i, j: (jnp.minimum(i, n_first - 1), 0)) for _ in range(4)]
    s_specs = [pl.BlockSpec((tm, w), lambda i, j: (jnp.maximum(i - n_first, 0), 0)) for _ in range(4)]
    w_specs = [pl.BlockSpec((w, tn), lambda i, j: (0, j)) for _ in range(4)]
    g_specs = [pl.BlockSpec((tm, tn), functools.partial(
        lambda i, j, base: (i, base + j), base=(OFF_GATE + b * d_model) // tn)) for b in range(4)]
    return pl.pallas_call(
        functools.partial(_merge_body, n_first=n_first),
        grid=((mp + ms) // tm, d_model // tn),
        in_specs=p_specs + s_specs + w_specs + g_specs,
        out_specs=pl.BlockSpec((tm, tn), lambda i, j: (i, j)),
        out_shape=jax.ShapeDtypeStruct((mp + ms, d_model), BF16),
        compiler_params=_cparams(2, 40),
        name="merge",
    )(*outs_p, *outs_s, *wbs, proj, proj, proj, proj)


def _matmul_res_body(a_ref, w_ref, r_ref, o_ref):
    o_ref[...] = r_ref[...] + jnp.dot(a_ref[...], w_ref[...], preferred_element_type=F32)


def _matmul_res(a, w, res):
    m, k = a.shape
    n = w.shape[1]
    tm = _tile(m, 512, 16)
    tn = _tile(n, 1024, LANES)
    return pl.pallas_call(
        _matmul_res_body,
        grid=(m // tm, n // tn),
        in_specs=[pl.BlockSpec((tm, k), lambda i, j: (i, 0)),
                  pl.BlockSpec((k, tn), lambda i, j: (0, j)),
                  pl.BlockSpec((tm, tn), lambda i, j: (i, j))],
        out_specs=pl.BlockSpec((tm, tn), lambda i, j: (i, j)),
        out_shape=jax.ShapeDtypeStruct((m, n), F32),
        compiler_params=_cparams(2, 40),
        name="out_proj",
    )(a, w, res)


def _mlp_body(x_ref, g_ref, wu_ref, wd_ref, fg_ref, o_ref, xn_ref, acc_ref, *, n_chunks, normed):
    c = pl.program_id(1)

    @pl.when(c == 0)
    def _():
        x = x_ref[...]
        ms = jnp.mean(x * x, axis=-1, keepdims=True)
        xn_ref[...] = (x * lax.rsqrt(ms + RMS_EPS) * g_ref[...]).astype(BF16)
        acc_ref[...] = jnp.zeros_like(acc_ref)

    h = jnp.dot(xn_ref[...], wu_ref[...], preferred_element_type=F32)
    h = jnp.square(jnp.maximum(h, 0.0))
    acc_ref[...] += jnp.dot(h.astype(BF16), wd_ref[...], preferred_element_type=F32)

    @pl.when(c == n_chunks - 1)
    def _():
        xo = x_ref[...] + acc_ref[...]
        if normed:
            ms = jnp.mean(xo * xo, axis=-1, keepdims=True)
            xo = xo * lax.rsqrt(ms + RMS_EPS) * fg_ref[...]
        o_ref[...] = xo


def _mlp(x, g, wu, wd, final_g, normed):
    m, d = x.shape
    f = wu.shape[1]
    tm = _tile(m, 512, SUBLANES)
    tc = _tile(f, 1024, LANES)
    n_chunks = f // tc
    vmem = (2 * tm * d * 4 + 4 * d * tc * 2 + 2 * tm * d * 4 + tm * d * 2 + tm * d * 4) / 2**20 + 12
    return pl.pallas_call(
        functools.partial(_mlp_body, n_chunks=n_chunks, normed=normed),
        grid=(m // tm, n_chunks),
        in_specs=[pl.BlockSpec((tm, d), lambda i, c: (i, 0)),
                  pl.BlockSpec((1, d), lambda i, c: (0, 0)),
                  pl.BlockSpec((d, tc), lambda i, c: (0, c)),
                  pl.BlockSpec((tc, d), lambda i, c: (c, 0)),
                  pl.BlockSpec((1, d), lambda i, c: (0, 0))],
        out_specs=pl.BlockSpec((tm, d), lambda i, c: (i, 0)),
        out_shape=jax.ShapeDtypeStruct((m, d), F32),
        scratch_shapes=[pltpu.VMEM((tm, d), BF16), pltpu.VMEM((tm, d), F32)],
        compiler_params=_cparams(2, vmem),
        name="mlp",
    )(x, g, wu, wd, final_g)


def _cumsum_rows(x, tc):
    if tc >= LANES:
        r = lax.broadcasted_iota(jnp.int32, (tc, tc), 0)
        c = lax.broadcasted_iota(jnp.int32, (tc, tc), 1)
        tri = jnp.where(c <= r, 1.0, 0.0).astype(BF16)
        return _dot_sel_lhs(tri, x)
    rows = lax.broadcasted_iota(jnp.int32, x.shape, 0)
    parts = [jnp.sum(jnp.where(rows <= t, x, 0.0), axis=0, keepdims=True) for t in range(tc)]
    return jnp.concatenate(parts, axis=0)


def _logf_cumsum_body(ff_ref, fb_ref, c0_ref, lf_ref, c_ref, carry_ref, *, tc):
    @pl.when(pl.program_id(1) == 0)
    def _():
        carry_ref[...] = c0_ref[...]

    lf = -_softplus(-(ff_ref[...] + fb_ref[...]))
    lf_ref[...] = lf
    cs = _cumsum_rows(lf, tc) + carry_ref[...]
    c_ref[...] = cs
    carry_ref[...] = cs[tc - 1:tc, :]


def _logf_cumsum(ff, fb, c0):
    nseq, t, h = ff.shape
    tc = _tile(t, 512, SUBLANES)
    blk = pl.BlockSpec((None, tc, h), lambda s, c: (s, c, 0))
    return pl.pallas_call(
        functools.partial(_logf_cumsum_body, tc=tc),
        grid=(nseq, t // tc),
        in_specs=[blk, pl.BlockSpec((1, h), lambda s, c: (0, 0)),
                  pl.BlockSpec((None, 1, h), lambda s, c: (s, 0, 0))],
        out_specs=[blk, blk],
        out_shape=[jax.ShapeDtypeStruct((nseq, t, h), F32)] * 2,
        scratch_shapes=[pltpu.VMEM((1, h), F32)],
        compiler_params=_cparams(2, 24),
        name="logf_cumsum",
    )(ff, fb, c0)


def _t5_bucket(n):
    max_exact = NUM_BUCKETS // 2
    nf = jnp.maximum(n, 1).astype(F32)
    large = max_exact + (jnp.log(nf / max_exact) / math.log(MAX_DISTANCE / max_exact)
                         * (NUM_BUCKETS - max_exact)).astype(jnp.int32)
    return jnp.where(n < max_exact, n, jnp.minimum(large, NUM_BUCKETS - 1))


def _t5_lookup(bucket, tab_ref, col):
    acc = jnp.zeros(bucket.shape, F32)
    for j in range(NUM_BUCKETS):
        acc = jnp.where(bucket == j, tab_ref[j, col], acc)
    return acc


def _softmax_update(s, v, m_ref, l_ref, acc_ref, idx):
    m_prev = m_ref[idx]
    m_new = jnp.maximum(m_prev, jnp.max(s, axis=-1, keepdims=True))
    alpha = jnp.exp(m_prev - m_new)
    p = jnp.exp(s - pltpu.repeat(m_new, s.shape[-1] // LANES, axis=1))
    l_ref[idx] = alpha * l_ref[idx] + jnp.sum(p, axis=-1, keepdims=True)
    acc_ref[idx] = (alpha[:, :v.shape[-1]] * acc_ref[idx]
                    + jnp.dot(p.astype(BF16), v, preferred_element_type=F32))
    m_ref[idx] = m_new


def _fox_flash_body(q_ref, k_ref, v_ref, c_ref, ct_ref, o_ref, m_ref, l_ref, acc_ref, *, tq):
    qi = pl.program_id(1)
    ki = pl.program_id(2)

    @pl.when(ki == 0)
    def _():
        m_ref[...] = jnp.full(m_ref.shape, NEG_INF, F32)
        l_ref[...] = jnp.zeros_like(l_ref)
        acc_ref[...] = jnp.zeros_like(acc_ref)

    first = lax.broadcasted_iota(jnp.int32, (tq, LANES), 1) < HEAD_DIM

    def tiles(masked):
        if masked:
            row = lax.broadcasted_iota(jnp.int32, (tq, tq), 0)
            col = lax.broadcasted_iota(jnp.int32, (tq, tq), 1)
            keep = col <= row
        for g in range(H_FOX // 2):
            grp = slice(g * LANES, (g + 1) * LANES)
            q2 = q_ref[:, grp] * QK_SCALE
            k2 = k_ref[:, grp].astype(BF16)
            v2 = v_ref[:, grp].astype(BF16)
            for j in range(2):
                h = 2 * g + j
                own = first if j == 0 else jnp.logical_not(first)
                s = _dot_nt(jnp.where(own, q2, 0.0).astype(BF16), k2)
                s = s + c_ref[:, h:h + 1] - ct_ref[h:h + 1, :]
                if masked:
                    s = jnp.where(keep, s, NEG_INF)
                m_prev = m_ref[h]
                m_new = jnp.maximum(m_prev, jnp.max(s, axis=-1, keepdims=True))
                alpha = jnp.exp(m_prev - m_new)
                p = jnp.exp(s - pltpu.repeat(m_new, tq // LANES, axis=1))
                l_ref[h] = alpha * l_ref[h] + jnp.sum(p, axis=-1, keepdims=True)
                pv = jnp.dot(p.astype(BF16), v2, preferred_element_type=F32)
                acc_ref[h] = alpha * acc_ref[h] + pv
                m_ref[h] = m_new

    @pl.when(ki < qi)
    def _():
        tiles(False)

    @pl.when(ki == qi)
    def _():
        tiles(True)
        for g in range(H_FOX // 2):
            o_ref[:, g * LANES:(g + 1) * LANES] = jnp.where(
                first, acc_ref[2 * g] / l_ref[2 * g], acc_ref[2 * g + 1] / l_ref[2 * g + 1])


def _qkv_specs(t, tq, col0, w):
    nq = t // tq
    c = col0 // w
    return [pl.BlockSpec((tq, w), lambda bi, qi, ki: (bi * nq + qi, c)),
            pl.BlockSpec((tq, w), lambda bi, qi, ki: (bi * nq + jnp.minimum(ki, qi), c + 1)),
            pl.BlockSpec((tq, w), lambda bi, qi, ki: (bi * nq + jnp.minimum(ki, qi), c + 2))]


def _fox_flash(proj, c, ct, b, t):
    w = W_BRANCH
    tq = _tile(t, 512, LANES)
    nq = t // tq
    return pl.pallas_call(
        functools.partial(_fox_flash_body, tq=tq),
        grid=(b, nq, nq),
        in_specs=_qkv_specs(t, tq, OFF_FQ, w) + [
            pl.BlockSpec((None, tq, H_FOX), lambda bi, qi, ki: (bi, qi, 0)),
            pl.BlockSpec((None, H_FOX, tq), lambda bi, qi, ki: (bi, 0, jnp.minimum(ki, qi)))],
        out_specs=pl.BlockSpec((tq, w), lambda bi, qi, ki: (bi * nq + qi, 0)),
        out_shape=jax.ShapeDtypeStruct((b * t, w), F32),
        scratch_shapes=[pltpu.VMEM((H_FOX, tq, LANES), F32), pltpu.VMEM((H_FOX, tq, LANES), F32),
                        pltpu.VMEM((H_FOX, tq, LANES), F32)],
        compiler_params=_cparams(3, 48),
        name="fox_flash",
    )(proj, proj, proj, c, ct)


def _lambda(lq1_ref, lk1_ref, lq2_ref, lk2_ref, lam_init):
    s1 = jnp.sum(lq1_ref[...] * lk1_ref[...], axis=-1, keepdims=True)
    s2 = jnp.sum(lq2_ref[...] * lk2_ref[...], axis=-1, keepdims=True)
    return jnp.exp(s1) - jnp.exp(s2) + lam_init


def _subln(o, g, lam_init):
    ms = jnp.mean(o * o, axis=-1, keepdims=True)
    return o * lax.rsqrt(ms + SUBLN_EPS) * g * (1.0 - lam_init)


def _diff_flash_body(tab_ref, q_ref, k_ref, v_ref, lq1_ref, lk1_ref, lq2_ref, lk2_ref, g_ref,
                     o_ref, bias_ref, m_ref, l_ref, acc_ref, *, tq, lam_init):
    bi = pl.program_id(0)
    qi = pl.program_id(1)
    ki = pl.program_id(2)

    @pl.when((bi == 0) & (qi == 0) & (ki == 0))
    def _():
        def fill(rb, carry):
            r0 = pl.multiple_of(rb * SUBLANES, SUBLANES)
            rows = r0 + lax.broadcasted_iota(jnp.int32, (SUBLANES, tq), 0)
            cols = lax.broadcasted_iota(jnp.int32, (SUBLANES, tq), 1)
            for which in range(2):
                bucket = _t5_bucket(jnp.maximum(rows - cols + which * tq, 0))
                for c in range(2 * H_DIFF):
                    bias_ref[c, which, pl.ds(r0, SUBLANES), :] = _t5_lookup(bucket, tab_ref, c)
            return carry
        lax.fori_loop(0, tq // SUBLANES, fill, 0)

    @pl.when(ki == 0)
    def _():
        m_ref[...] = jnp.full(m_ref.shape, NEG_INF, F32)
        l_ref[...] = jnp.zeros_like(l_ref)
        acc_ref[...] = jnp.zeros_like(acc_ref)

    def tiles(mode):
        lane = lax.broadcasted_iota(jnp.int32, (tq, 2 * HEAD_DIM), 1)
        if mode == 0:
            row = lax.broadcasted_iota(jnp.int32, (tq, tq), 0)
            col = lax.broadcasted_iota(jnp.int32, (tq, tq), 1)
            keep = col <= row
        for h in range(H_DIFF):
            grp = slice(h * 2 * HEAD_DIM, (h + 1) * 2 * HEAD_DIM)
            q = q_ref[:, grp] * QK_SCALE
            k = k_ref[:, grp].astype(BF16)
            v = v_ref[:, grp].astype(BF16)
            for mp in range(2):
                c = mp * H_DIFF + h
                qm = jnp.where((lane >= mp * HEAD_DIM) & (lane < (mp + 1) * HEAD_DIM), q, 0.0)
                s = _dot_nt(qm.astype(BF16), k)
                if mode == 0:
                    s = jnp.where(keep, s + bias_ref[c, 0], NEG_INF)
                elif mode == 1:
                    s = s + bias_ref[c, 1]
                else:
                    s = s + tab_ref[NUM_BUCKETS - 1, c]
                _softmax_update(s, v, m_ref, l_ref, acc_ref, c)

    @pl.when(ki < qi - 1)
    def _():
        tiles(2)

    @pl.when(ki == qi - 1)
    def _():
        tiles(1)

    @pl.when(ki == qi)
    def _():
        tiles(0)
        lam = _lambda(lq1_ref, lk1_ref, lq2_ref, lk2_ref, lam_init)
        for h in range(H_DIFF):
            o = acc_ref[h] / l_ref[h] - lam * (acc_ref[H_DIFF + h] / l_ref[H_DIFF + h])
            o_ref[:, h * 2 * HEAD_DIM:(h + 1) * 2 * HEAD_DIM] = _subln(o, g_ref[...], lam_init)


def _diff_flash(table, proj, lq1, lk1, lq2, lk2, g, lam_init, b, t):
    w = W_BRANCH
    d = 2 * HEAD_DIM
    h = H_DIFF
    tq = _tile(t, 512, LANES)
    assert tq >= MAX_DISTANCE or tq == t, "far blocks must lie in the last bucket"
    nq = t // tq
    vec = pl.BlockSpec((1, HEAD_DIM), lambda bi, qi, ki: (0, 0))
    return pl.pallas_call(
        functools.partial(_diff_flash_body, tq=tq, lam_init=lam_init),
        grid=(b, nq, nq),
        in_specs=[pl.BlockSpec(memory_space=pltpu.SMEM)] + _qkv_specs(t, tq, OFF_DQ, w) + [
            vec, vec, vec, vec, pl.BlockSpec((1, d), lambda bi, qi, ki: (0, 0))],
        out_specs=pl.BlockSpec((tq, w), lambda bi, qi, ki: (bi * nq + qi, 0)),
        out_shape=jax.ShapeDtypeStruct((b * t, w), F32),
        scratch_shapes=[pltpu.VMEM((2 * h, 2, tq, tq), F32),
                        pltpu.VMEM((2 * h, tq, LANES), F32), pltpu.VMEM((2 * h, tq, LANES), F32),
                        pltpu.VMEM((2 * h, tq, d), F32)],
        compiler_params=_cparams(3, 54),
        name="diff_flash",
    )(table, proj, proj, proj, lq1, lk1, lq2, lk2, g)


def _softmax_tiles(tiles):
    m = tiles[0][0].max(axis=-1, keepdims=True)
    for s, _ in tiles[1:]:
        m = jnp.maximum(m, s.max(axis=-1, keepdims=True))
    l, acc = None, None
    for s, pv in tiles:
        p = jnp.exp(s - m)
        ps, pa = jnp.sum(p, axis=-1, keepdims=True), pv(p.astype(BF16))
        l, acc = (ps, pa) if l is None else (l + ps, acc + pa)
    return acc / l


def _page_specs(layer, n_pages, rows, cols):
    return [pl.BlockSpec((None, None, rows, cols),
                         functools.partial(lambda b, pt, p: (layer, pt[b, p], 0, 0), p=p))
            for p in range(n_pages)]


def _fox_dec_body(pt_ref, qb_ref, *refs, n_pages, ts):
    kt = refs[:n_pages]
    vt = refs[n_pages:2 * n_pages]
    lf = refs[2 * n_pages:3 * n_pages]
    knt_ref, vnt_ref, cnb_ref, cnt_ref, o_ref = refs[3 * n_pages:]
    rows = H_FOX * ts
    page = knt_ref.shape[-1]
    qb = qb_ref[...]
    qk = lambda k_ref: jnp.dot(qb, k_ref[...].astype(BF16), preferred_element_type=F32)
    pv = lambda v_ref: (lambda p: _dot_nt(p, v_ref[...].astype(BF16)))

    lane = lax.broadcasted_iota(jnp.int32, (ts, H_FOX, page), 2).reshape(rows, page)
    tok = lax.broadcasted_iota(jnp.int32, (ts, H_FOX, page), 0).reshape(rows, page)
    cn_keys = jnp.concatenate([cnt_ref[...]] * ts, axis=0)
    bias_new = jnp.where((lane <= tok) & (lane < ts), cnb_ref[...] - cn_keys, NEG_INF)
    tiles = [(qk(knt_ref) + bias_new, pv(vnt_ref))]

    j_idx = lax.broadcasted_iota(jnp.int32, (page, page), 0)
    k_idx = lax.broadcasted_iota(jnp.int32, (page, page), 1)
    later = jnp.where(j_idx > k_idx, 1.0, 0.0).astype(BF16)
    in_page = [_dot_sel_rhs(lf[p][...], later) for p in range(n_pages)]
    carry = jnp.zeros((H_FOX, 1), F32)
    for p in reversed(range(n_pages)):
        suffix = in_page[p] + carry
        carry = suffix[:, 0:1] + lf[p][:, 0:1]
        bias = jnp.concatenate([suffix] * ts, axis=0) + cnb_ref[...]
        tiles.append((qk(kt[p]) + bias, pv(vt[p])))

    accn = _softmax_tiles(tiles).reshape(ts, H_FOX, W_BRANCH)
    col_head = lax.shift_right_logical(lax.broadcasted_iota(jnp.int32, (H_FOX, W_BRANCH), 1),
                                       HEAD_DIM.bit_length() - 1)
    own = col_head == lax.broadcasted_iota(jnp.int32, (H_FOX, W_BRANCH), 0)
    o_ref[...] = jnp.sum(jnp.where(own[None], accn, 0.0), axis=1)


def _fox_decode(page_table, qb, kt, vt, lft, knt, vnt, cnb, cnt, layer, ts):
    db, rows, w = qb.shape
    page = kt.shape[-1]
    n_pages = page_table.shape[1]
    per_seq = lambda r, c: pl.BlockSpec((None, r, c), lambda b, pt: (b, 0, 0))
    grid_spec = pltpu.PrefetchScalarGridSpec(
        num_scalar_prefetch=1,
        grid=(db,),
        in_specs=([per_seq(rows, w)] + _page_specs(layer, n_pages, w, page) * 2
                  + _page_specs(layer, n_pages, H_FOX, page)
                  + [per_seq(w, page), per_seq(w, page), per_seq(rows, page), per_seq(H_FOX, page)]),
        out_specs=per_seq(ts, w))
    return pl.pallas_call(
        functools.partial(_fox_dec_body, n_pages=n_pages, ts=ts),
        grid_spec=grid_spec,
        out_shape=jax.ShapeDtypeStruct((db, ts, w), F32),
        compiler_params=_cparams(1, 40),
        name="fox_decode",
    )(page_table, qb, *([kt] * n_pages), *([vt] * n_pages), *([lft] * n_pages), knt, vnt, cnb, cnt)


def _diff_dec_body(pt_ref, tab_ref, q_ref, *refs, n_pages, ts, page, lam_init):
    kp = refs[:n_pages]
    vp = refs[n_pages:2 * n_pages]
    (kn_ref, vn_ref, lq1_ref, lk1_ref, lq2_ref, lk2_ref, g_ref, o_ref, bias_ref) = refs[2 * n_pages:]
    rows = 2 * H_DIFF * ts
    cols = page * H_DIFF
    past = n_pages * page

    @pl.when(pl.program_id(0) == 0)
    def _():
        lane = lax.broadcasted_iota(jnp.int32, (ts, cols), 1)
        pos = lax.shift_right_logical(lane, H_DIFF.bit_length() - 1)
        head = lane & (H_DIFF - 1)
        qpos = past + lax.broadcasted_iota(jnp.int32, (ts, cols), 0)

        def fill(tile, kpos, valid):
            bucket = _t5_bucket(jnp.maximum(qpos - kpos, 0))
            for rb in range(2 * H_DIFF):
                h, mp = rb // 2, rb % 2
                val = _t5_lookup(bucket, tab_ref, mp * H_DIFF + h)
                bias_ref[tile, rb * ts:(rb + 1) * ts, :] = jnp.where(valid & (head == h), val, NEG_INF)

        def past_tile(tile, carry):
            kpos = tile * page + pos
            fill(tile, kpos, kpos <= qpos)
            return carry

        lax.fori_loop(0, n_pages, past_tile, 0)
        kpos = past + pos
        fill(n_pages, kpos, (kpos <= qpos) & (pos < ts))

    q = q_ref[...]
    tiles = []
    for p in range(n_pages + 1):
        k_ref, v_ref = (kp[p], vp[p]) if p < n_pages else (kn_ref, vn_ref)
        s = _dot_nt(q, k_ref[...].astype(BF16)) + bias_ref[p]
        tiles.append((s, lambda pr, v_ref=v_ref: jnp.dot(pr, v_ref[...].astype(BF16),
                                                         preferred_element_type=F32)))

    lam = _lambda(lq1_ref, lk1_ref, lq2_ref, lk2_ref, lam_init)
    on = _softmax_tiles(tiles)
    for h in range(H_DIFF):
        o1 = on[2 * h * ts:(2 * h + 1) * ts, :]
        o2 = on[(2 * h + 1) * ts:(2 * h + 2) * ts, :]
        o_ref[:, h * 2 * HEAD_DIM:(h + 1) * 2 * HEAD_DIM] = _subln(o1 - lam * o2, g_ref[...], lam_init)


def _diff_decode(page_table, table, q, kp, vp, kn, vn, lq1, lk1, lq2, lk2, g, layer, ts, lam_init):
    db, rows, d = q.shape
    cols = kp.shape[2]
    page = cols // H_DIFF
    n_pages = page_table.shape[1]
    per_seq = lambda r, c: pl.BlockSpec((None, r, c), lambda b, pt: (b, 0, 0))
    vec = pl.BlockSpec((1, HEAD_DIM), lambda b, pt: (0, 0))
    grid_spec = pltpu.PrefetchScalarGridSpec(
        num_scalar_prefetch=1,
        grid=(db,),
        in_specs=([pl.BlockSpec(memory_space=pltpu.SMEM), per_seq(rows, d)]
                  + _page_specs(layer, n_pages, cols, d) * 2
                  + [per_seq(cols, d), per_seq(cols, d), vec, vec, vec, vec,
                     pl.BlockSpec((1, d), lambda b, pt: (0, 0))]),
        out_specs=per_seq(ts, H_DIFF * d),
        scratch_shapes=[pltpu.VMEM((n_pages + 1, rows, cols), F32)])
    return pl.pallas_call(
        functools.partial(_diff_dec_body, n_pages=n_pages, ts=ts, page=page, lam_init=lam_init),
        grid_spec=grid_spec,
        out_shape=jax.ShapeDtypeStruct((db, ts, H_DIFF * d), F32),
        compiler_params=_cparams(1, 40),
        name="diff_decode",
    )(page_table, table, q, *([kp] * n_pages), *([vp] * n_pages), kn, vn, lq1, lk1, lq2, lk2, g)


def _lru_body(lx_ref, lg_ref, c0_ref, h0_ref, cw_ref, cb_ref, wa_ref, ba_ref, wx_ref, bx_ref,
              lam_ref, o_ref, hn_ref, cin_ref, a_ref, u_ref, hs_ref, hc_ref, *, tc):
    @pl.when(pl.program_id(1) == 0)
    def _():
        cin_ref[0:SUBLANES, :] = jnp.zeros((SUBLANES, W_BRANCH), F32)
        cin_ref[SUBLANES - (CONV_W - 1):SUBLANES, :] = c0_ref[...]
        hc_ref[...] = h0_ref[...]

    lx = lx_ref[...]
    cin_ref[SUBLANES:SUBLANES + tc, :] = lx
    base = SUBLANES - (CONV_W - 1)
    acc = cw_ref[0:1, :] * cin_ref[base:base + tc, :]
    for j in range(1, CONV_W):
        acc = acc + cw_ref[j:j + 1, :] * cin_ref[base + j:base + j + tc, :]
    xc = cb_ref[...] + acc
    xb = xc.astype(BF16)
    r_gate = jax.nn.sigmoid(jnp.dot(xb, wa_ref[...], preferred_element_type=F32) + ba_ref[...])
    i_gate = jax.nn.sigmoid(jnp.dot(xb, wx_ref[...], preferred_element_type=F32) + bx_ref[...])
    log_a = -LRU_C * r_gate * _softplus(-lam_ref[...])
    a_ref[...] = jnp.exp(log_a)
    u_ref[...] = jnp.sqrt(1.0 - jnp.exp(2.0 * log_a)) * (i_gate * xc)

    def step(t, h):
        h = a_ref[pl.ds(t, 1), :] * h + u_ref[pl.ds(t, 1), :]
        hs_ref[pl.ds(t, 1), :] = h
        return h

    h = lax.fori_loop(0, tc, step, hc_ref[...], unroll=8)
    hc_ref[...] = h
    hn_ref[...] = h
    o_ref[...] = hs_ref[...] * jax.nn.gelu(lg_ref[...])
    cin_ref[0:SUBLANES, :] = lx[tc - SUBLANES:tc, :]


def _lru(proj, row0, nseq, t, conv0, h0, layer, cw, cb, wa, ba, wx, bx, lam):
    m_total = proj.shape[0]
    tc = _tile(t, 512, SUBLANES)
    nc = t // tc
    blk0 = row0 // tc
    col = lambda off: (lambda s, c: (blk0 + s * nc + c, off // W_BRANCH))
    vec = pl.BlockSpec((1, W_BRANCH), lambda s, c: (0, 0))
    mat = pl.BlockSpec((W_BRANCH, W_BRANCH), lambda s, c: (0, 0))
    return pl.pallas_call(
        functools.partial(_lru_body, tc=tc),
        grid=(nseq, nc),
        in_specs=[pl.BlockSpec((tc, W_BRANCH), col(OFF_LX)),
                  pl.BlockSpec((tc, W_BRANCH), col(OFF_LG)),
                  pl.BlockSpec((None, None, CONV_W - 1, W_BRANCH), lambda s, c: (layer, s, 0, 0)),
                  pl.BlockSpec((None, None, 1, W_BRANCH), lambda s, c: (layer, s, 0, 0)),
                  pl.BlockSpec((CONV_W, W_BRANCH), lambda s, c: (0, 0)),
                  vec, mat, vec, mat, vec, vec],
        out_specs=[pl.BlockSpec((tc, W_BRANCH), lambda s, c: (s * nc + c, 0)),
                   pl.BlockSpec((None, 1, W_BRANCH), lambda s, c: (s, 0, 0))],
        out_shape=[jax.ShapeDtypeStruct((nseq * t, W_BRANCH), F32),
                   jax.ShapeDtypeStruct((nseq, 1, W_BRANCH), F32)],
        scratch_shapes=[pltpu.VMEM((tc + SUBLANES, W_BRANCH), F32),
                        pltpu.VMEM((tc, W_BRANCH), F32), pltpu.VMEM((tc, W_BRANCH), F32),
                        pltpu.VMEM((tc, W_BRANCH), F32), pltpu.VMEM((1, W_BRANCH), F32)],
        compiler_params=_cparams(2, 32),
        name="lru",
    )(proj, proj, conv0, h0, cw, cb, wa, ba, wx, bx, lam)


def _rwkv_prep_body(rr_ref, rk_ref, rv_ref, lo_ref, sh_ref, mu_ref, w0_ref, w2_ref, a0_ref, a2_ref,
                    g2_ref, kk_ref, ka_ref, rkk_ref, seg_ref,
                    r_o, w_o, k_o, v_o, a_o, b_o, g_o, bon_o, xs_ref, *, tc):
    wb = W_BRANCH

    @pl.when(pl.program_id(1) == 0)
    def _():
        xs_ref[0:SUBLANES, :] = jnp.zeros((SUBLANES, W_RWKV_PROJ), F32)
        xs_ref[SUBLANES - 1:SUBLANES, :] = sh_ref[...]

    xs_ref[SUBLANES:SUBLANES + tc, 0:wb] = rr_ref[...]
    xs_ref[SUBLANES:SUBLANES + tc, wb:2 * wb] = rk_ref[...]
    xs_ref[SUBLANES:SUBLANES + tc, 2 * wb:3 * wb] = rv_ref[...]
    xs_ref[SUBLANES:SUBLANES + tc, 3 * wb:W_RWKV_PROJ] = lo_ref[...]
    cur = xs_ref[SUBLANES:SUBLANES + tc, :]
    prev = xs_ref[SUBLANES - 1:SUBLANES - 1 + tc, :]
    rx = cur + (prev - cur) * mu_ref[...]
    xs_ref[0:SUBLANES, :] = cur[tc - SUBLANES:tc, :]

    rr = rx[:, 0:wb]
    rk = rx[:, wb:2 * wb]
    rv = rx[:, 2 * wb:3 * wb]
    wl_al = rx[:, 3 * wb:3 * wb + LANES]
    gl = rx[:, 3 * wb + LANES:W_RWKV_PROJ]
    dot = functools.partial(jnp.dot, preferred_element_type=F32)
    wd = -_softplus(-(w0_ref[...] + dot(jnp.tanh(wl_al).astype(BF16), w2_ref[...]))) - 0.5
    decay = jnp.exp(-jnp.exp(wd))
    aa = jax.nn.sigmoid(a0_ref[...] + dot(wl_al.astype(BF16), a2_ref[...]))
    gg = dot(jax.nn.sigmoid(gl).astype(BF16), g2_ref[...])
    kk = rk * kk_ref[...]
    norm = jnp.sqrt(_dot_sel_rhs(kk * kk, seg_ref[...]))
    kk = kk / jnp.maximum(norm, 1e-12)
    kh = rk * (1.0 + (aa - 1.0) * ka_ref[...])
    r_o[...] = rr
    w_o[...] = decay
    k_o[...] = kh
    v_o[...] = rv
    a_o[...] = -kk
    b_o[...] = kk * aa
    g_o[...] = gg
    bon_o[...] = _dot_sel_rhs(rr * kh * rkk_ref[...], seg_ref[...]) * rv


def _rwkv_prep(proj, row0, nseq, t, shift0, layer, mu, w0, w2p, a0, a2p, g2, k_k, k_a, r_k, seg):
    tc = _tile(t, 256, SUBLANES)
    nc = t // tc
    blk0 = row0 // tc
    wb = W_BRANCH
    col = lambda off, w: (lambda s, c: (blk0 + s * nc + c, off // w))
    vec = pl.BlockSpec((1, wb), lambda s, c: (0, 0))
    out_blk = pl.BlockSpec((tc, wb), lambda s, c: (s * nc + c, 0))
    lo_w = W_RWKV_PROJ - 3 * wb
    return pl.pallas_call(
        functools.partial(_rwkv_prep_body, tc=tc),
        grid=(nseq, nc),
        in_specs=[pl.BlockSpec((tc, wb), col(OFF_RW, wb)),
                  pl.BlockSpec((tc, wb), col(OFF_RW + wb, wb)),
                  pl.BlockSpec((tc, wb), col(OFF_RW + 2 * wb, wb)),
                  pl.BlockSpec((tc, lo_w), col(OFF_RW + 3 * wb, lo_w)),
                  pl.BlockSpec((None, None, 1, W_RWKV_PROJ), lambda s, c: (layer, s, 0, 0)),
                  pl.BlockSpec((1, W_RWKV_PROJ), lambda s, c: (0, 0)),
                  vec, pl.BlockSpec((LANES, wb), lambda s, c: (0, 0)),
                  vec, pl.BlockSpec((LANES, wb), lambda s, c: (0, 0)),
                  pl.BlockSpec((LANES, wb), lambda s, c: (0, 0)),
                  vec, vec, vec, pl.BlockSpec((wb, wb), lambda s, c: (0, 0))],
        out_specs=[out_blk] * 8,
        out_shape=[jax.ShapeDtypeStruct((nseq * t, wb), F32)] * 8,
        scratch_shapes=[pltpu.VMEM((tc + SUBLANES, W_RWKV_PROJ), F32)],
        compiler_params=_cparams(2, 40),
        name="rwkv_prep",
    )(proj, proj, proj, proj, shift0, mu, w0, w2p, a0, a2p, g2, k_k, k_a, r_k, seg)


def _rwkv_scan_body(r_ref, w_ref, k_ref, v_ref, a_ref, b_ref, s0_ref, y_ref, sn_ref, st_ref,
                    *, nb, tc, n_chunks):
    c = pl.program_id(1)

    @pl.when(c == 0)
    def _():
        st_ref[...] = s0_ref[...]

    n_pairs = H_RWKV // 2
    shape = (HEAD_DIM, LANES)
    lane = lax.broadcasted_iota(jnp.int32, shape, 1)
    first = lane < HEAD_DIM
    eye2 = jnp.where((lane & (HEAD_DIM - 1)) == lax.broadcasted_iota(jnp.int32, shape, 0), 1.0, 0.0)
    half = lambda d: lax.shift_right_logical(lax.broadcasted_iota(jnp.int32, (LANES, LANES), d),
                                             HEAD_DIM.bit_length() - 1)
    ones_bd = jnp.where(half(0) == half(1), 1.0, 0.0).astype(BF16)

    def step(t, states):
        new_states = []
        for s in range(nb):
            rows = [ref[s, pl.ds(t, 1), :] for ref in (r_ref, w_ref, k_ref, v_ref, a_ref, b_ref)]
            y_parts = []
            for p in range(n_pairs):
                r_t, w_t, k_t, v_t, a_t, b_t = (z[:, p * LANES:(p + 1) * LANES] for z in rows)
                st = states[s * n_pairs + p]
                sa = jnp.dot((st * a_t).astype(BF16), ones_bd, preferred_element_type=F32)
                d = eye2 * v_t
                v1 = jnp.sum(jnp.where(first, d, 0.0), axis=1, keepdims=True)
                v2 = jnp.sum(jnp.where(first, 0.0, d), axis=1, keepdims=True)
                st = st * w_t + sa * b_t + jnp.where(first, v1, v2) * k_t
                y = jnp.dot((st * r_t).astype(BF16), ones_bd, preferred_element_type=F32)
                y_parts.append(jnp.sum(eye2 * y, axis=0, keepdims=True))
                new_states.append(st)
            y_ref[s, pl.ds(t, 1), :] = jnp.concatenate(y_parts, axis=1)
        return tuple(new_states)

    init = tuple(st_ref[s, p] for s in range(nb) for p in range(n_pairs))
    final = lax.fori_loop(0, tc, step, init, unroll=16)
    for i, st in enumerate(final):
        st_ref[i // n_pairs, i % n_pairs] = st

    @pl.when(c == n_chunks - 1)
    def _():
        sn_ref[...] = st_ref[...]


def _pack_pairs(s):
    n = s.shape[0]
    s = s.reshape(n, H_RWKV // 2, 2, HEAD_DIM, HEAD_DIM).transpose(0, 1, 3, 2, 4)
    return s.reshape(n, H_RWKV // 2, HEAD_DIM, 2 * HEAD_DIM)


def _unpack_pairs(s):
    n = s.shape[0]
    s = s.reshape(n, H_RWKV // 2, HEAD_DIM, 2, HEAD_DIM).transpose(0, 1, 3, 2, 4)
    return s.reshape(n, H_RWKV, HEAD_DIM, HEAD_DIM)


def _rwkv_scan(r, w, k, v, a, b, s0):
    nseq, t, wd = r.shape
    nb = 2 if nseq % 2 == 0 else 1
    tc = _tile(t, LANES, SUBLANES)
    n_chunks = t // tc
    n_pairs = H_RWKV // 2
    blk = pl.BlockSpec((nb, tc, wd), lambda s, c: (s, c, 0))
    st_blk = pl.BlockSpec((nb, n_pairs, HEAD_DIM, LANES), lambda s, c: (s, 0, 0, 0))
    y, sn = pl.pallas_call(
        functools.partial(_rwkv_scan_body, nb=nb, tc=tc, n_chunks=n_chunks),
        grid=(nseq // nb, n_chunks),
        in_specs=[blk] * 6 + [st_blk],
        out_specs=[blk, st_blk],
        out_shape=[jax.ShapeDtypeStruct((nseq, t, wd), F32),
                   jax.ShapeDtypeStruct((nseq, n_pairs, HEAD_DIM, LANES), F32)],
        scratch_shapes=[pltpu.VMEM((nb, n_pairs, HEAD_DIM, LANES), F32)],
        compiler_params=_cparams(2, 40),
        name="rwkv_scan",
    )(r, w, k, v, a, b, _pack_pairs(s0))
    return y, _unpack_pairs(sn)


def _rwkv_post_body(y_ref, g_ref, bon_ref, lg_ref, lb_ref, seg_ref, o_ref):
    y = y_ref[...]
    inv_n = 1.0 / HEAD_DIM
    mu = _dot_sel_rhs(y, seg_ref[...]) * inv_n
    d = y - mu
    var = _dot_sel_rhs(d * d, seg_ref[...]) * inv_n
    yn = d * lax.rsqrt(var + GN_EPS) * lg_ref[...] + lb_ref[...]
    o_ref[...] = (yn + bon_ref[...]) * g_ref[...]


def _rwkv_post(y, gg, bonus, lnx_g, lnx_b, seg):
    m, wb = y.shape
    tm = _tile(m, 512, SUBLANES)
    blk = pl.BlockSpec((tm, wb), lambda i: (i, 0))
    vec = pl.BlockSpec((1, wb), lambda i: (0, 0))
    return pl.pallas_call(
        _rwkv_post_body,
        grid=(m // tm,),
        in_specs=[blk, blk, blk, vec, vec, pl.BlockSpec((wb, wb), lambda i: (0, 0))],
        out_specs=blk,
        out_shape=jax.ShapeDtypeStruct((m, wb), F32),
        compiler_params=_cparams(1, 24),
        name="rwkv_post",
    )(y, gg, bonus, lnx_g, lnx_b, seg)


def _block_diag(blocks):
    n, bi, bj = blocks.shape
    eye = jnp.eye(n, dtype=blocks.dtype)
    return jnp.einsum('nij,nm->nimj', blocks, eye).reshape(n * bi, n * bj)


def kernel(x_prompt, x_sample, cache_fox_k, cache_fox_v, cache_fox_logf, cache_diff_k, cache_diff_v, state_lru_conv, state_lru_h, state_rwkv_shift, state_rwkv_wkv, page_table, norm1_g, w_in, fox_fb, lam_q1, lam_k1, lam_q2, lam_k2, diff_subln_g, t5_table, lru_conv_w, lru_conv_b, lru_wa, lru_ba, lru_wx, lru_bx, lru_lambda, rwkv_mu, rwkv_w0, rwkv_w2, rwkv_a0, rwkv_a2, rwkv_g2, rwkv_k_k, rwkv_k_a, rwkv_r_k, rwkv_lnx_g, rwkv_lnx_b, wb_fox, wb_diff, wb_lru, wb_rwkv, w_out, norm2_g, w_up, w_down, final_g):
    B, T, D = x_prompt.shape
    DB, TS, _ = x_sample.shape
    L = w_in.shape[0]
    n_pool, page = cache_fox_k.shape[1], cache_fox_k.shape[2]
    n_pages = page_table.shape[1]
    mp, ms = B * T, DB * TS
    wb = W_BRANCH
    assert TS % SUBLANES == 0 and TS <= page and T % SUBLANES == 0

    x = jnp.concatenate([x_prompt.reshape(mp, D), x_sample.reshape(ms, D)], axis=0)
    kt_fox = cache_fox_k.transpose(0, 1, 3, 4, 2).reshape(L, n_pool, wb, page)
    vt_fox = cache_fox_v.transpose(0, 1, 3, 4, 2).reshape(L, n_pool, wb, page)
    lft_fox = cache_fox_logf.transpose(0, 1, 3, 2)
    kp_diff = cache_diff_k.reshape(L, n_pool, page * H_DIFF, 2 * HEAD_DIM)
    vp_diff = cache_diff_v.reshape(L, n_pool, page * H_DIFF, 2 * HEAD_DIM)
    seg = _block_diag(jnp.ones((H_RWKV, HEAD_DIM, HEAD_DIM), BF16))
    zeros_conv = jnp.zeros((L, B, CONV_W - 1, wb), F32)
    zeros_h = jnp.zeros((L, B, 1, wb), F32)
    zeros_shift = jnp.zeros((L, B, 1, W_RWKV_PROJ), F32)
    zeros_wkv = jnp.zeros((B, H_RWKV, HEAD_DIM, HEAD_DIM), F32)
    eye_f = jnp.eye(H_FOX, dtype=F32)
    eye_2 = jnp.eye(2, dtype=F32)
    row = lambda v: v.reshape(1, -1)

    def pad_rows(z, n):
        return jnp.pad(z, ((0, 0), (0, n - z.shape[1]), (0, 0)))

    def pad_lanes(z):
        return jnp.pad(z, ((0, 0), (0, 0), (0, page - z.shape[2])))

    p_states, s_states = [], []
    for l in range(L):
        lam_init = 0.8 - 0.6 * math.exp(-0.3 * l)
        w = w_in[l]
        n_a = OFF_FF
        split_ff = 3 * wb
        w_r = jnp.concatenate(
            [w[:, :split_ff], w[:, split_ff + H_FOX:split_ff + H_FOX + (n_a - split_ff)],
             w[:, split_ff:split_ff + H_FOX], jnp.zeros((D, FF_PAD), F32),
             w[:, split_ff + H_FOX + (n_a - split_ff):]], axis=1).astype(BF16)
        proj = _rms_matmul(x, row(norm1_g[l]), w_r)
        pp, ps = proj[:mp], proj[mp:]

        lf_p, c_p = _logf_cumsum(pp[:, OFF_FF:OFF_FF + H_FOX].reshape(B, T, H_FOX), row(fox_fb[l]),
                                 jnp.zeros((B, 1, H_FOX), F32))
        lf_s, cn = _logf_cumsum(ps[:, OFF_FF:OFF_FF + H_FOX].reshape(DB, TS, H_FOX), row(fox_fb[l]),
                                jnp.zeros((DB, 1, H_FOX), F32))

        o_fox_p = _fox_flash(proj, c_p, c_p.transpose(0, 2, 1), B, T)

        fq_s = ps[:, OFF_FQ:OFF_FQ + wb].reshape(DB, TS, H_FOX, HEAD_DIM) * QK_SCALE
        qb = jnp.einsum('bthd,hg->bthgd', fq_s, eye_f).reshape(DB, TS * H_FOX, wb).astype(BF16)
        cnb = jnp.broadcast_to(cn.reshape(DB, TS * H_FOX, 1), (DB, TS * H_FOX, page))
        cnt = pad_lanes(cn.transpose(0, 2, 1))
        knt = pad_lanes(ps[:, OFF_FK:OFF_FK + wb].reshape(DB, TS, wb).transpose(0, 2, 1))
        vnt = pad_lanes(ps[:, OFF_FV:OFF_FV + wb].reshape(DB, TS, wb).transpose(0, 2, 1))
        o_fox_s = _fox_decode(page_table, qb, kt_fox, vt_fox, lft_fox, knt, vnt, cnb, cnt, l, TS).reshape(ms, wb)

        lam_vecs = (row(lam_q1[l]), row(lam_k1[l]), row(lam_q2[l]), row(lam_k2[l]))
        g_sub = row(diff_subln_g[l])
        o_diff_p = _diff_flash(t5_table, proj, *lam_vecs, g_sub, lam_init, B, T)

        dq_s = ps[:, OFF_DQ:OFF_DQ + wb].reshape(DB, TS, H_DIFF, 2, HEAD_DIM) * QK_SCALE
        qb_d = jnp.einsum('bthmd,mn->bhmtnd', dq_s, eye_2).reshape(
            DB, 2 * H_DIFF * TS, 2 * HEAD_DIM).astype(BF16)
        kn_d = pad_rows(ps[:, OFF_DK:OFF_DK + wb].reshape(DB, TS * H_DIFF, 2 * HEAD_DIM), page * H_DIFF)
        vn_d = pad_rows(ps[:, OFF_DV:OFF_DV + wb].reshape(DB, TS * H_DIFF, 2 * HEAD_DIM), page * H_DIFF)
        o_diff_s = _diff_decode(page_table, t5_table, qb_d, kp_diff, vp_diff, kn_d, vn_d,
                                *lam_vecs, g_sub, l, TS, lam_init).reshape(ms, wb)

        lru_w = (lru_conv_w[l], row(lru_conv_b[l]), _block_diag(lru_wa[l]).astype(BF16), row(lru_ba[l]),
                 _block_diag(lru_wx[l]).astype(BF16), row(lru_bx[l]), row(lru_lambda[l]))
        o_lru_p, h_p = _lru(proj, 0, B, T, zeros_conv, zeros_h, l, *lru_w)
        o_lru_s, h_s = _lru(proj, mp, DB, TS, state_lru_conv, state_lru_h.reshape(L, DB, 1, wb), l, *lru_w)

        rank = rwkv_w2.shape[1]
        w2p = jnp.concatenate([rwkv_w2[l], jnp.zeros((LANES - rank, wb), F32)], axis=0).astype(BF16)
        a2p = jnp.concatenate([jnp.zeros((LANES - rwkv_a2.shape[1], wb), F32), rwkv_a2[l]], axis=0).astype(BF16)
        rw_w = (row(rwkv_mu[l]), row(rwkv_w0[l]), w2p, row(rwkv_a0[l]), a2p, rwkv_g2[l].astype(BF16),
                row(rwkv_k_k[l]), row(rwkv_k_a[l]), row(rwkv_r_k[l]), seg)
        prep_p = _rwkv_prep(proj, 0, B, T, zeros_shift, l, *rw_w)
        prep_s = _rwkv_prep(proj, mp, DB, TS, state_rwkv_shift.reshape(L, DB, 1, W_RWKV_PROJ), l, *rw_w)
        y_p, wkv_p = _rwkv_scan(*(z.reshape(B, T, wb) for z in prep_p[:6]), zeros_wkv)
        y_s, wkv_s = _rwkv_scan(*(z.reshape(DB, TS, wb) for z in prep_s[:6]), state_rwkv_wkv[l])
        lnx = (row(rwkv_lnx_g[l]), row(rwkv_lnx_b[l]), seg)
        o_rwkv_p = _rwkv_post(y_p.reshape(mp, wb), prep_p[6], prep_p[7], *lnx)
        o_rwkv_s = _rwkv_post(y_s.reshape(ms, wb), prep_s[6], prep_s[7], *lnx)

        wbs = (wb_fox[l].astype(BF16), wb_diff[l].astype(BF16), wb_lru[l].astype(BF16), wb_rwkv[l].astype(BF16))
        mixed = _merge((o_fox_p, o_diff_p, o_lru_p, o_rwkv_p), (o_fox_s, o_diff_s, o_lru_s, o_rwkv_s),
                       wbs, proj, D)
        x = _matmul_res(mixed, w_out[l].astype(BF16), x)
        x = _mlp(x, row(norm2_g[l]), w_up[l].astype(BF16), w_down[l].astype(BF16), row(final_g),
                 normed=(l == L - 1))

        def states(pz, n, t, lf, conv0, h_last, wkv):
            lx = pz[:, OFF_LX:OFF_LX + wb].reshape(n, t, wb)
            conv_in = jnp.concatenate([conv0, lx], axis=1)
            rw = pz[:, OFF_RW:OFF_RW + W_RWKV_PROJ].reshape(n, t, W_RWKV_PROJ)
            return (pz[:, OFF_FK:OFF_FK + wb].reshape(n, t, H_FOX, HEAD_DIM),
                    pz[:, OFF_FV:OFF_FV + wb].reshape(n, t, H_FOX, HEAD_DIM),
                    lf,
                    pz[:, OFF_DK:OFF_DK + wb].reshape(n, t, H_DIFF, 2 * HEAD_DIM),
                    pz[:, OFF_DV:OFF_DV + wb].reshape(n, t, H_DIFF, 2 * HEAD_DIM),
                    conv_in[:, -(CONV_W - 1):], h_last.reshape(n, wb), rw[:, -1], wkv)

        p_states.append(states(pp, B, T, lf_p, zeros_conv[0], h_p, wkv_p))
        s_states.append(states(ps, DB, TS, lf_s, state_lru_conv[l], h_s, wkv_s))

    y_prompt = x[:mp].reshape(B, T, D)
    y_sample = x[mp:].reshape(DB, TS, D)
    p_out = [jnp.stack(z) for z in zip(*p_states)]
    s_out = [jnp.stack(z) for z in zip(*s_states)]
    return (y_prompt, y_sample, *p_out, *s_out)
```

```python
import functools
import math

import jax
import jax.numpy as jnp
from jax import lax
from jax.experimental import pallas as pl
from jax.experimental.pallas import tpu as pltpu

F32 = jnp.float32
BF16 = jnp.bfloat16

HEAD_DIM = 64
W_BRANCH = 512
H_FOX = 8
H_DIFF = 4
H_RWKV = 8
CONV_W = 4
LRU_C = 8.0
NUM_BUCKETS = 32
MAX_DISTANCE = 128
RMS_EPS = 1e-6
SUBLN_EPS = 1e-5
GN_EPS = 64e-5
NEG_INF = -1e30
QK_SCALE = HEAD_DIM ** -0.5

LANES = 128
SUBLANES = 8
VMEM_LIMIT_CAP = 56 * 1024 * 1024

OFF_FQ, OFF_FK, OFF_FV = 0, 512, 1024
OFF_DQ, OFF_DK, OFF_DV = 1536, 2048, 2560
OFF_LX, OFF_LG = 3072, 3584
OFF_RW = 4096
W_RWKV_PROJ = 1792
OFF_FF = 5888
OFF_GATE = 6144
FF_PAD = OFF_GATE - OFF_FF - H_FOX


def _tile(n, pref, mult):
    best = None
    t = mult
    while t <= min(n, pref):
        if n % t == 0:
            best = t
        t += mult
    return best if best is not None else n


def _cparams(n_axes, vmem_mb):
    return pltpu.CompilerParams(
        dimension_semantics=("arbitrary",) * n_axes,
        vmem_limit_bytes=min(int(vmem_mb * 1024 * 1024), VMEM_LIMIT_CAP))


def _softplus(x):
    return jnp.maximum(x, 0.0) + jnp.log1p(jnp.exp(-jnp.abs(x)))


def _split3(x):
    def top(v):
        bits = lax.bitcast_convert_type(v, jnp.uint32) & jnp.uint32(0xFFFF0000)
        return lax.bitcast_convert_type(bits, F32)

    x1 = top(x)
    r1 = x - x1
    x2 = top(r1)
    x3 = r1 - x2
    return x1.astype(BF16), x2.astype(BF16), x3.astype(BF16)


def _dot_sel_rhs(x, sel):
    x1, x2, x3 = _split3(x)
    d = functools.partial(jnp.dot, preferred_element_type=F32)
    return d(x1, sel) + d(x2, sel) + d(x3, sel)


def _dot_sel_lhs(sel, x):
    x1, x2, x3 = _split3(x)
    d = functools.partial(jnp.dot, preferred_element_type=F32)
    return d(sel, x1) + d(sel, x2) + d(sel, x3)


def _dot_nt(a, b):
    return lax.dot_general(a, b, (((1,), (1,)), ((), ())), preferred_element_type=F32)


def _rms_matmul_body(x_ref, g_ref, w_ref, o_ref, xn_ref):
    @pl.when(pl.program_id(1) == 0)
    def _():
        x = x_ref[...]
        ms = jnp.mean(x * x, axis=-1, keepdims=True)
        xn_ref[...] = (x * lax.rsqrt(ms + RMS_EPS) * g_ref[...]).astype(BF16)

    o_ref[...] = jnp.dot(xn_ref[...], w_ref[...], preferred_element_type=F32)


def _rms_matmul(x, g, w):
    m, k = x.shape
    n = w.shape[1]
    tm = _tile(m, 1024, SUBLANES)
    tn = _tile(n, 1024, LANES)
    vmem = (2 * tm * k * 4 + 2 * k * tn * 2 + 2 * tm * tn * 4 + tm * k * 2) / 2**20 + 8
    return pl.pallas_call(
        _rms_matmul_body,
        grid=(m // tm, n // tn),
        in_specs=[pl.BlockSpec((tm, k), lambda i, j: (i, 0)),
                  pl.BlockSpec((1, k), lambda i, j: (0, 0)),
                  pl.BlockSpec((k, tn), lambda i, j: (0, j))],
        out_specs=pl.BlockSpec((tm, tn), lambda i, j: (i, j)),
        out_shape=jax.ShapeDtypeStruct((m, n), F32),
        scratch_shapes=[pltpu.VMEM((tm, k), BF16)],
        compiler_params=_cparams(2, vmem),
        name="in_proj",
    )(x, g, w)


def _merge_body(*refs, n_first):
    first, second, ws, gs, o_ref = refs[0:4], refs[4:8], refs[8:12], refs[12:16], refs[16]

    def run(branches):
        acc = None
        for o_r, w_r, g_r in zip(branches, ws, gs):
            y = jax.nn.sigmoid(g_r[...]) * jnp.dot(o_r[...].astype(BF16), w_r[...],
                                                   preferred_element_type=F32)
            acc = y if acc is None else acc + y
        o_ref[...] = acc.astype(BF16)

    @pl.when(pl.program_id(0) < n_first)
    def _():
        run(first)

    @pl.when(pl.program_id(0) >= n_first)
    def _():
        run(second)


def _merge(outs_p, outs_s, wbs, proj, d_model):
    mp, w = outs_p[0].shape
    ms = outs_s[0].shape[0]
    tm = _tile(math.gcd(mp, ms), 512, 16)
    tn = _tile(d_model, 512, LANES)
    n_first = mp // tm
    p_specs = [pl.BlockSpec((tm, w), lambda i, j: (jnp.minimum(i, n_first - 1), 0)) for _ in range(4)]
    s_specs = [pl.BlockSpec((tm, w), lambda i, j: (jnp.maximum(i - n_first, 0), 0)) for _ in range(4)]
    w_specs = [pl.BlockSpec((w, tn), lambda i, j: (0, j)) for _ in range(4)]
    g_specs = [pl.BlockSpec((tm, tn), functools.partial(
        lambda i, j, base: (i, base + j), base=(OFF_GATE + b * d_model) // tn)) for b in range(4)]
    return pl.pallas_call(
        functools.partial(_merge_body, n_first=n_first),
        grid=((mp + ms) // tm, d_model // tn),
        in_specs=p_specs + s_specs + w_specs + g_specs,
        out_specs=pl.BlockSpec((tm, tn), lambda i, j: (i, j)),
        out_shape=jax.ShapeDtypeStruct((mp + ms, d_model), BF16),
        compiler_params=_cparams(2, 40),
        name="merge",
    )(*outs_p, *outs_s, *wbs, proj, proj, proj, proj)


def _matmul_res_body(a_ref, w_ref, r_ref, o_ref):
    o_ref[...] = r_ref[...] + jnp.dot(a_ref[...], w_ref[...], preferred_element_type=F32)


def _matmul_res(a, w, res):
    m, k = a.shape
    n = w.shape[1]
    tm = _tile(m, 512, 16)
    tn = _tile(n, 1024, LANES)
    return pl.pallas_call(
        _matmul_res_body,
        grid=(m // tm, n // tn),
        in_specs=[pl.BlockSpec((tm, k), lambda i, j: (i, 0)),
                  pl.BlockSpec((k, tn), lambda i, j: (0, j)),
                  pl.BlockSpec((tm, tn), lambda i, j: (i, j))],
        out_specs=pl.BlockSpec((tm, tn), lambda i, j: (i, j)),
        out_shape=jax.ShapeDtypeStruct((m, n), F32),
        compiler_params=_cparams(2, 40),
        name="out_proj",
    )(a, w, res)


def _mlp_body(x_ref, g_ref, wu_ref, wd_ref, fg_ref, o_ref, xn_ref, acc_ref, *, n_chunks, normed):
    c = pl.program_id(1)

    @pl.when(c == 0)
    def _():
        x = x_ref[...]
        ms = jnp.mean(x * x, axis=-1, keepdims=True)
        xn_ref[...] = (x * lax.rsqrt(ms + RMS_EPS) * g_ref[...]).astype(BF16)
        acc_ref[...] = jnp.zeros_like(acc_ref)

    h = jnp.dot(xn_ref[...], wu_ref[...], preferred_element_type=F32)
    h = jnp.square(jnp.maximum(h, 0.0))
    acc_ref[...] += jnp.dot(h.astype(BF16), wd_ref[...], preferred_element_type=F32)

    @pl.when(c == n_chunks - 1)
    def _():
        xo = x_ref[...] + acc_ref[...]
        if normed:
            ms = jnp.mean(xo * xo, axis=-1, keepdims=True)
            xo = xo * lax.rsqrt(ms + RMS_EPS) * fg_ref[...]
        o_ref[...] = xo


def _mlp(x, g, wu, wd, layer, final_g, normed):
    m, d = x.shape
    f = wu.shape[2]
    tm = _tile(m, 512, SUBLANES)
    tc = _tile(f, 1024, LANES)
    n_chunks = f // tc
    vmem = (2 * tm * d * 4 + 4 * d * tc * 2 + 2 * tm * d * 4 + tm * d * 2 + tm * d * 4) / 2**20 + 12
    return pl.pallas_call(
        functools.partial(_mlp_body, n_chunks=n_chunks, normed=normed),
        grid=(m // tm, n_chunks),
        in_specs=[pl.BlockSpec((tm, d), lambda i, c: (i, 0)),
                  pl.BlockSpec((1, d), lambda i, c: (0, 0)),
                  pl.BlockSpec((None, d, tc), lambda i, c: (layer, 0, c)),
                  pl.BlockSpec((None, tc, d), lambda i, c: (layer, c, 0)),
                  pl.BlockSpec((1, d), lambda i, c: (0, 0))],
        out_specs=pl.BlockSpec((tm, d), lambda i, c: (i, 0)),
        out_shape=jax.ShapeDtypeStruct((m, d), F32),
        scratch_shapes=[pltpu.VMEM((tm, d), BF16), pltpu.VMEM((tm, d), F32)],
        compiler_params=_cparams(2, vmem),
        name="mlp",
    )(x, g, wu, wd, final_g)


def _cumsum_rows(x, tc):
    if tc >= LANES:
        r = lax.broadcasted_iota(jnp.int32, (tc, tc), 0)
        c = lax.broadcasted_iota(jnp.int32, (tc, tc), 1)
        tri = jnp.where(c <= r, 1.0, 0.0).astype(BF16)
        return _dot_sel_lhs(tri, x)
    rows = lax.broadcasted_iota(jnp.int32, x.shape, 0)
    parts = [jnp.sum(jnp.where(rows <= t, x, 0.0), axis=0, keepdims=True) for t in range(tc)]
    return jnp.concatenate(parts, axis=0)


def _logf_cumsum_body(ff_ref, fb_ref, c0_ref, lf_ref, c_ref, carry_ref, *, tc):
    @pl.when(pl.program_id(1) == 0)
    def _():
        carry_ref[...] = c0_ref[...]

    lf = -_softplus(-(ff_ref[...] + fb_ref[...]))
    lf_ref[...] = lf
    cs = _cumsum_rows(lf, tc) + carry_ref[...]
    c_ref[...] = cs
    carry_ref[...] = cs[tc - 1:tc, :]


def _logf_cumsum(ff, fb, c0):
    nseq, t, h = ff.shape
    tc = _tile(t, 512, SUBLANES)
    blk = pl.BlockSpec((None, tc, h), lambda s, c: (s, c, 0))
    return pl.pallas_call(
        functools.partial(_logf_cumsum_body, tc=tc),
        grid=(nseq, t // tc),
        in_specs=[blk, pl.BlockSpec((1, h), lambda s, c: (0, 0)),
                  pl.BlockSpec((None, 1, h), lambda s, c: (s, 0, 0))],
        out_specs=[blk, blk],
        out_shape=[jax.ShapeDtypeStruct((nseq, t, h), F32)] * 2,
        scratch_shapes=[pltpu.VMEM((1, h), F32)],
        compiler_params=_cparams(2, 24),
        name="logf_cumsum",
    )(ff, fb, c0)


def _t5_bucket(n):
    max_exact = NUM_BUCKETS // 2
    nf = jnp.maximum(n, 1).astype(F32)
    large = max_exact + (jnp.log(nf / max_exact) / math.log(MAX_DISTANCE / max_exact)
                         * (NUM_BUCKETS - max_exact)).astype(jnp.int32)
    return jnp.where(n < max_exact, n, jnp.minimum(large, NUM_BUCKETS - 1))


def _t5_lookup(bucket, tab_ref, col):
    acc = jnp.zeros(bucket.shape, F32)
    for j in range(NUM_BUCKETS):
        acc = jnp.where(bucket == j, tab_ref[j, col], acc)
    return acc


def _softmax_update(s, v, m_ref, l_ref, acc_ref, idx):
    m_prev = m_ref[idx]
    m_new = jnp.maximum(m_prev, jnp.max(s, axis=-1, keepdims=True))
    alpha = jnp.exp(m_prev - m_new)
    p = jnp.exp(s - pltpu.repeat(m_new, s.shape[-1] // LANES, axis=1))
    l_ref[idx] = alpha * l_ref[idx] + jnp.sum(p, axis=-1, keepdims=True)
    acc_ref[idx] = (alpha[:, :v.shape[-1]] * acc_ref[idx]
                    + jnp.dot(p.astype(BF16), v, preferred_element_type=F32))
    m_ref[idx] = m_new


def _fox_flash_body(q_ref, k_ref, v_ref, c_ref, ct_ref, o_ref, m_ref, l_ref, acc_ref, *, tq):
    qi = pl.program_id(1)
    ki = pl.program_id(2)

    @pl.when(ki == 0)
    def _():
        m_ref[...] = jnp.full(m_ref.shape, NEG_INF, F32)
        l_ref[...] = jnp.zeros_like(l_ref)
        acc_ref[...] = jnp.zeros_like(acc_ref)

    first = lax.broadcasted_iota(jnp.int32, (tq, LANES), 1) < HEAD_DIM

    def tiles(masked):
        if masked:
            row = lax.broadcasted_iota(jnp.int32, (tq, tq), 0)
            col = lax.broadcasted_iota(jnp.int32, (tq, tq), 1)
            keep = col <= row
        for g in range(H_FOX // 2):
            grp = slice(g * LANES, (g + 1) * LANES)
            q2 = q_ref[:, grp] * QK_SCALE
            k2 = k_ref[:, grp].astype(BF16)
            v2 = v_ref[:, grp].astype(BF16)
            for j in range(2):
                h = 2 * g + j
                own = first if j == 0 else jnp.logical_not(first)
                s = _dot_nt(jnp.where(own, q2, 0.0).astype(BF16), k2)
                s = s + c_ref[:, h:h + 1] - ct_ref[h:h + 1, :]
                if masked:
                    s = jnp.where(keep, s, NEG_INF)
                m_prev = m_ref[h]
                m_new = jnp.maximum(m_prev, jnp.max(s, axis=-1, keepdims=True))
                alpha = jnp.exp(m_prev - m_new)
                p = jnp.exp(s - pltpu.repeat(m_new, tq // LANES, axis=1))
                l_ref[h] = alpha * l_ref[h] + jnp.sum(p, axis=-1, keepdims=True)
                pv = jnp.dot(p.astype(BF16), v2, preferred_element_type=F32)
                acc_ref[h] = alpha * acc_ref[h] + pv
                m_ref[h] = m_new

    @pl.when(ki < qi)
    def _():
        tiles(False)

    @pl.when(ki == qi)
    def _():
        tiles(True)
        for g in range(H_FOX // 2):
            o_ref[:, g * LANES:(g + 1) * LANES] = jnp.where(
                first, acc_ref[2 * g] / l_ref[2 * g], acc_ref[2 * g + 1] / l_ref[2 * g + 1])


def _qkv_specs(t, tq, col0, w):
    nq = t // tq
    c = col0 // w
    return [pl.BlockSpec((tq, w), lambda bi, qi, ki: (bi * nq + qi, c)),
            pl.BlockSpec((tq, w), lambda bi, qi, ki: (bi * nq + jnp.minimum(ki, qi), c + 1)),
            pl.BlockSpec((tq, w), lambda bi, qi, ki: (bi * nq + jnp.minimum(ki, qi), c + 2))]


def _fox_flash(proj, c, ct, b, t):
    w = W_BRANCH
    tq = _tile(t, 512, LANES)
    nq = t // tq
    return pl.pallas_call(
        functools.partial(_fox_flash_body, tq=tq),
        grid=(b, nq, nq),
        in_specs=_qkv_specs(t, tq, OFF_FQ, w) + [
            pl.BlockSpec((None, tq, H_FOX), lambda bi, qi, ki: (bi, qi, 0)),
            pl.BlockSpec((None, H_FOX, tq), lambda bi, qi, ki: (bi, 0, jnp.minimum(ki, qi)))],
        out_specs=pl.BlockSpec((tq, w), lambda bi, qi, ki: (bi * nq + qi, 0)),
        out_shape=jax.ShapeDtypeStruct((b * t, w), F32),
        scratch_shapes=[pltpu.VMEM((H_FOX, tq, LANES), F32), pltpu.VMEM((H_FOX, tq, LANES), F32),
                        pltpu.VMEM((H_FOX, tq, LANES), F32)],
        compiler_params=_cparams(3, 48),
        name="fox_flash",
    )(proj, proj, proj, c, ct)


def _lambda(lq1_ref, lk1_ref, lq2_ref, lk2_ref, lam_init):
    s1 = jnp.sum(lq1_ref[...] * lk1_ref[...], axis=-1, keepdims=True)
    s2 = jnp.sum(lq2_ref[...] * lk2_ref[...], axis=-1, keepdims=True)
    return jnp.exp(s1) - jnp.exp(s2) + lam_init


def _subln(o, g, lam_init):
    ms = jnp.mean(o * o, axis=-1, keepdims=True)
    return o * lax.rsqrt(ms + SUBLN_EPS) * g * (1.0 - lam_init)


def _diff_flash_body(tab_ref, q_ref, k_ref, v_ref, lq1_ref, lk1_ref, lq2_ref, lk2_ref, g_ref,
                     o_ref, bias_ref, m_ref, l_ref, acc_ref, *, tq, lam_init):
    bi = pl.program_id(0)
    qi = pl.program_id(1)
    ki = pl.program_id(2)

    @pl.when((bi == 0) & (qi == 0) & (ki == 0))
    def _():
        def fill(rb, carry):
            r0 = pl.multiple_of(rb * SUBLANES, SUBLANES)
            rows = r0 + lax.broadcasted_iota(jnp.int32, (SUBLANES, tq), 0)
            cols = lax.broadcasted_iota(jnp.int32, (SUBLANES, tq), 1)
            for which in range(2):
                bucket = _t5_bucket(jnp.maximum(rows - cols + which * tq, 0))
                for c in range(2 * H_DIFF):
                    bias_ref[c, which, pl.ds(r0, SUBLANES), :] = _t5_lookup(bucket, tab_ref, c)
            return carry
        lax.fori_loop(0, tq // SUBLANES, fill, 0)

    @pl.when(ki == 0)
    def _():
        m_ref[...] = jnp.full(m_ref.shape, NEG_INF, F32)
        l_ref[...] = jnp.zeros_like(l_ref)
        acc_ref[...] = jnp.zeros_like(acc_ref)

    def tiles(mode):
        lane = lax.broadcasted_iota(jnp.int32, (tq, 2 * HEAD_DIM), 1)
        if mode == 0:
            row = lax.broadcasted_iota(jnp.int32, (tq, tq), 0)
            col = lax.broadcasted_iota(jnp.int32, (tq, tq), 1)
            keep = col <= row
        for h in range(H_DIFF):
            grp = slice(h * 2 * HEAD_DIM, (h + 1) * 2 * HEAD_DIM)
            q = q_ref[:, grp] * QK_SCALE
            k = k_ref[:, grp].astype(BF16)
            v = v_ref[:, grp].astype(BF16)
            for mp in range(2):
                c = mp * H_DIFF + h
                qm = jnp.where((lane >= mp * HEAD_DIM) & (lane < (mp + 1) * HEAD_DIM), q, 0.0)
                s = _dot_nt(qm.astype(BF16), k)
                if mode == 0:
                    s = jnp.where(keep, s + bias_ref[c, 0], NEG_INF)
                elif mode == 1:
                    s = s + bias_ref[c, 1]
                else:
                    s = s + tab_ref[NUM_BUCKETS - 1, c]
                _softmax_update(s, v, m_ref, l_ref, acc_ref, c)

    @pl.when(ki < qi - 1)
    def _():
        tiles(2)

    @pl.when(ki == qi - 1)
    def _():
        tiles(1)

    @pl.when(ki == qi)
    def _():
        tiles(0)
        lam = _lambda(lq1_ref, lk1_ref, lq2_ref, lk2_ref, lam_init)
        for h in range(H_DIFF):
            o = acc_ref[h] / l_ref[h] - lam * (acc_ref[H_DIFF + h] / l_ref[H_DIFF + h])
            o_ref[:, h * 2 * HEAD_DIM:(h + 1) * 2 * HEAD_DIM] = _subln(o, g_ref[...], lam_init)


def _diff_flash(table, proj, lq1, lk1, lq2, lk2, g, lam_init, b, t):
    w = W_BRANCH
    d = 2 * HEAD_DIM
    h = H_DIFF
    tq = _tile(t, 512, LANES)
    assert tq >= MAX_DISTANCE or tq == t, "far blocks must lie in the last bucket"
    nq = t // tq
    vec = pl.BlockSpec((1, HEAD_DIM), lambda bi, qi, ki: (0, 0))
    return pl.pallas_call(
        functools.partial(_diff_flash_body, tq=tq, lam_init=lam_init),
        grid=(b, nq, nq),
        in_specs=[pl.BlockSpec(memory_space=pltpu.SMEM)] + _qkv_specs(t, tq, OFF_DQ, w) + [
            vec, vec, vec, vec, pl.BlockSpec((1, d), lambda bi, qi, ki: (0, 0))],
        out_specs=pl.BlockSpec((tq, w), lambda bi, qi, ki: (bi * nq + qi, 0)),
        out_shape=jax.ShapeDtypeStruct((b * t, w), F32),
        scratch_shapes=[pltpu.VMEM((2 * h, 2, tq, tq), F32),
                        pltpu.VMEM((2 * h, tq, LANES), F32), pltpu.VMEM((2 * h, tq, LANES), F32),
                        pltpu.VMEM((2 * h, tq, d), F32)],
        compiler_params=_cparams(3, 54),
        name="diff_flash",
    )(table, proj, proj, proj, lq1, lk1, lq2, lk2, g)


def _softmax_tiles(tiles):
    m = tiles[0][0].max(axis=-1, keepdims=True)
    for s, _ in tiles[1:]:
        m = jnp.maximum(m, s.max(axis=-1, keepdims=True))
    l, acc = None, None
    for s, pv in tiles:
        p = jnp.exp(s - m)
        ps, pa = jnp.sum(p, axis=-1, keepdims=True), pv(p.astype(BF16))
        l, acc = (ps, pa) if l is None else (l + ps, acc + pa)
    return acc / l


def _page_specs(layer, n_pages, rows, cols):
    return [pl.BlockSpec((None, None, rows, cols),
                         functools.partial(lambda b, pt, p: (layer, pt[b, p], 0, 0), p=p))
            for p in range(n_pages)]


def _fox_dec_body(pt_ref, qb_ref, *refs, n_pages, ts):
    kt = refs[:n_pages]
    vt = refs[n_pages:2 * n_pages]
    lf_all_ref, knt_ref, vnt_ref, cnb_ref, cnt_ref, o_ref = refs[2 * n_pages:]
    b = pl.program_id(0)
    lf = [lf_all_ref[pt_ref[b, p]] for p in range(n_pages)]
    rows = H_FOX * ts
    page = knt_ref.shape[-1]
    qb = qb_ref[...]
    qk = lambda k_ref: jnp.dot(qb, k_ref[...].astype(BF16), preferred_element_type=F32)
    pv = lambda v_ref: (lambda p: _dot_nt(p, v_ref[...].astype(BF16)))

    lane = lax.broadcasted_iota(jnp.int32, (ts, H_FOX, page), 2).reshape(rows, page)
    tok = lax.broadcasted_iota(jnp.int32, (ts, H_FOX, page), 0).reshape(rows, page)
    cn_keys = jnp.concatenate([cnt_ref[...]] * ts, axis=0)
    bias_new = jnp.where((lane <= tok) & (lane < ts), cnb_ref[...] - cn_keys, NEG_INF)
    tiles = [(qk(knt_ref) + bias_new, pv(vnt_ref))]

    j_idx = lax.broadcasted_iota(jnp.int32, (page, page), 0)
    k_idx = lax.broadcasted_iota(jnp.int32, (page, page), 1)
    later = jnp.where(j_idx > k_idx, 1.0, 0.0).astype(BF16)
    in_page = _dot_sel_rhs(jnp.concatenate(lf, axis=0), later)
    carry = jnp.zeros((H_FOX, 1), F32)
    for p in reversed(range(n_pages)):
        suffix = in_page[p * H_FOX:(p + 1) * H_FOX] + carry
        carry = suffix[:, 0:1] + lf[p][:, 0:1]
        bias = jnp.concatenate([suffix] * ts, axis=0) + cnb_ref[...]
        tiles.append((qk(kt[p]) + bias, pv(vt[p])))

    accn = _softmax_tiles(tiles).reshape(ts, H_FOX, W_BRANCH)
    col_head = lax.shift_right_logical(lax.broadcasted_iota(jnp.int32, (H_FOX, W_BRANCH), 1),
                                       HEAD_DIM.bit_length() - 1)
    own = col_head == lax.broadcasted_iota(jnp.int32, (H_FOX, W_BRANCH), 0)
    o_ref[...] = jnp.sum(jnp.where(own[None], accn, 0.0), axis=1)


def _fox_decode(page_table, qb, kt, vt, lft, knt, vnt, cnb, cnt, layer, ts):
    db, rows, w = qb.shape
    page = kt.shape[-1]
    n_pages = page_table.shape[1]
    per_seq = lambda r, c: pl.BlockSpec((None, r, c), lambda b, pt: (b, 0, 0))
    grid_spec = pltpu.PrefetchScalarGridSpec(
        num_scalar_prefetch=1,
        grid=(db,),
        in_specs=([per_seq(rows, w)] + _page_specs(layer, n_pages, w, page) * 2
                  + [pl.BlockSpec((None,) + lft.shape[1:], lambda b, pt: (layer, 0, 0, 0),
                                  pipeline_mode=pl.Buffered(1))]
                  + [per_seq(w, page), per_seq(w, page), per_seq(rows, page), per_seq(H_FOX, page)]),
        out_specs=per_seq(ts, w))
    return pl.pallas_call(
        functools.partial(_fox_dec_body, n_pages=n_pages, ts=ts),
        grid_spec=grid_spec,
        out_shape=jax.ShapeDtypeStruct((db, ts, w), F32),
        compiler_params=_cparams(1, 40),
        name="fox_decode",
    )(page_table, qb, *([kt] * n_pages), *([vt] * n_pages), lft, knt, vnt, cnb, cnt)


def _diff_dec_body(pt_ref, tab_ref, q_ref, *refs, n_pages, ts, page, lam_init):
    kp = refs[:n_pages]
    vp = refs[n_pages:2 * n_pages]
    (kn_ref, vn_ref, lq1_ref, lk1_ref, lq2_ref, lk2_ref, g_ref, o_ref, bias_ref) = refs[2 * n_pages:]
    rows = 2 * H_DIFF * ts
    cols = page * H_DIFF
    past = n_pages * page

    @pl.when(pl.program_id(0) == 0)
    def _():
        lane = lax.broadcasted_iota(jnp.int32, (ts, cols), 1)
        pos = lax.shift_right_logical(lane, H_DIFF.bit_length() - 1)
        head = lane & (H_DIFF - 1)
        qpos = past + lax.broadcasted_iota(jnp.int32, (ts, cols), 0)

        def fill(tile, kpos, valid):
            bucket = _t5_bucket(jnp.maximum(qpos - kpos, 0))
            for rb in range(2 * H_DIFF):
                h, mp = rb // 2, rb % 2
                val = _t5_lookup(bucket, tab_ref, mp * H_DIFF + h)
                bias_ref[tile, rb * ts:(rb + 1) * ts, :] = jnp.where(valid & (head == h), val, NEG_INF)

        def past_tile(tile, carry):
            kpos = tile * page + pos
            fill(tile, kpos, kpos <= qpos)
            return carry

        lax.fori_loop(0, n_pages, past_tile, 0)
        kpos = past + pos
        fill(n_pages, kpos, (kpos <= qpos) & (pos < ts))

    q = q_ref[...]
    tiles = []
    for p in range(n_pages + 1):
        k_ref, v_ref = (kp[p], vp[p]) if p < n_pages else (kn_ref, vn_ref)
        s = _dot_nt(q, k_ref[...].astype(BF16)) + bias_ref[p]
        tiles.append((s, lambda pr, v_ref=v_ref: jnp.dot(pr, v_ref[...].astype(BF16),
                                                         preferred_element_type=F32)))

    lam = _lambda(lq1_ref, lk1_ref, lq2_ref, lk2_ref, lam_init)
    on = _softmax_tiles(tiles)
    for h in range(H_DIFF):
        o1 = on[2 * h * ts:(2 * h + 1) * ts, :]
        o2 = on[(2 * h + 1) * ts:(2 * h + 2) * ts, :]
        o_ref[:, h * 2 * HEAD_DIM:(h + 1) * 2 * HEAD_DIM] = _subln(o1 - lam * o2, g_ref[...], lam_init)


def _diff_decode(page_table, table, q, kp, vp, kn, vn, lq1, lk1, lq2, lk2, g, layer, ts, lam_init):
    db, rows, d = q.shape
    cols = kp.shape[2]
    page = cols // H_DIFF
    n_pages = page_table.shape[1]
    per_seq = lambda r, c: pl.BlockSpec((None, r, c), lambda b, pt: (b, 0, 0))
    vec = pl.BlockSpec((1, HEAD_DIM), lambda b, pt: (0, 0))
    grid_spec = pltpu.PrefetchScalarGridSpec(
        num_scalar_prefetch=1,
        grid=(db,),
        in_specs=([pl.BlockSpec(memory_space=pltpu.SMEM), per_seq(rows, d)]
                  + _page_specs(layer, n_pages, cols, d) * 2
                  + [per_seq(cols, d), per_seq(cols, d), vec, vec, vec, vec,
                     pl.BlockSpec((1, d), lambda b, pt: (0, 0))]),
        out_specs=per_seq(ts, H_DIFF * d),
        scratch_shapes=[pltpu.VMEM((n_pages + 1, rows, cols), F32)])
    return pl.pallas_call(
        functools.partial(_diff_dec_body, n_pages=n_pages, ts=ts, page=page, lam_init=lam_init),
        grid_spec=grid_spec,
        out_shape=jax.ShapeDtypeStruct((db, ts, H_DIFF * d), F32),
        compiler_params=_cparams(1, 40),
        name="diff_decode",
    )(page_table, table, q, *([kp] * n_pages), *([vp] * n_pages), kn, vn, lq1, lk1, lq2, lk2, g)


def _lru_body(lx_ref, lg_ref, c0_ref, h0_ref, cw_ref, cb_ref, wa_ref, ba_ref, wx_ref, bx_ref,
              lam_ref, o_ref, hn_ref, cin_ref, a_ref, u_ref, hs_ref, hc_ref, *, tc):
    @pl.when(pl.program_id(1) == 0)
    def _():
        cin_ref[0:SUBLANES, :] = jnp.zeros((SUBLANES, W_BRANCH), F32)
        cin_ref[SUBLANES - (CONV_W - 1):SUBLANES, :] = c0_ref[...]
        hc_ref[...] = h0_ref[...]

    lx = lx_ref[...]
    cin_ref[SUBLANES:SUBLANES + tc, :] = lx
    base = SUBLANES - (CONV_W - 1)
    acc = cw_ref[0:1, :] * cin_ref[base:base + tc, :]
    for j in range(1, CONV_W):
        acc = acc + cw_ref[j:j + 1, :] * cin_ref[base + j:base + j + tc, :]
    xc = cb_ref[...] + acc
    xb = xc.astype(BF16)
    r_gate = jax.nn.sigmoid(jnp.dot(xb, wa_ref[...], preferred_element_type=F32) + ba_ref[...])
    i_gate = jax.nn.sigmoid(jnp.dot(xb, wx_ref[...], preferred_element_type=F32) + bx_ref[...])
    log_a = -LRU_C * r_gate * _softplus(-lam_ref[...])
    a_ref[...] = jnp.exp(log_a)
    u_ref[...] = jnp.sqrt(1.0 - jnp.exp(2.0 * log_a)) * (i_gate * xc)

    def step(t, h):
        h = a_ref[pl.ds(t, 1), :] * h + u_ref[pl.ds(t, 1), :]
        hs_ref[pl.ds(t, 1), :] = h
        return h

    h = lax.fori_loop(0, tc, step, hc_ref[...], unroll=8)
    hc_ref[...] = h
    hn_ref[...] = h
    o_ref[...] = hs_ref[...] * jax.nn.gelu(lg_ref[...])
    cin_ref[0:SUBLANES, :] = lx[tc - SUBLANES:tc, :]


def _lru(proj, row0, nseq, t, conv0, h0, layer, cw, cb, wa, ba, wx, bx, lam):
    m_total = proj.shape[0]
    tc = _tile(t, 512, SUBLANES)
    nc = t // tc
    blk0 = row0 // tc
    col = lambda off: (lambda s, c: (blk0 + s * nc + c, off // W_BRANCH))
    vec = pl.BlockSpec((1, W_BRANCH), lambda s, c: (0, 0))
    mat = pl.BlockSpec((W_BRANCH, W_BRANCH), lambda s, c: (0, 0))
    return pl.pallas_call(
        functools.partial(_lru_body, tc=tc),
        grid=(nseq, nc),
        in_specs=[pl.BlockSpec((tc, W_BRANCH), col(OFF_LX)),
                  pl.BlockSpec((tc, W_BRANCH), col(OFF_LG)),
                  pl.BlockSpec((None, None, CONV_W - 1, W_BRANCH), lambda s, c: (layer, s, 0, 0)),
                  pl.BlockSpec((None, None, 1, W_BRANCH), lambda s, c: (layer, s, 0, 0)),
                  pl.BlockSpec((CONV_W, W_BRANCH), lambda s, c: (0, 0)),
                  vec, mat, vec, mat, vec, vec],
        out_specs=[pl.BlockSpec((tc, W_BRANCH), lambda s, c: (s * nc + c, 0)),
                   pl.BlockSpec((None, 1, W_BRANCH), lambda s, c: (s, 0, 0))],
        out_shape=[jax.ShapeDtypeStruct((nseq * t, W_BRANCH), F32),
                   jax.ShapeDtypeStruct((nseq, 1, W_BRANCH), F32)],
        scratch_shapes=[pltpu.VMEM((tc + SUBLANES, W_BRANCH), F32),
                        pltpu.VMEM((tc, W_BRANCH), F32), pltpu.VMEM((tc, W_BRANCH), F32),
                        pltpu.VMEM((tc, W_BRANCH), F32), pltpu.VMEM((1, W_BRANCH), F32)],
        compiler_params=_cparams(2, 32),
        name="lru",
    )(proj, proj, conv0, h0, cw, cb, wa, ba, wx, bx, lam)


def _rwkv_prep_body(rr_ref, rk_ref, rv_ref, lo_ref, sh_ref, mu_ref, w0_ref, w2_ref, a0_ref, a2_ref,
                    g2_ref, kk_ref, ka_ref, rkk_ref, seg_ref,
                    r_o, w_o, k_o, v_o, a_o, b_o, g_o, bon_o, xs_ref, *, tc):
    wb = W_BRANCH

    @pl.when(pl.program_id(1) == 0)
    def _():
        xs_ref[0:SUBLANES, :] = jnp.zeros((SUBLANES, W_RWKV_PROJ), F32)
        xs_ref[SUBLANES - 1:SUBLANES, :] = sh_ref[...]

    xs_ref[SUBLANES:SUBLANES + tc, 0:wb] = rr_ref[...]
    xs_ref[SUBLANES:SUBLANES + tc, wb:2 * wb] = rk_ref[...]
    xs_ref[SUBLANES:SUBLANES + tc, 2 * wb:3 * wb] = rv_ref[...]
    xs_ref[SUBLANES:SUBLANES + tc, 3 * wb:W_RWKV_PROJ] = lo_ref[...]
    cur = xs_ref[SUBLANES:SUBLANES + tc, :]
    prev = xs_ref[SUBLANES - 1:SUBLANES - 1 + tc, :]
    rx = cur + (prev - cur) * mu_ref[...]
    xs_ref[0:SUBLANES, :] = cur[tc - SUBLANES:tc, :]

    rr = rx[:, 0:wb]
    rk = rx[:, wb:2 * wb]
    rv = rx[:, 2 * wb:3 * wb]
    wl_al = rx[:, 3 * wb:3 * wb + LANES]
    gl = rx[:, 3 * wb + LANES:W_RWKV_PROJ]
    dot = functools.partial(jnp.dot, preferred_element_type=F32)
    wd = -_softplus(-(w0_ref[...] + dot(jnp.tanh(wl_al).astype(BF16), w2_ref[...]))) - 0.5
    decay = jnp.exp(-jnp.exp(wd))
    aa = jax.nn.sigmoid(a0_ref[...] + dot(wl_al.astype(BF16), a2_ref[...]))
    gg = dot(jax.nn.sigmoid(gl).astype(BF16), g2_ref[...])
    kk = rk * kk_ref[...]
    norm = jnp.sqrt(_dot_sel_rhs(kk * kk, seg_ref[...]))
    kk = kk / jnp.maximum(norm, 1e-12)
    kh = rk * (1.0 + (aa - 1.0) * ka_ref[...])
    r_o[...] = rr
    w_o[...] = decay
    k_o[...] = kh
    v_o[...] = rv
    a_o[...] = -kk
    b_o[...] = kk * aa
    g_o[...] = gg
    bon_o[...] = _dot_sel_rhs(rr * kh * rkk_ref[...], seg_ref[...]) * rv


def _rwkv_prep(proj, row0, nseq, t, shift0, layer, mu, w0, w2p, a0, a2p, g2, k_k, k_a, r_k, seg):
    tc = _tile(t, 256, SUBLANES)
    nc = t // tc
    blk0 = row0 // tc
    wb = W_BRANCH
    col = lambda off, w: (lambda s, c: (blk0 + s * nc + c, off // w))
    vec = pl.BlockSpec((1, wb), lambda s, c: (0, 0))
    out_blk = pl.BlockSpec((tc, wb), lambda s, c: (s * nc + c, 0))
    lo_w = W_RWKV_PROJ - 3 * wb
    return pl.pallas_call(
        functools.partial(_rwkv_prep_body, tc=tc),
        grid=(nseq, nc),
        in_specs=[pl.BlockSpec((tc, wb), col(OFF_RW, wb)),
                  pl.BlockSpec((tc, wb), col(OFF_RW + wb, wb)),
                  pl.BlockSpec((tc, wb), col(OFF_RW + 2 * wb, wb)),
                  pl.BlockSpec((tc, lo_w), col(OFF_RW + 3 * wb, lo_w)),
                  pl.BlockSpec((None, None, 1, W_RWKV_PROJ), lambda s, c: (layer, s, 0, 0)),
                  pl.BlockSpec((1, W_RWKV_PROJ), lambda s, c: (0, 0)),
                  vec, pl.BlockSpec((LANES, wb), lambda s, c: (0, 0)),
                  vec, pl.BlockSpec((LANES, wb), lambda s, c: (0, 0)),
                  pl.BlockSpec((LANES, wb), lambda s, c: (0, 0)),
                  vec, vec, vec, pl.BlockSpec((wb, wb), lambda s, c: (0, 0))],
        out_specs=[out_blk] * 8,
        out_shape=[jax.ShapeDtypeStruct((nseq * t, wb), F32)] * 8,
        scratch_shapes=[pltpu.VMEM((tc + SUBLANES, W_RWKV_PROJ), F32)],
        compiler_params=_cparams(2, 40),
        name="rwkv_prep",
    )(proj, proj, proj, proj, shift0, mu, w0, w2p, a0, a2p, g2, k_k, k_a, r_k, seg)


def _rwkv_scan_body(r_ref, w_ref, k_ref, v_ref, a_ref, b_ref, s0_ref, y_ref, sn_ref, st_ref,
                    *, nb, tc, n_chunks):
    c = pl.program_id(1)

    @pl.when(c == 0)
    def _():
        st_ref[...] = s0_ref[...]

    n_pairs = H_RWKV // 2
    shape = (HEAD_DIM, LANES)
    lane = lax.broadcasted_iota(jnp.int32, shape, 1)
    first = lane < HEAD_DIM
    eye2 = jnp.where((lane & (HEAD_DIM - 1)) == lax.broadcasted_iota(jnp.int32, shape, 0), 1.0, 0.0)
    half = lambda d: lax.shift_right_logical(lax.broadcasted_iota(jnp.int32, (LANES, LANES), d),
                                             HEAD_DIM.bit_length() - 1)
    ones_bd = jnp.where(half(0) == half(1), 1.0, 0.0).astype(BF16)

    def step(t, states):
        new_states = []
        for s in range(nb):
            rows = [ref[s, pl.ds(t, 1), :] for ref in (r_ref, w_ref, k_ref, v_ref, a_ref, b_ref)]
            y_parts = []
            for p in range(n_pairs):
                r_t, w_t, k_t, v_t, a_t, b_t = (z[:, p * LANES:(p + 1) * LANES] for z in rows)
                st = states[s * n_pairs + p]
                sa = jnp.dot((st * a_t).astype(BF16), ones_bd, preferred_element_type=F32)
                d = eye2 * v_t
                v1 = jnp.sum(jnp.where(first, d, 0.0), axis=1, keepdims=True)
                v2 = jnp.sum(jnp.where(first, 0.0, d), axis=1, keepdims=True)
                st = st * w_t + sa * b_t + jnp.where(first, v1, v2) * k_t
                y = jnp.dot((st * r_t).astype(BF16), ones_bd, preferred_element_type=F32)
                y_parts.append(jnp.sum(eye2 * y, axis=0, keepdims=True))
                new_states.append(st)
            y_ref[s, pl.ds(t, 1), :] = jnp.concatenate(y_parts, axis=1)
        return tuple(new_states)

    init = tuple(st_ref[s, p] for s in range(nb) for p in range(n_pairs))
    final = lax.fori_loop(0, tc, step, init, unroll=16)
    for i, st in enumerate(final):
        st_ref[i // n_pairs, i % n_pairs] = st

    @pl.when(c == n_chunks - 1)
    def _():
        sn_ref[...] = st_ref[...]


def _pack_pairs(s):
    n = s.shape[0]
    s = s.reshape(n, H_RWKV // 2, 2, HEAD_DIM, HEAD_DIM).transpose(0, 1, 3, 2, 4)
    return s.reshape(n, H_RWKV // 2, HEAD_DIM, 2 * HEAD_DIM)


def _unpack_pairs(s):
    n = s.shape[0]
    s = s.reshape(n, H_RWKV // 2, HEAD_DIM, 2, HEAD_DIM).transpose(0, 1, 3, 2, 4)
    return s.reshape(n, H_RWKV, HEAD_DIM, HEAD_DIM)


def _rwkv_scan(r, w, k, v, a, b, s0):
    nseq, t, wd = r.shape
    nb = 2 if nseq % 2 == 0 else 1
    tc = _tile(t, LANES, SUBLANES)
    n_chunks = t // tc
    n_pairs = H_RWKV // 2
    blk = pl.BlockSpec((nb, tc, wd), lambda s, c: (s, c, 0))
    st_blk = pl.BlockSpec((nb, n_pairs, HEAD_DIM, LANES), lambda s, c: (s, 0, 0, 0))
    y, sn = pl.pallas_call(
        functools.partial(_rwkv_scan_body, nb=nb, tc=tc, n_chunks=n_chunks),
        grid=(nseq // nb, n_chunks),
        in_specs=[blk] * 6 + [st_blk],
        out_specs=[blk, st_blk],
        out_shape=[jax.ShapeDtypeStruct((nseq, t, wd), F32),
                   jax.ShapeDtypeStruct((nseq, n_pairs, HEAD_DIM, LANES), F32)],
        scratch_shapes=[pltpu.VMEM((nb, n_pairs, HEAD_DIM, LANES), F32)],
        compiler_params=_cparams(2, 40),
        name="rwkv_scan",
    )(r, w, k, v, a, b, _pack_pairs(s0))
    return y, _unpack_pairs(sn)


def _rwkv_post_body(y_ref, g_ref, bon_ref, lg_ref, lb_ref, seg_ref, o_ref):
    y = y_ref[...]
    inv_n = 1.0 / HEAD_DIM
    mu = _dot_sel_rhs(y, seg_ref[...]) * inv_n
    d = y - mu
    var = _dot_sel_rhs(d * d, seg_ref[...]) * inv_n
    yn = d * lax.rsqrt(var + GN_EPS) * lg_ref[...] + lb_ref[...]
    o_ref[...] = (yn + bon_ref[...]) * g_ref[...]


def _rwkv_post(y, gg, bonus, lnx_g, lnx_b, seg):
    m, wb = y.shape
    tm = _tile(m, 512, SUBLANES)
    blk = pl.BlockSpec((tm, wb), lambda i: (i, 0))
    vec = pl.BlockSpec((1, wb), lambda i: (0, 0))
    return pl.pallas_call(
        _rwkv_post_body,
        grid=(m // tm,),
        in_specs=[blk, blk, blk, vec, vec, pl.BlockSpec((wb, wb), lambda i: (0, 0))],
        out_specs=blk,
        out_shape=jax.ShapeDtypeStruct((m, wb), F32),
        compiler_params=_cparams(1, 24),
        name="rwkv_post",
    )(y, gg, bonus, lnx_g, lnx_b, seg)


def _block_diag(blocks):
    n, bi, bj = blocks.shape
    eye = jnp.eye(n, dtype=blocks.dtype)
    return jnp.einsum('nij,nm->nimj', blocks, eye).reshape(n * bi, n * bj)


def kernel(x_prompt, x_sample, cache_fox_k, cache_fox_v, cache_fox_logf, cache_diff_k, cache_diff_v, state_lru_conv, state_lru_h, state_rwkv_shift, state_rwkv_wkv, page_table, norm1_g, w_in, fox_fb, lam_q1, lam_k1, lam_q2, lam_k2, diff_subln_g, t5_table, lru_conv_w, lru_conv_b, lru_wa, lru_ba, lru_wx, lru_bx, lru_lambda, rwkv_mu, rwkv_w0, rwkv_w2, rwkv_a0, rwkv_a2, rwkv_g2, rwkv_k_k, rwkv_k_a, rwkv_r_k, rwkv_lnx_g, rwkv_lnx_b, wb_fox, wb_diff, wb_lru, wb_rwkv, w_out, norm2_g, w_up, w_down, final_g):
    B, T, D = x_prompt.shape
    DB, TS, _ = x_sample.shape
    L = w_in.shape[0]
    n_pool, page = cache_fox_k.shape[1], cache_fox_k.shape[2]
    n_pages = page_table.shape[1]
    mp, ms = B * T, DB * TS
    wb = W_BRANCH
    assert TS % SUBLANES == 0 and TS <= page and T % SUBLANES == 0

    x = jnp.concatenate([x_prompt.reshape(mp, D), x_sample.reshape(ms, D)], axis=0)
    kt_fox = cache_fox_k.transpose(0, 1, 3, 4, 2).reshape(L, n_pool, wb, page)
    vt_fox = cache_fox_v.transpose(0, 1, 3, 4, 2).reshape(L, n_pool, wb, page)
    lft_fox = cache_fox_logf.transpose(0, 1, 3, 2)
    kp_diff = cache_diff_k.reshape(L, n_pool, page * H_DIFF, 2 * HEAD_DIM)
    vp_diff = cache_diff_v.reshape(L, n_pool, page * H_DIFF, 2 * HEAD_DIM)
    seg = _block_diag(jnp.ones((H_RWKV, HEAD_DIM, HEAD_DIM), BF16))
    zeros_conv = jnp.zeros((L, B, CONV_W - 1, wb), F32)
    zeros_h = jnp.zeros((L, B, 1, wb), F32)
    zeros_shift = jnp.zeros((L, B, 1, W_RWKV_PROJ), F32)
    zeros_wkv = jnp.zeros((B, H_RWKV, HEAD_DIM, HEAD_DIM), F32)
    eye_f = jnp.eye(H_FOX, dtype=F32)
    eye_2 = jnp.eye(2, dtype=F32)
    row = lambda v: v.reshape(1, -1)

    def pad_rows(z, n):
        return jnp.pad(z, ((0, 0), (0, n - z.shape[1]), (0, 0)))

    def pad_lanes(z):
        return jnp.pad(z, ((0, 0), (0, 0), (0, page - z.shape[2])))

    w_up_b, w_down_b = w_up.astype(BF16), w_down.astype(BF16)
    p_states, s_states = [], []
    for l in range(L):
        lam_init = 0.8 - 0.6 * math.exp(-0.3 * l)
        w = w_in[l]
        n_a = OFF_FF
        split_ff = 3 * wb
        w_r = jnp.concatenate(
            [w[:, :split_ff], w[:, split_ff + H_FOX:split_ff + H_FOX + (n_a - split_ff)],
             w[:, split_ff:split_ff + H_FOX], jnp.zeros((D, FF_PAD), F32),
             w[:, split_ff + H_FOX + (n_a - split_ff):]], axis=1).astype(BF16)
        proj = _rms_matmul(x, row(norm1_g[l]), w_r)
        pp, ps = proj[:mp], proj[mp:]

        lf_p, c_p = _logf_cumsum(pp[:, OFF_FF:OFF_FF + H_FOX].reshape(B, T, H_FOX), row(fox_fb[l]),
                                 jnp.zeros((B, 1, H_FOX), F32))
        lf_s, cn = _logf_cumsum(ps[:, OFF_FF:OFF_FF + H_FOX].reshape(DB, TS, H_FOX), row(fox_fb[l]),
                                jnp.zeros((DB, 1, H_FOX), F32))

        o_fox_p = _fox_flash(proj, c_p, c_p.transpose(0, 2, 1), B, T)

        fq_s = ps[:, OFF_FQ:OFF_FQ + wb].reshape(DB, TS, H_FOX, HEAD_DIM) * QK_SCALE
        qb = jnp.einsum('bthd,hg->bthgd', fq_s, eye_f).reshape(DB, TS * H_FOX, wb).astype(BF16)
        cnb = jnp.broadcast_to(cn.reshape(DB, TS * H_FOX, 1), (DB, TS * H_FOX, page))
        cnt = pad_lanes(cn.transpose(0, 2, 1))
        knt = pad_lanes(ps[:, OFF_FK:OFF_FK + wb].reshape(DB, TS, wb).transpose(0, 2, 1))
        vnt = pad_lanes(ps[:, OFF_FV:OFF_FV + wb].reshape(DB, TS, wb).transpose(0, 2, 1))
        o_fox_s = _fox_decode(page_table, qb, kt_fox, vt_fox, lft_fox, knt, vnt, cnb, cnt, l, TS).reshape(ms, wb)

        lam_vecs = (row(lam_q1[l]), row(lam_k1[l]), row(lam_q2[l]), row(lam_k2[l]))
        g_sub = row(diff_subln_g[l])
        o_diff_p = _diff_flash(t5_table, proj, *lam_vecs, g_sub, lam_init, B, T)

        dq_s = ps[:, OFF_DQ:OFF_DQ + wb].reshape(DB, TS, H_DIFF, 2, HEAD_DIM) * QK_SCALE
        qb_d = jnp.einsum('bthmd,mn->bhmtnd', dq_s, eye_2).reshape(
            DB, 2 * H_DIFF * TS, 2 * HEAD_DIM).astype(BF16)
        kn_d = pad_rows(ps[:, OFF_DK:OFF_DK + wb].reshape(DB, TS * H_DIFF, 2 * HEAD_DIM), page * H_DIFF)
        vn_d = pad_rows(ps[:, OFF_DV:OFF_DV + wb].reshape(DB, TS * H_DIFF, 2 * HEAD_DIM), page * H_DIFF)
        o_diff_s = _diff_decode(page_table, t5_table, qb_d, kp_diff, vp_diff, kn_d, vn_d,
                                *lam_vecs, g_sub, l, TS, lam_init).reshape(ms, wb)

        lru_w = (lru_conv_w[l], row(lru_conv_b[l]), _block_diag(lru_wa[l]).astype(BF16), row(lru_ba[l]),
                 _block_diag(lru_wx[l]).astype(BF16), row(lru_bx[l]), row(lru_lambda[l]))
        o_lru_p, h_p = _lru(proj, 0, B, T, zeros_conv, zeros_h, l, *lru_w)
        o_lru_s, h_s = _lru(proj, mp, DB, TS, state_lru_conv, state_lru_h.reshape(L, DB, 1, wb), l, *lru_w)

        rank = rwkv_w2.shape[1]
        w2p = jnp.concatenate([rwkv_w2[l], jnp.zeros((LANES - rank, wb), F32)], axis=0).astype(BF16)
        a2p = jnp.concatenate([jnp.zeros((LANES - rwkv_a2.shape[1], wb), F32), rwkv_a2[l]], axis=0).astype(BF16)
        rw_w = (row(rwkv_mu[l]), row(rwkv_w0[l]), w2p, row(rwkv_a0[l]), a2p, rwkv_g2[l].astype(BF16),
                row(rwkv_k_k[l]), row(rwkv_k_a[l]), row(rwkv_r_k[l]), seg)
        prep_p = _rwkv_prep(proj, 0, B, T, zeros_shift, l, *rw_w)
        prep_s = _rwkv_prep(proj, mp, DB, TS, state_rwkv_shift.reshape(L, DB, 1, W_RWKV_PROJ), l, *rw_w)
        y_p, wkv_p = _rwkv_scan(*(z.reshape(B, T, wb) for z in prep_p[:6]), zeros_wkv)
        y_s, wkv_s = _rwkv_scan(*(z.reshape(DB, TS, wb) for z in prep_s[:6]), state_rwkv_wkv[l])
        lnx = (row(rwkv_lnx_g[l]), row(rwkv_lnx_b[l]), seg)
        o_rwkv_p = _rwkv_post(y_p.reshape(mp, wb), prep_p[6], prep_p[7], *lnx)
        o_rwkv_s = _rwkv_post(y_s.reshape(ms, wb), prep_s[6], prep_s[7], *lnx)

        wbs = (wb_fox[l].astype(BF16), wb_diff[l].astype(BF16), wb_lru[l].astype(BF16), wb_rwkv[l].astype(BF16))
        mixed = _merge((o_fox_p, o_diff_p, o_lru_p, o_rwkv_p), (o_fox_s, o_diff_s, o_lru_s, o_rwkv_s),
                       wbs, proj, D)
        x = _matmul_res(mixed, w_out[l].astype(BF16), x)
        x = _mlp(x, row(norm2_g[l]), w_up_b, w_down_b, l, row(final_g), normed=(l == L - 1))

        def states(pz, n, t, lf, conv0, h_last, wkv):
            k8 = min(t, SUBLANES)
            cols = slice(OFF_LX, OFF_RW + W_RWKV_PROJ)
            if k8 == t:
                tail = pz[:, cols].reshape(n, t, -1)
            else:
                tail = jnp.stack([pz[(i + 1) * t - k8:(i + 1) * t, cols] for i in range(n)])
            conv_in = jnp.concatenate([conv0, tail[:, :, :wb]], axis=1)
            return (pz[:, OFF_FK:OFF_FK + wb], pz[:, OFF_FV:OFF_FV + wb], lf,
                    pz[:, OFF_DK:OFF_DK + wb], pz[:, OFF_DV:OFF_DV + wb],
                    conv_in[:, -(CONV_W - 1):], h_last.reshape(n, wb),
                    tail[:, -1, OFF_RW - OFF_LX:], wkv)

        p_states.append(states(pp, B, T, lf_p, zeros_conv[0], h_p, wkv_p))
        s_states.append(states(ps, DB, TS, lf_s, state_lru_conv[l], h_s, wkv_s))

    y_prompt = x[:mp].reshape(B, T, D)
    y_sample = x[mp:].reshape(DB, TS, D)
    def stacked(layers, n, t):
        out = [jnp.stack(z) for z in zip(*layers)]
        for i, (h, d) in ((0, (H_FOX, HEAD_DIM)), (1, (H_FOX, HEAD_DIM)),
                          (3, (H_DIFF, 2 * HEAD_DIM)), (4, (H_DIFF, 2 * HEAD_DIM))):
            out[i] = out[i].reshape(L, n, t, h, d)
        return out

    return (y_prompt, y_sample, *stacked(p_states, B, T), *stacked(s_states, DB, TS))
```

```python
import functools
import math

import jax
import jax.numpy as jnp
from jax import lax
from jax.experimental import pallas as pl
from jax.experimental.pallas import tpu as pltpu

F32 = jnp.float32
BF16 = jnp.bfloat16

HEAD_DIM = 64
W_BRANCH = 512
H_FOX = 8
H_DIFF = 4
H_RWKV = 8
CONV_W = 4
LRU_C = 8.0
NUM_BUCKETS = 32
MAX_DISTANCE = 128
RMS_EPS = 1e-6
SUBLN_EPS = 1e-5
GN_EPS = 64e-5
NEG_INF = -1e30
QK_SCALE = HEAD_DIM ** -0.5

LANES = 128
SUBLANES = 8
VMEM_LIMIT_CAP = 56 * 1024 * 1024

OFF_FQ, OFF_FK, OFF_FV = 0, 512, 1024
OFF_DQ, OFF_DK, OFF_DV = 1536, 2048, 2560
OFF_LX, OFF_LG = 3072, 3584
OFF_RW = 4096
W_RWKV_PROJ = 1792
OFF_FF = 5888
OFF_GATE = 6144
FF_PAD = OFF_GATE - OFF_FF - H_FOX


def _tile(n, pref, mult):
    best = None
    t = mult
    while t <= min(n, pref):
        if n % t == 0:
            best = t
        t += mult
    return best if best is not None else n


def _cparams(n_axes, vmem_mb):
    return pltpu.CompilerParams(
        dimension_semantics=("arbitrary",) * n_axes,
        vmem_limit_bytes=min(int(vmem_mb * 1024 * 1024), VMEM_LIMIT_CAP))


def _softplus(x):
    return jnp.maximum(x, 0.0) + jnp.log1p(jnp.exp(-jnp.abs(x)))


def _split3(x):
    def top(v):
        bits = lax.bitcast_convert_type(v, jnp.uint32) & jnp.uint32(0xFFFF0000)
        return lax.bitcast_convert_type(bits, F32)

    x1 = top(x)
    r1 = x - x1
    x2 = top(r1)
    x3 = r1 - x2
    return x1.astype(BF16), x2.astype(BF16), x3.astype(BF16)


def _dot_sel_rhs(x, sel):
    x1, x2, x3 = _split3(x)
    d = functools.partial(jnp.dot, preferred_element_type=F32)
    return d(x1, sel) + d(x2, sel) + d(x3, sel)


def _dot_sel_lhs(sel, x):
    x1, x2, x3 = _split3(x)
    d = functools.partial(jnp.dot, preferred_element_type=F32)
    return d(sel, x1) + d(sel, x2) + d(sel, x3)


def _dot_nt(a, b):
    return lax.dot_general(a, b, (((1,), (1,)), ((), ())), preferred_element_type=F32)


def _rms_matmul_body(x_ref, g_ref, w_ref, o_ref, xn_ref):
    @pl.when(pl.program_id(1) == 0)
    def _():
        x = x_ref[...]
        ms = jnp.mean(x * x, axis=-1, keepdims=True)
        xn_ref[...] = (x * lax.rsqrt(ms + RMS_EPS) * g_ref[...]).astype(BF16)

    o_ref[...] = jnp.dot(xn_ref[...], w_ref[...], preferred_element_type=F32)


def _rms_matmul(x, g, w):
    m, k = x.shape
    n = w.shape[1]
    tm = _tile(m, 1024, SUBLANES)
    tn = _tile(n, 1024, LANES)
    vmem = (2 * tm * k * 4 + 2 * k * tn * 2 + 2 * tm * tn * 4 + tm * k * 2) / 2**20 + 8
    return pl.pallas_call(
        _rms_matmul_body,
        grid=(m // tm, n // tn),
        in_specs=[pl.BlockSpec((tm, k), lambda i, j: (i, 0)),
                  pl.BlockSpec((1, k), lambda i, j: (0, 0)),
                  pl.BlockSpec((k, tn), lambda i, j: (0, j))],
        out_specs=pl.BlockSpec((tm, tn), lambda i, j: (i, j)),
        out_shape=jax.ShapeDtypeStruct((m, n), F32),
        scratch_shapes=[pltpu.VMEM((tm, k), BF16)],
        compiler_params=_cparams(2, vmem),
        name="in_proj",
    )(x, g, w)


def _merge_body(*refs, n_first):
    first, second, ws, gs, o_ref = refs[0:4], refs[4:8], refs[8:12], refs[12:16], refs[16]

    def run(branches):
        acc = None
        for o_r, w_r, g_r in zip(branches, ws, gs):
            y = jax.nn.sigmoid(g_r[...]) * jnp.dot(o_r[...].astype(BF16), w_r[...],
                                                   preferred_element_type=F32)
            acc = y if acc is None else acc + y
        o_ref[...] = acc.astype(BF16)

    @pl.when(pl.program_id(0) < n_first)
    def _():
        run(first)

    @pl.when(pl.program_id(0) >= n_first)
    def _():
        run(second)


def _merge(outs_p, outs_s, wbs, proj, d_model):
    mp, w = outs_p[0].shape
    ms = outs_s[0].shape[0]
    tm = _tile(math.gcd(mp, ms), 512, 16)
    tn = _tile(d_model, 512, LANES)
    n_first = mp // tm
    p_specs = [pl.BlockSpec((tm, w), lambda i, j: (jnp.minimum(i, n_first - 1), 0)) for _ in range(4)]
    s_specs = [pl.BlockSpec((tm, w), lambda i, j: (jnp.maximum(i - n_first, 0), 0)) for _ in range(4)]
    w_specs = [pl.BlockSpec((w, tn), lambda i, j: (0, j)) for _ in range(4)]
    g_specs = [pl.BlockSpec((tm, tn), functools.partial(
        lambda i, j, base: (i, base + j), base=(OFF_GATE + b * d_model) // tn)) for b in range(4)]
    return pl.pallas_call(
        functools.partial(_merge_body, n_first=n_first),
        grid=((mp + ms) // tm, d_model // tn),
        in_specs=p_specs + s_specs + w_specs + g_specs,
        out_specs=pl.BlockSpec((tm, tn), lambda i, j: (i, j)),
        out_shape=jax.ShapeDtypeStruct((mp + ms, d_model), BF16),
        compiler_params=_cparams(2, 40),
        name="merge",
    )(*outs_p, *outs_s, *wbs, proj, proj, proj, proj)


def _matmul_res_body(a_ref, w_ref, r_ref, o_ref):
    o_ref[...] = r_ref[...] + jnp.dot(a_ref[...], w_ref[...], preferred_element_type=F32)


def _matmul_res(a, w, res):
    m, k = a.shape
    n = w.shape[1]
    tm = _tile(m, 512, 16)
    tn = _tile(n, 1024, LANES)
    return pl.pallas_call(
        _matmul_res_body,
        grid=(m // tm, n // tn),
        in_specs=[pl.BlockSpec((tm, k), lambda i, j: (i, 0)),
                  pl.BlockSpec((k, tn), lambda i, j: (0, j)),
                  pl.BlockSpec((tm, tn), lambda i, j: (i, j))],
        out_specs=pl.BlockSpec((tm, tn), lambda i, j: (i, j)),
        out_shape=jax.ShapeDtypeStruct((m, n), F32),
        compiler_params=_cparams(2, 40),
        name="out_proj",
    )(a, w, res)


def _mlp_body(x_ref, g_ref, wu_ref, wd_ref, fg_ref, o_ref, xn_ref, acc_ref, *, n_chunks, normed):
    c = pl.program_id(1)

    @pl.when(c == 0)
    def _():
        x = x_ref[...]
        ms = jnp.mean(x * x, axis=-1, keepdims=True)
        xn_ref[...] = (x * lax.rsqrt(ms + RMS_EPS) * g_ref[...]).astype(BF16)
        acc_ref[...] = jnp.zeros_like(acc_ref)

    h = jnp.dot(xn_ref[...], wu_ref[...], preferred_element_type=F32)
    h = jnp.square(jnp.maximum(h, 0.0))
    acc_ref[...] += jnp.dot(h.astype(BF16), wd_ref[...], preferred_element_type=F32)

    @pl.when(c == n_chunks - 1)
    def _():
        xo = x_ref[...] + acc_ref[...]
        if normed:
            ms = jnp.mean(xo * xo, axis=-1, keepdims=True)
            xo = xo * lax.rsqrt(ms + RMS_EPS) * fg_ref[...]
        o_ref[...] = xo


def _mlp(x, g, wu, wd, layer, final_g, normed):
    m, d = x.shape
    f = wu.shape[2]
    tm = _tile(m, 512, SUBLANES)
    tc = _tile(f, 1024, LANES)
    n_chunks = f // tc
    vmem = (2 * tm * d * 4 + 4 * d * tc * 2 + 2 * tm * d * 4 + tm * d * 2 + tm * d * 4) / 2**20 + 12
    return pl.pallas_call(
        functools.partial(_mlp_body, n_chunks=n_chunks, normed=normed),
        grid=(m // tm, n_chunks),
        in_specs=[pl.BlockSpec((tm, d), lambda i, c: (i, 0)),
                  pl.BlockSpec((1, d), lambda i, c: (0, 0)),
                  pl.BlockSpec((None, d, tc), lambda i, c: (layer, 0, c)),
                  pl.BlockSpec((None, tc, d), lambda i, c: (layer, c, 0)),
                  pl.BlockSpec((1, d), lambda i, c: (0, 0))],
        out_specs=pl.BlockSpec((tm, d), lambda i, c: (i, 0)),
        out_shape=jax.ShapeDtypeStruct((m, d), F32),
        scratch_shapes=[pltpu.VMEM((tm, d), BF16), pltpu.VMEM((tm, d), F32)],
        compiler_params=_cparams(2, vmem),
        name="mlp",
    )(x, g, wu, wd, final_g)


def _cumsum_rows(x, tc):
    if tc >= LANES:
        r = lax.broadcasted_iota(jnp.int32, (tc, tc), 0)
        c = lax.broadcasted_iota(jnp.int32, (tc, tc), 1)
        tri = jnp.where(c <= r, 1.0, 0.0).astype(BF16)
        return _dot_sel_lhs(tri, x)
    rows = lax.broadcasted_iota(jnp.int32, x.shape, 0)
    parts = [jnp.sum(jnp.where(rows <= t, x, 0.0), axis=0, keepdims=True) for t in range(tc)]
    return jnp.concatenate(parts, axis=0)


def _logf_cumsum_body(ff_ref, fb_ref, c0_ref, lf_ref, c_ref, carry_ref, *, tc):
    @pl.when(pl.program_id(1) == 0)
    def _():
        carry_ref[...] = c0_ref[...]

    lf = -_softplus(-(ff_ref[...] + fb_ref[...]))
    lf_ref[...] = lf
    cs = _cumsum_rows(lf, tc) + carry_ref[...]
    c_ref[...] = cs
    carry_ref[...] = cs[tc - 1:tc, :]


def _logf_cumsum(ff, fb, c0):
    nseq, t, h = ff.shape
    tc = _tile(t, 512, SUBLANES)
    blk = pl.BlockSpec((None, tc, h), lambda s, c: (s, c, 0))
    return pl.pallas_call(
        functools.partial(_logf_cumsum_body, tc=tc),
        grid=(nseq, t // tc),
        in_specs=[blk, pl.BlockSpec((1, h), lambda s, c: (0, 0)),
                  pl.BlockSpec((None, 1, h), lambda s, c: (s, 0, 0))],
        out_specs=[blk, blk],
        out_shape=[jax.ShapeDtypeStruct((nseq, t, h), F32)] * 2,
        scratch_shapes=[pltpu.VMEM((1, h), F32)],
        compiler_params=_cparams(2, 24),
        name="logf_cumsum",
    )(ff, fb, c0)


def _t5_bucket(n):
    max_exact = NUM_BUCKETS // 2
    nf = jnp.maximum(n, 1).astype(F32)
    large = max_exact + (jnp.log(nf / max_exact) / math.log(MAX_DISTANCE / max_exact)
                         * (NUM_BUCKETS - max_exact)).astype(jnp.int32)
    return jnp.where(n < max_exact, n, jnp.minimum(large, NUM_BUCKETS - 1))


def _t5_lookup(bucket, tab_ref, col):
    acc = jnp.zeros(bucket.shape, F32)
    for j in range(NUM_BUCKETS):
        acc = jnp.where(bucket == j, tab_ref[j, col], acc)
    return acc


def _softmax_update(s, v, m_ref, l_ref, acc_ref, idx):
    m_prev = m_ref[idx]
    m_new = jnp.maximum(m_prev, jnp.max(s, axis=-1, keepdims=True))
    alpha = jnp.exp(m_prev - m_new)
    p = jnp.exp(s - pltpu.repeat(m_new, s.shape[-1] // LANES, axis=1))
    l_ref[idx] = alpha * l_ref[idx] + jnp.sum(p, axis=-1, keepdims=True)
    acc_ref[idx] = (alpha[:, :v.shape[-1]] * acc_ref[idx]
                    + jnp.dot(p.astype(BF16), v, preferred_element_type=F32))
    m_ref[idx] = m_new


def _fox_flash_body(q_ref, k_ref, v_ref, c_ref, ct_ref, o_ref, m_ref, l_ref, acc_ref, *, tq):
    qi = pl.program_id(1)
    ki = pl.program_id(2)

    @pl.when(ki == 0)
    def _():
        m_ref[...] = jnp.full(m_ref.shape, NEG_INF, F32)
        l_ref[...] = jnp.zeros_like(l_ref)
        acc_ref[...] = jnp.zeros_like(acc_ref)

    first = lax.broadcasted_iota(jnp.int32, (tq, LANES), 1) < HEAD_DIM

    def tiles(masked):
        if masked:
            row = lax.broadcasted_iota(jnp.int32, (tq, tq), 0)
            col = lax.broadcasted_iota(jnp.int32, (tq, tq), 1)
            keep = col <= row
        for g in range(H_FOX // 2):
            grp = slice(g * LANES, (g + 1) * LANES)
            q2 = q_ref[:, grp] * QK_SCALE
            k2 = k_ref[:, grp].astype(BF16)
            v2 = v_ref[:, grp].astype(BF16)
            for j in range(2):
                h = 2 * g + j
                own = first if j == 0 else jnp.logical_not(first)
                s = _dot_nt(jnp.where(own, q2, 0.0).astype(BF16), k2)
                s = s + c_ref[:, h:h + 1] - ct_ref[h:h + 1, :]
                if masked:
                    s = jnp.where(keep, s, NEG_INF)
                m_prev = m_ref[h]
                m_new = jnp.maximum(m_prev, jnp.max(s, axis=-1, keepdims=True))
                alpha = jnp.exp(m_prev - m_new)
                p = jnp.exp(s - pltpu.repeat(m_new, tq // LANES, axis=1))
                l_ref[h] = alpha * l_ref[h] + jnp.sum(p, axis=-1, keepdims=True)
                pv = jnp.dot(p.astype(BF16), v2, preferred_element_type=F32)
                acc_ref[h] = alpha * acc_ref[h] + pv
                m_ref[h] = m_new

    @pl.when(ki < qi)
    def _():
        tiles(False)

    @pl.when(ki == qi)
    def _():
        tiles(True)
        for g in range(H_FOX // 2):
            o_ref[:, g * LANES:(g + 1) * LANES] = jnp.where(
                first, acc_ref[2 * g] / l_ref[2 * g], acc_ref[2 * g + 1] / l_ref[2 * g + 1])


def _qkv_specs(t, tq, col0, w):
    nq = t // tq
    c = col0 // w
    return [pl.BlockSpec((tq, w), lambda bi, qi, ki: (bi * nq + qi, c)),
            pl.BlockSpec((tq, w), lambda bi, qi, ki: (bi * nq + jnp.minimum(ki, qi), c + 1)),
            pl.BlockSpec((tq, w), lambda bi, qi, ki: (bi * nq + jnp.minimum(ki, qi), c + 2))]


def _fox_flash(proj, c, ct, b, t):
    w = W_BRANCH
    tq = _tile(t, 512, LANES)
    nq = t // tq
    return pl.pallas_call(
        functools.partial(_fox_flash_body, tq=tq),
        grid=(b, nq, nq),
        in_specs=_qkv_specs(t, tq, OFF_FQ, w) + [
            pl.BlockSpec((None, tq, H_FOX), lambda bi, qi, ki: (bi, qi, 0)),
            pl.BlockSpec((None, H_FOX, tq), lambda bi, qi, ki: (bi, 0, jnp.minimum(ki, qi)))],
        out_specs=pl.BlockSpec((tq, w), lambda bi, qi, ki: (bi * nq + qi, 0)),
        out_shape=jax.ShapeDtypeStruct((b * t, w), F32),
        scratch_shapes=[pltpu.VMEM((H_FOX, tq, LANES), F32), pltpu.VMEM((H_FOX, tq, LANES), F32),
                        pltpu.VMEM((H_FOX, tq, LANES), F32)],
        compiler_params=_cparams(3, 48),
        name="fox_flash",
    )(proj, proj, proj, c, ct)


def _lambda(lq1_ref, lk1_ref, lq2_ref, lk2_ref, lam_init):
    s1 = jnp.sum(lq1_ref[...] * lk1_ref[...], axis=-1, keepdims=True)
    s2 = jnp.sum(lq2_ref[...] * lk2_ref[...], axis=-1, keepdims=True)
    return jnp.exp(s1) - jnp.exp(s2) + lam_init


def _subln(o, g, lam_init):
    ms = jnp.mean(o * o, axis=-1, keepdims=True)
    return o * lax.rsqrt(ms + SUBLN_EPS) * g * (1.0 - lam_init)


def _diff_flash_body(tab_ref, q_ref, k_ref, v_ref, lq1_ref, lk1_ref, lq2_ref, lk2_ref, g_ref,
                     o_ref, bias_ref, m_ref, l_ref, acc_ref, *, tq, lam_init):
    bi = pl.program_id(0)
    qi = pl.program_id(1)
    ki = pl.program_id(2)

    @pl.when((bi == 0) & (qi == 0) & (ki == 0))
    def _():
        def fill(rb, carry):
            r0 = pl.multiple_of(rb * SUBLANES, SUBLANES)
            rows = r0 + lax.broadcasted_iota(jnp.int32, (SUBLANES, tq), 0)
            cols = lax.broadcasted_iota(jnp.int32, (SUBLANES, tq), 1)
            for which in range(2):
                bucket = _t5_bucket(jnp.maximum(rows - cols + which * tq, 0))
                for c in range(2 * H_DIFF):
                    bias_ref[c, which, pl.ds(r0, SUBLANES), :] = _t5_lookup(bucket, tab_ref, c)
            return carry
        lax.fori_loop(0, tq // SUBLANES, fill, 0)

    @pl.when(ki == 0)
    def _():
        m_ref[...] = jnp.full(m_ref.shape, NEG_INF, F32)
        l_ref[...] = jnp.zeros_like(l_ref)
        acc_ref[...] = jnp.zeros_like(acc_ref)

    def tiles(mode):
        lane = lax.broadcasted_iota(jnp.int32, (tq, 2 * HEAD_DIM), 1)
        if mode == 0:
            row = lax.broadcasted_iota(jnp.int32, (tq, tq), 0)
            col = lax.broadcasted_iota(jnp.int32, (tq, tq), 1)
            keep = col <= row
        for h in range(H_DIFF):
            grp = slice(h * 2 * HEAD_DIM, (h + 1) * 2 * HEAD_DIM)
            q = q_ref[:, grp] * QK_SCALE
            k = k_ref[:, grp].astype(BF16)
            v = v_ref[:, grp].astype(BF16)
            for mp in range(2):
                c = mp * H_DIFF + h
                qm = jnp.where((lane >= mp * HEAD_DIM) & (lane < (mp + 1) * HEAD_DIM), q, 0.0)
                s = _dot_nt(qm.astype(BF16), k)
                if mode == 0:
                    s = jnp.where(keep, s + bias_ref[c, 0], NEG_INF)
                elif mode == 1:
                    s = s + bias_ref[c, 1]
                else:
                    s = s + tab_ref[NUM_BUCKETS - 1, c]
                _softmax_update(s, v, m_ref, l_ref, acc_ref, c)

    @pl.when(ki < qi - 1)
    def _():
        tiles(2)

    @pl.when(ki == qi - 1)
    def _():
        tiles(1)

    @pl.when(ki == qi)
    def _():
        tiles(0)
        lam = _lambda(lq1_ref, lk1_ref, lq2_ref, lk2_ref, lam_init)
        for h in range(H_DIFF):
            o = acc_ref[h] / l_ref[h] - lam * (acc_ref[H_DIFF + h] / l_ref[H_DIFF + h])
            o_ref[:, h * 2 * HEAD_DIM:(h + 1) * 2 * HEAD_DIM] = _subln(o, g_ref[...], lam_init)


def _diff_flash(table, proj, lq1, lk1, lq2, lk2, g, lam_init, b, t):
    w = W_BRANCH
    d = 2 * HEAD_DIM
    h = H_DIFF
    tq = _tile(t, 512, LANES)
    assert tq >= MAX_DISTANCE or tq == t, "far blocks must lie in the last bucket"
    nq = t // tq
    vec = pl.BlockSpec((1, HEAD_DIM), lambda bi, qi, ki: (0, 0))
    return pl.pallas_call(
        functools.partial(_diff_flash_body, tq=tq, lam_init=lam_init),
        grid=(b, nq, nq),
        in_specs=[pl.BlockSpec(memory_space=pltpu.SMEM)] + _qkv_specs(t, tq, OFF_DQ, w) + [
            vec, vec, vec, vec, pl.BlockSpec((1, d), lambda bi, qi, ki: (0, 0))],
        out_specs=pl.BlockSpec((tq, w), lambda bi, qi, ki: (bi * nq + qi, 0)),
        out_shape=jax.ShapeDtypeStruct((b * t, w), F32),
        scratch_shapes=[pltpu.VMEM((2 * h, 2, tq, tq), F32),
                        pltpu.VMEM((2 * h, tq, LANES), F32), pltpu.VMEM((2 * h, tq, LANES), F32),
                        pltpu.VMEM((2 * h, tq, d), F32)],
        compiler_params=_cparams(3, 54),
        name="diff_flash",
    )(table, proj, proj, proj, lq1, lk1, lq2, lk2, g)


def _softmax_tiles(tiles):
    m = tiles[0][0].max(axis=-1, keepdims=True)
    for s, _ in tiles[1:]:
        m = jnp.maximum(m, s.max(axis=-1, keepdims=True))
    l, acc = None, None
    for s, pv in tiles:
        p = jnp.exp(s - m)
        ps, pa = jnp.sum(p, axis=-1, keepdims=True), pv(p.astype(BF16))
        l, acc = (ps, pa) if l is None else (l + ps, acc + pa)
    return acc / l


def _page_specs(layer, n_pages, rows, cols):
    return [pl.BlockSpec((None, None, rows, cols),
                         functools.partial(lambda b, pt, p: (layer, pt[b, p], 0, 0), p=p))
            for p in range(n_pages)]


def _fox_dec_body(pt_ref, qb_ref, *refs, n_pages, ts):
    kt = refs[:n_pages]
    vt = refs[n_pages:2 * n_pages]
    lf_all_ref, knt_ref, vnt_ref, cnb_ref, cnt_ref, o_ref = refs[2 * n_pages:]
    b = pl.program_id(0)
    lf = [lf_all_ref[pt_ref[b, p]] for p in range(n_pages)]
    rows = H_FOX * ts
    page = knt_ref.shape[-1]
    qb = qb_ref[...]
    qk = lambda k_ref: jnp.dot(qb, k_ref[...].astype(BF16), preferred_element_type=F32)
    pv = lambda v_ref: (lambda p: _dot_nt(p, v_ref[...].astype(BF16)))

    lane = lax.broadcasted_iota(jnp.int32, (ts, H_FOX, page), 2).reshape(rows, page)
    tok = lax.broadcasted_iota(jnp.int32, (ts, H_FOX, page), 0).reshape(rows, page)
    cn_keys = jnp.concatenate([cnt_ref[...]] * ts, axis=0)
    bias_new = jnp.where((lane <= tok) & (lane < ts), cnb_ref[...] - cn_keys, NEG_INF)
    tiles = [(qk(knt_ref) + bias_new, pv(vnt_ref))]

    j_idx = lax.broadcasted_iota(jnp.int32, (page, page), 0)
    k_idx = lax.broadcasted_iota(jnp.int32, (page, page), 1)
    later = jnp.where(j_idx > k_idx, 1.0, 0.0).astype(BF16)
    in_page = _dot_sel_rhs(jnp.concatenate(lf, axis=0), later)
    carry = jnp.zeros((H_FOX, 1), F32)
    for p in reversed(range(n_pages)):
        suffix = in_page[p * H_FOX:(p + 1) * H_FOX] + carry
        carry = suffix[:, 0:1] + lf[p][:, 0:1]
        bias = jnp.concatenate([suffix] * ts, axis=0) + cnb_ref[...]
        tiles.append((qk(kt[p]) + bias, pv(vt[p])))

    accn = _softmax_tiles(tiles).reshape(ts, H_FOX, W_BRANCH)
    col_head = lax.shift_right_logical(lax.broadcasted_iota(jnp.int32, (H_FOX, W_BRANCH), 1),
                                       HEAD_DIM.bit_length() - 1)
    own = col_head == lax.broadcasted_iota(jnp.int32, (H_FOX, W_BRANCH), 0)
    o_ref[...] = jnp.sum(jnp.where(own[None], accn, 0.0), axis=1)


def _fox_decode(page_table, qb, kt, vt, lft, knt, vnt, cnb, cnt, layer, ts):
    db, rows, w = qb.shape
    page = kt.shape[-1]
    n_pages = page_table.shape[1]
    per_seq = lambda r, c: pl.BlockSpec((None, r, c), lambda b, pt: (b, 0, 0))
    grid_spec = pltpu.PrefetchScalarGridSpec(
        num_scalar_prefetch=1,
        grid=(db,),
        in_specs=([per_seq(rows, w)] + _page_specs(layer, n_pages, w, page) * 2
                  + [pl.BlockSpec((None,) + lft.shape[1:], lambda b, pt: (layer, 0, 0, 0),
                                  pipeline_mode=pl.Buffered(1))]
                  + [per_seq(w, page), per_seq(w, page), per_seq(rows, page), per_seq(H_FOX, page)]),
        out_specs=per_seq(ts, w))
    return pl.pallas_call(
        functools.partial(_fox_dec_body, n_pages=n_pages, ts=ts),
        grid_spec=grid_spec,
        out_shape=jax.ShapeDtypeStruct((db, ts, w), F32),
        compiler_params=_cparams(1, 40),
        name="fox_decode",
    )(page_table, qb, *([kt] * n_pages), *([vt] * n_pages), lft, knt, vnt, cnb, cnt)


def _diff_dec_body(pt_ref, tab_ref, q_ref, *refs, n_pages, ts, page, lam_init):
    kp = refs[:n_pages]
    vp = refs[n_pages:2 * n_pages]
    (kn_ref, vn_ref, lq1_ref, lk1_ref, lq2_ref, lk2_ref, g_ref, o_ref, bias_ref) = refs[2 * n_pages:]
    rows = 2 * H_DIFF * ts
    cols = page * H_DIFF
    past = n_pages * page

    @pl.when(pl.program_id(0) == 0)
    def _():
        lane = lax.broadcasted_iota(jnp.int32, (ts, cols), 1)
        pos = lax.shift_right_logical(lane, H_DIFF.bit_length() - 1)
        head = lane & (H_DIFF - 1)
        qpos = past + lax.broadcasted_iota(jnp.int32, (ts, cols), 0)

        def fill(tile, kpos, valid):
            bucket = _t5_bucket(jnp.maximum(qpos - kpos, 0))
            for rb in range(2 * H_DIFF):
                h, mp = rb // 2, rb % 2
                val = _t5_lookup(bucket, tab_ref, mp * H_DIFF + h)
                bias_ref[tile, rb * ts:(rb + 1) * ts, :] = jnp.where(valid & (head == h), val, NEG_INF)

        def past_tile(tile, carry):
            kpos = tile * page + pos
            fill(tile, kpos, kpos <= qpos)
            return carry

        lax.fori_loop(0, n_pages, past_tile, 0)
        kpos = past + pos
        fill(n_pages, kpos, (kpos <= qpos) & (pos < ts))

    q = q_ref[...]
    tiles = []
    for p in range(n_pages + 1):
        k_ref, v_ref = (kp[p], vp[p]) if p < n_pages else (kn_ref, vn_ref)
        s = _dot_nt(q, k_ref[...].astype(BF16)) + bias_ref[p]
        tiles.append((s, lambda pr, v_ref=v_ref: jnp.dot(pr, v_ref[...].astype(BF16),
                                                         preferred_element_type=F32)))

    lam = _lambda(lq1_ref, lk1_ref, lq2_ref, lk2_ref, lam_init)
    on = _softmax_tiles(tiles)
    for h in range(H_DIFF):
        o1 = on[2 * h * ts:(2 * h + 1) * ts, :]
        o2 = on[(2 * h + 1) * ts:(2 * h + 2) * ts, :]
        o_ref[:, h * 2 * HEAD_DIM:(h + 1) * 2 * HEAD_DIM] = _subln(o1 - lam * o2, g_ref[...], lam_init)


def _diff_decode(page_table, table, q, kp, vp, kn, vn, lq1, lk1, lq2, lk2, g, layer, ts, lam_init):
    db, rows, d = q.shape
    cols = kp.shape[2]
    page = cols // H_DIFF
    n_pages = page_table.shape[1]
    per_seq = lambda r, c: pl.BlockSpec((None, r, c), lambda b, pt: (b, 0, 0))
    vec = pl.BlockSpec((1, HEAD_DIM), lambda b, pt: (0, 0))
    grid_spec = pltpu.PrefetchScalarGridSpec(
        num_scalar_prefetch=1,
        grid=(db,),
        in_specs=([pl.BlockSpec(memory_space=pltpu.SMEM), per_seq(rows, d)]
                  + _page_specs(layer, n_pages, cols, d) * 2
                  + [per_seq(cols, d), per_seq(cols, d), vec, vec, vec, vec,
                     pl.BlockSpec((1, d), lambda b, pt: (0, 0))]),
        out_specs=per_seq(ts, H_DIFF * d),
        scratch_shapes=[pltpu.VMEM((n_pages + 1, rows, cols), F32)])
    return pl.pallas_call(
        functools.partial(_diff_dec_body, n_pages=n_pages, ts=ts, page=page, lam_init=lam_init),
        grid_spec=grid_spec,
        out_shape=jax.ShapeDtypeStruct((db, ts, H_DIFF * d), F32),
        compiler_params=_cparams(1, 40),
        name="diff_decode",
    )(page_table, table, q, *([kp] * n_pages), *([vp] * n_pages), kn, vn, lq1, lk1, lq2, lk2, g)


def _lru_body(lx_ref, lg_ref, c0_ref, h0_ref, cw_ref, cb_ref, wa_ref, ba_ref, wx_ref, bx_ref,
              lam_ref, o_ref, hn_ref, cin_ref, a_ref, u_ref, hs_ref, hc_ref, *, tc):
    @pl.when(pl.program_id(1) == 0)
    def _():
        cin_ref[0:SUBLANES, :] = jnp.zeros((SUBLANES, W_BRANCH), F32)
        cin_ref[SUBLANES - (CONV_W - 1):SUBLANES, :] = c0_ref[...]
        hc_ref[...] = h0_ref[...]

    lx = lx_ref[...]
    cin_ref[SUBLANES:SUBLANES + tc, :] = lx
    base = SUBLANES - (CONV_W - 1)
    acc = cw_ref[0:1, :] * cin_ref[base:base + tc, :]
    for j in range(1, CONV_W):
        acc = acc + cw_ref[j:j + 1, :] * cin_ref[base + j:base + j + tc, :]
    xc = cb_ref[...] + acc
    xb = xc.astype(BF16)
    r_gate = jax.nn.sigmoid(jnp.dot(xb, wa_ref[...], preferred_element_type=F32) + ba_ref[...])
    i_gate = jax.nn.sigmoid(jnp.dot(xb, wx_ref[...], preferred_element_type=F32) + bx_ref[...])
    log_a = -LRU_C * r_gate * _softplus(-lam_ref[...])
    a_ref[...] = jnp.exp(log_a)
    u_ref[...] = jnp.sqrt(1.0 - jnp.exp(2.0 * log_a)) * (i_gate * xc)

    def step(t, h):
        h = a_ref[pl.ds(t, 1), :] * h + u_ref[pl.ds(t, 1), :]
        hs_ref[pl.ds(t, 1), :] = h
        return h

    h = lax.fori_loop(0, tc, step, hc_ref[...], unroll=8)
    hc_ref[...] = h
    hn_ref[...] = h
    o_ref[...] = hs_ref[...] * jax.nn.gelu(lg_ref[...])
    cin_ref[0:SUBLANES, :] = lx[tc - SUBLANES:tc, :]


def _lru(proj, row0, nseq, t, conv0, h0, layer, cw, cb, wa, ba, wx, bx, lam):
    m_total = proj.shape[0]
    tc = _tile(t, 512, SUBLANES)
    nc = t // tc
    blk0 = row0 // tc
    col = lambda off: (lambda s, c: (blk0 + s * nc + c, off // W_BRANCH))
    vec = pl.BlockSpec((1, W_BRANCH), lambda s, c: (0, 0))
    mat = pl.BlockSpec((W_BRANCH, W_BRANCH), lambda s, c: (0, 0))
    return pl.pallas_call(
        functools.partial(_lru_body, tc=tc),
        grid=(nseq, nc),
        in_specs=[pl.BlockSpec((tc, W_BRANCH), col(OFF_LX)),
                  pl.BlockSpec((tc, W_BRANCH), col(OFF_LG)),
                  pl.BlockSpec((None, None, CONV_W - 1, W_BRANCH), lambda s, c: (layer, s, 0, 0)),
                  pl.BlockSpec((None, None, 1, W_BRANCH), lambda s, c: (layer, s, 0, 0)),
                  pl.BlockSpec((CONV_W, W_BRANCH), lambda s, c: (0, 0)),
                  vec, mat, vec, mat, vec, vec],
        out_specs=[pl.BlockSpec((tc, W_BRANCH), lambda s, c: (s * nc + c, 0)),
                   pl.BlockSpec((None, 1, W_BRANCH), lambda s, c: (s, 0, 0))],
        out_shape=[jax.ShapeDtypeStruct((nseq * t, W_BRANCH), F32),
                   jax.ShapeDtypeStruct((nseq, 1, W_BRANCH), F32)],
        scratch_shapes=[pltpu.VMEM((tc + SUBLANES, W_BRANCH), F32),
                        pltpu.VMEM((tc, W_BRANCH), F32), pltpu.VMEM((tc, W_BRANCH), F32),
                        pltpu.VMEM((tc, W_BRANCH), F32), pltpu.VMEM((1, W_BRANCH), F32)],
        compiler_params=_cparams(2, 32),
        name="lru",
    )(proj, proj, conv0, h0, cw, cb, wa, ba, wx, bx, lam)


def _rwkv_prep_body(rr_ref, rk_ref, rv_ref, lo_ref, sh_ref, mu_ref, w0_ref, w2_ref, a0_ref, a2_ref,
                    g2_ref, kk_ref, ka_ref, rkk_ref, seg_ref,
                    r_o, w_o, k_o, v_o, a_o, b_o, g_o, bon_o, xs_ref, *, tc):
    wb = W_BRANCH

    @pl.when(pl.program_id(1) == 0)
    def _():
        xs_ref[0:SUBLANES, :] = jnp.zeros((SUBLANES, W_RWKV_PROJ), F32)
        xs_ref[SUBLANES - 1:SUBLANES, :] = sh_ref[...]

    xs_ref[SUBLANES:SUBLANES + tc, 0:wb] = rr_ref[...]
    xs_ref[SUBLANES:SUBLANES + tc, wb:2 * wb] = rk_ref[...]
    xs_ref[SUBLANES:SUBLANES + tc, 2 * wb:3 * wb] = rv_ref[...]
    xs_ref[SUBLANES:SUBLANES + tc, 3 * wb:W_RWKV_PROJ] = lo_ref[...]
    cur = xs_ref[SUBLANES:SUBLANES + tc, :]
    prev = xs_ref[SUBLANES - 1:SUBLANES - 1 + tc, :]
    rx = cur + (prev - cur) * mu_ref[...]
    xs_ref[0:SUBLANES, :] = cur[tc - SUBLANES:tc, :]

    rr = rx[:, 0:wb]
    rk = rx[:, wb:2 * wb]
    rv = rx[:, 2 * wb:3 * wb]
    wl_al = rx[:, 3 * wb:3 * wb + LANES]
    gl = rx[:, 3 * wb + LANES:W_RWKV_PROJ]
    dot = functools.partial(jnp.dot, preferred_element_type=F32)
    wd = -_softplus(-(w0_ref[...] + dot(jnp.tanh(wl_al).astype(BF16), w2_ref[...]))) - 0.5
    decay = jnp.exp(-jnp.exp(wd))
    aa = jax.nn.sigmoid(a0_ref[...] + dot(wl_al.astype(BF16), a2_ref[...]))
    gg = dot(jax.nn.sigmoid(gl).astype(BF16), g2_ref[...])
    kk = rk * kk_ref[...]
    norm = jnp.sqrt(_dot_sel_rhs(kk * kk, seg_ref[...]))
    kk = kk / jnp.maximum(norm, 1e-12)
    kh = rk * (1.0 + (aa - 1.0) * ka_ref[...])
    r_o[...] = rr
    w_o[...] = decay
    k_o[...] = kh
    v_o[...] = rv
    a_o[...] = -kk
    b_o[...] = kk * aa
    g_o[...] = gg
    bon_o[...] = _dot_sel_rhs(rr * kh * rkk_ref[...], seg_ref[...]) * rv


def _rwkv_prep(proj, row0, nseq, t, shift0, layer, mu, w0, w2p, a0, a2p, g2, k_k, k_a, r_k, seg):
    tc = _tile(t, 256, SUBLANES)
    nc = t // tc
    blk0 = row0 // tc
    wb = W_BRANCH
    col = lambda off, w: (lambda s, c: (blk0 + s * nc + c, off // w))
    vec = pl.BlockSpec((1, wb), lambda s, c: (0, 0))
    out_blk = pl.BlockSpec((tc, wb), lambda s, c: (s * nc + c, 0))
    lo_w = W_RWKV_PROJ - 3 * wb
    return pl.pallas_call(
        functools.partial(_rwkv_prep_body, tc=tc),
        grid=(nseq, nc),
        in_specs=[pl.BlockSpec((tc, wb), col(OFF_RW, wb)),
                  pl.BlockSpec((tc, wb), col(OFF_RW + wb, wb)),
                  pl.BlockSpec((tc, wb), col(OFF_RW + 2 * wb, wb)),
                  pl.BlockSpec((tc, lo_w), col(OFF_RW + 3 * wb, lo_w)),
                  pl.BlockSpec((None, None, 1, W_RWKV_PROJ), lambda s, c: (layer, s, 0, 0)),
                  pl.BlockSpec((1, W_RWKV_PROJ), lambda s, c: (0, 0)),
                  vec, pl.BlockSpec((LANES, wb), lambda s, c: (0, 0)),
                  vec, pl.BlockSpec((LANES, wb), lambda s, c: (0, 0)),
                  pl.BlockSpec((LANES, wb), lambda s, c: (0, 0)),
                  vec, vec, vec, pl.BlockSpec((wb, wb), lambda s, c: (0, 0))],
        out_specs=[out_blk] * 8,
        out_shape=[jax.ShapeDtypeStruct((nseq * t, wb), F32)] * 8,
        scratch_shapes=[pltpu.VMEM((tc + SUBLANES, W_RWKV_PROJ), F32)],
        compiler_params=_cparams(2, 40),
        name="rwkv_prep",
    )(proj, proj, proj, proj, shift0, mu, w0, w2p, a0, a2p, g2, k_k, k_a, r_k, seg)


def _rwkv_scan_body(r_ref, w_ref, k_ref, v_ref, a_ref, b_ref, s0_ref, y_ref, sn_ref, st_ref,
                    *, nb, tc, n_chunks):
    c = pl.program_id(1)

    @pl.when(c == 0)
    def _():
        st_ref[...] = s0_ref[...]

    n_pairs = H_RWKV // 2
    shape = (HEAD_DIM, LANES)
    lane = lax.broadcasted_iota(jnp.int32, shape, 1)
    first = lane < HEAD_DIM
    eye2 = jnp.where((lane & (HEAD_DIM - 1)) == lax.broadcasted_iota(jnp.int32, shape, 0), 1.0, 0.0)
    half = lambda d: lax.shift_right_logical(lax.broadcasted_iota(jnp.int32, (LANES, LANES), d),
                                             HEAD_DIM.bit_length() - 1)
    ones_bd = jnp.where(half(0) == half(1), 1.0, 0.0).astype(BF16)

    def step(t, states):
        new_states = []
        for s in range(nb):
            rows = [ref[s, pl.ds(t, 1), :] for ref in (r_ref, w_ref, k_ref, v_ref, a_ref, b_ref)]
            y_parts = []
            for p in range(n_pairs):
                r_t, w_t, k_t, v_t, a_t, b_t = (z[:, p * LANES:(p + 1) * LANES] for z in rows)
                st = states[s * n_pairs + p]
                sa = jnp.dot((st * a_t).astype(BF16), ones_bd, preferred_element_type=F32)
                d = eye2 * v_t
                v1 = jnp.sum(jnp.where(first, d, 0.0), axis=1, keepdims=True)
                v2 = jnp.sum(jnp.where(first, 0.0, d), axis=1, keepdims=True)
                st = st * w_t + sa * b_t + jnp.where(first, v1, v2) * k_t
                y = jnp.dot((st * r_t).astype(BF16), ones_bd, preferred_element_type=F32)
                y_parts.append(jnp.sum(eye2 * y, axis=0, keepdims=True))
                new_states.append(st)
            y_ref[s, pl.ds(t, 1), :] = jnp.concatenate(y_parts, axis=1)
        return tuple(new_states)

    init = tuple(st_ref[s, p] for s in range(nb) for p in range(n_pairs))
    final = lax.fori_loop(0, tc, step, init, unroll=64)
    for i, st in enumerate(final):
        st_ref[i // n_pairs, i % n_pairs] = st

    @pl.when(c == n_chunks - 1)
    def _():
        sn_ref[...] = st_ref[...]


def _pack_pairs(s):
    n = s.shape[0]
    s = s.reshape(n, H_RWKV // 2, 2, HEAD_DIM, HEAD_DIM).transpose(0, 1, 3, 2, 4)
    return s.reshape(n, H_RWKV // 2, HEAD_DIM, 2 * HEAD_DIM)


def _unpack_pairs(s):
    n = s.shape[0]
    s = s.reshape(n, H_RWKV // 2, HEAD_DIM, 2, HEAD_DIM).transpose(0, 1, 3, 2, 4)
    return s.reshape(n, H_RWKV, HEAD_DIM, HEAD_DIM)


def _rwkv_scan(r, w, k, v, a, b, s0):
    nseq, t, wd = r.shape
    nb = 2 if nseq % 2 == 0 else 1
    tc = _tile(t, LANES, SUBLANES)
    n_chunks = t // tc
    n_pairs = H_RWKV // 2
    blk = pl.BlockSpec((nb, tc, wd), lambda s, c: (s, c, 0))
    st_blk = pl.BlockSpec((nb, n_pairs, HEAD_DIM, LANES), lambda s, c: (s, 0, 0, 0))
    y, sn = pl.pallas_call(
        functools.partial(_rwkv_scan_body, nb=nb, tc=tc, n_chunks=n_chunks),
        grid=(nseq // nb, n_chunks),
        in_specs=[blk] * 6 + [st_blk],
        out_specs=[blk, st_blk],
        out_shape=[jax.ShapeDtypeStruct((nseq, t, wd), F32),
                   jax.ShapeDtypeStruct((nseq, n_pairs, HEAD_DIM, LANES), F32)],
        scratch_shapes=[pltpu.VMEM((nb, n_pairs, HEAD_DIM, LANES), F32)],
        compiler_params=_cparams(2, 40),
        name="rwkv_scan",
    )(r, w, k, v, a, b, _pack_pairs(s0))
    return y, _unpack_pairs(sn)


def _rwkv_post_body(y_ref, g_ref, bon_ref, lg_ref, lb_ref, seg_ref, o_ref):
    y = y_ref[...]
    inv_n = 1.0 / HEAD_DIM
    mu = _dot_sel_rhs(y, seg_ref[...]) * inv_n
    d = y - mu
    var = _dot_sel_rhs(d * d, seg_ref[...]) * inv_n
    yn = d * lax.rsqrt(var + GN_EPS) * lg_ref[...] + lb_ref[...]
    o_ref[...] = (yn + bon_ref[...]) * g_ref[...]


def _rwkv_post(y, gg, bonus, lnx_g, lnx_b, seg):
    m, wb = y.shape
    tm = _tile(m, 512, SUBLANES)
    blk = pl.BlockSpec((tm, wb), lambda i: (i, 0))
    vec = pl.BlockSpec((1, wb), lambda i: (0, 0))
    return pl.pallas_call(
        _rwkv_post_body,
        grid=(m // tm,),
        in_specs=[blk, blk, blk, vec, vec, pl.BlockSpec((wb, wb), lambda i: (0, 0))],
        out_specs=blk,
        out_shape=jax.ShapeDtypeStruct((m, wb), F32),
        compiler_params=_cparams(1, 24),
        name="rwkv_post",
    )(y, gg, bonus, lnx_g, lnx_b, seg)


def _block_diag(blocks):
    n, bi, bj = blocks.shape
    eye = jnp.eye(n, dtype=blocks.dtype)
    return jnp.einsum('nij,nm->nimj', blocks, eye).reshape(n * bi, n * bj)


def kernel(x_prompt, x_sample, cache_fox_k, cache_fox_v, cache_fox_logf, cache_diff_k, cache_diff_v, state_lru_conv, state_lru_h, state_rwkv_shift, state_rwkv_wkv, page_table, norm1_g, w_in, fox_fb, lam_q1, lam_k1, lam_q2, lam_k2, diff_subln_g, t5_table, lru_conv_w, lru_conv_b, lru_wa, lru_ba, lru_wx, lru_bx, lru_lambda, rwkv_mu, rwkv_w0, rwkv_w2, rwkv_a0, rwkv_a2, rwkv_g2, rwkv_k_k, rwkv_k_a, rwkv_r_k, rwkv_lnx_g, rwkv_lnx_b, wb_fox, wb_diff, wb_lru, wb_rwkv, w_out, norm2_g, w_up, w_down, final_g):
    B, T, D = x_prompt.shape
    DB, TS, _ = x_sample.shape
    L = w_in.shape[0]
    n_pool, page = cache_fox_k.shape[1], cache_fox_k.shape[2]
    n_pages = page_table.shape[1]
    mp, ms = B * T, DB * TS
    wb = W_BRANCH
    assert TS % SUBLANES == 0 and TS <= page and T % SUBLANES == 0

    x = jnp.concatenate([x_prompt.reshape(mp, D), x_sample.reshape(ms, D)], axis=0)
    kt_fox = cache_fox_k.transpose(0, 1, 3, 4, 2).reshape(L, n_pool, wb, page)
    vt_fox = cache_fox_v.transpose(0, 1, 3, 4, 2).reshape(L, n_pool, wb, page)
    lft_fox = cache_fox_logf.transpose(0, 1, 3, 2)
    kp_diff = cache_diff_k.reshape(L, n_pool, page * H_DIFF, 2 * HEAD_DIM)
    vp_diff = cache_diff_v.reshape(L, n_pool, page * H_DIFF, 2 * HEAD_DIM)
    seg = _block_diag(jnp.ones((H_RWKV, HEAD_DIM, HEAD_DIM), BF16))
    zeros_conv = jnp.zeros((L, B, CONV_W - 1, wb), F32)
    zeros_h = jnp.zeros((L, B, 1, wb), F32)
    zeros_shift = jnp.zeros((L, B, 1, W_RWKV_PROJ), F32)
    zeros_wkv = jnp.zeros((B, H_RWKV, HEAD_DIM, HEAD_DIM), F32)
    eye_f = jnp.eye(H_FOX, dtype=F32)
    eye_2 = jnp.eye(2, dtype=F32)
    row = lambda v: v.reshape(1, -1)

    def pad_rows(z, n):
        return jnp.pad(z, ((0, 0), (0, n - z.shape[1]), (0, 0)))

    def pad_lanes(z):
        return jnp.pad(z, ((0, 0), (0, 0), (0, page - z.shape[2])))

    w_up_b, w_down_b = w_up.astype(BF16), w_down.astype(BF16)
    p_states, s_states = [], []
    for l in range(L):
        lam_init = 0.8 - 0.6 * math.exp(-0.3 * l)
        w = w_in[l]
        n_a = OFF_FF
        split_ff = 3 * wb
        w_r = jnp.concatenate(
            [w[:, :split_ff], w[:, split_ff + H_FOX:split_ff + H_FOX + (n_a - split_ff)],
             w[:, split_ff:split_ff + H_FOX], jnp.zeros((D, FF_PAD), F32),
             w[:, split_ff + H_FOX + (n_a - split_ff):]], axis=1).astype(BF16)
        proj = _rms_matmul(x, row(norm1_g[l]), w_r)
        pp, ps = proj[:mp], proj[mp:]

        lf_p, c_p = _logf_cumsum(pp[:, OFF_FF:OFF_FF + H_FOX].reshape(B, T, H_FOX), row(fox_fb[l]),
                                 jnp.zeros((B, 1, H_FOX), F32))
        lf_s, cn = _logf_cumsum(ps[:, OFF_FF:OFF_FF + H_FOX].reshape(DB, TS, H_FOX), row(fox_fb[l]),
                                jnp.zeros((DB, 1, H_FOX), F32))

        o_fox_p = _fox_flash(proj, c_p, c_p.transpose(0, 2, 1), B, T)

        fq_s = ps[:, OFF_FQ:OFF_FQ + wb].reshape(DB, TS, H_FOX, HEAD_DIM) * QK_SCALE
        qb = jnp.einsum('bthd,hg->bthgd', fq_s, eye_f).reshape(DB, TS * H_FOX, wb).astype(BF16)
        cnb = jnp.broadcast_to(cn.reshape(DB, TS * H_FOX, 1), (DB, TS * H_FOX, page))
        cnt = pad_lanes(cn.transpose(0, 2, 1))
        knt = pad_lanes(ps[:, OFF_FK:OFF_FK + wb].reshape(DB, TS, wb).transpose(0, 2, 1))
        vnt = pad_lanes(ps[:, OFF_FV:OFF_FV + wb].reshape(DB, TS, wb).transpose(0, 2, 1))
        o_fox_s = _fox_decode(page_table, qb, kt_fox, vt_fox, lft_fox, knt, vnt, cnb, cnt, l, TS).reshape(ms, wb)

        lam_vecs = (row(lam_q1[l]), row(lam_k1[l]), row(lam_q2[l]), row(lam_k2[l]))
        g_sub = row(diff_subln_g[l])
        o_diff_p = _diff_flash(t5_table, proj, *lam_vecs, g_sub, lam_init, B, T)

        dq_s = ps[:, OFF_DQ:OFF_DQ + wb].reshape(DB, TS, H_DIFF, 2, HEAD_DIM) * QK_SCALE
        qb_d = jnp.einsum('bthmd,mn->bhmtnd', dq_s, eye_2).reshape(
            DB, 2 * H_DIFF * TS, 2 * HEAD_DIM).astype(BF16)
        kn_d = pad_rows(ps[:, OFF_DK:OFF_DK + wb].reshape(DB, TS * H_DIFF, 2 * HEAD_DIM), page * H_DIFF)
        vn_d = pad_rows(ps[:, OFF_DV:OFF_DV + wb].reshape(DB, TS * H_DIFF, 2 * HEAD_DIM), page * H_DIFF)
        o_diff_s = _diff_decode(page_table, t5_table, qb_d, kp_diff, vp_diff, kn_d, vn_d,
                                *lam_vecs, g_sub, l, TS, lam_init).reshape(ms, wb)

        lru_w = (lru_conv_w[l], row(lru_conv_b[l]), _block_diag(lru_wa[l]).astype(BF16), row(lru_ba[l]),
                 _block_diag(lru_wx[l]).astype(BF16), row(lru_bx[l]), row(lru_lambda[l]))
        o_lru_p, h_p = _lru(proj, 0, B, T, zeros_conv, zeros_h, l, *lru_w)
        o_lru_s, h_s = _lru(proj, mp, DB, TS, state_lru_conv, state_lru_h.reshape(L, DB, 1, wb), l, *lru_w)

        rank = rwkv_w2.shape[1]
        w2p = jnp.concatenate([rwkv_w2[l], jnp.zeros((LANES - rank, wb), F32)], axis=0).astype(BF16)
        a2p = jnp.concatenate([jnp.zeros((LANES - rwkv_a2.shape[1], wb), F32), rwkv_a2[l]], axis=0).astype(BF16)
        rw_w = (row(rwkv_mu[l]), row(rwkv_w0[l]), w2p, row(rwkv_a0[l]), a2p, rwkv_g2[l].astype(BF16),
                row(rwkv_k_k[l]), row(rwkv_k_a[l]), row(rwkv_r_k[l]), seg)
        prep_p = _rwkv_prep(proj, 0, B, T, zeros_shift, l, *rw_w)
        prep_s = _rwkv_prep(proj, mp, DB, TS, state_rwkv_shift.reshape(L, DB, 1, W_RWKV_PROJ), l, *rw_w)
        y_p, wkv_p = _rwkv_scan(*(z.reshape(B, T, wb) for z in prep_p[:6]), zeros_wkv)
        y_s, wkv_s = _rwkv_scan(*(z.reshape(DB, TS, wb) for z in prep_s[:6]), state_rwkv_wkv[l])
        lnx = (row(rwkv_lnx_g[l]), row(rwkv_lnx_b[l]), seg)
        o_rwkv_p = _rwkv_post(y_p.reshape(mp, wb), prep_p[6], prep_p[7], *lnx)
        o_rwkv_s = _rwkv_post(y_s.reshape(ms, wb), prep_s[6], prep_s[7], *lnx)

        wbs = (wb_fox[l].astype(BF16), wb_diff[l].astype(BF16), wb_lru[l].astype(BF16), wb_rwkv[l].astype(BF16))
        mixed = _merge((o_fox_p, o_diff_p, o_lru_p, o_rwkv_p), (o_fox_s, o_diff_s, o_lru_s, o_rwkv_s),
                       wbs, proj, D)
        x = _matmul_res(mixed, w_out[l].astype(BF16), x)
        x = _mlp(x, row(norm2_g[l]), w_up_b, w_down_b, l, row(final_g), normed=(l == L - 1))

        def states(pz, n, t, lf, conv0, h_last, wkv):
            k8 = min(t, SUBLANES)
            cols = slice(OFF_LX, OFF_RW + W_RWKV_PROJ)
            if k8 == t:
                tail = pz[:, cols].reshape(n, t, -1)
            else:
                tail = jnp.stack([pz[(i + 1) * t - k8:(i + 1) * t, cols] for i in range(n)])
            conv_in = jnp.concatenate([conv0, tail[:, :, :wb]], axis=1)
            return (pz[:, OFF_FK:OFF_FK + wb], pz[:, OFF_FV:OFF_FV + wb], lf,
                    pz[:, OFF_DK:OFF_DK + wb], pz[:, OFF_DV:OFF_DV + wb],
                    conv_in[:, -(CONV_W - 1):], h_last.reshape(n, wb),
                    tail[:, -1, OFF_RW - OFF_LX:], wkv)

        p_states.append(states(pp, B, T, lf_p, zeros_conv[0], h_p, wkv_p))
        s_states.append(states(ps, DB, TS, lf_s, state_lru_conv[l], h_s, wkv_s))

    y_prompt = x[:mp].reshape(B, T, D)
    y_sample = x[mp:].reshape(DB, TS, D)
    def stacked(layers, n, t):
        out = [jnp.stack(z) for z in zip(*layers)]
        for i, (h, d) in ((0, (H_FOX, HEAD_DIM)), (1, (H_FOX, HEAD_DIM)),
                          (3, (H_DIFF, 2 * HEAD_DIM)), (4, (H_DIFF, 2 * HEAD_DIM))):
            out[i] = out[i].reshape(L, n, t, h, d)
        return out

    return (y_prompt, y_sample, *stacked(p_states, B, T), *stacked(s_states, DB, TS))
```

```python
import functools
import math

import jax
import jax.numpy as jnp
from jax import lax
from jax.experimental import pallas as pl
from jax.experimental.pallas import tpu as pltpu

F32 = jnp.float32
BF16 = jnp.bfloat16

HEAD_DIM = 64
W_BRANCH = 512
H_FOX = 8
H_DIFF = 4
H_RWKV = 8
CONV_W = 4
LRU_C = 8.0
NUM_BUCKETS = 32
MAX_DISTANCE = 128
RMS_EPS = 1e-6
SUBLN_EPS = 1e-5
GN_EPS = 64e-5
NEG_INF = -1e30
QK_SCALE = HEAD_DIM ** -0.5

LANES = 128
SUBLANES = 8
VMEM_LIMIT_CAP = 56 * 1024 * 1024

OFF_FQ, OFF_FK, OFF_FV = 0, 512, 1024
OFF_DQ, OFF_DK, OFF_DV = 1536, 2048, 2560
OFF_LX, OFF_LG = 3072, 3584
OFF_RW = 4096
W_RWKV_PROJ = 1792
OFF_FF = 5888
OFF_GATE = 6144
FF_PAD = OFF_GATE - OFF_FF - H_FOX


def _tile(n, pref, mult):
    best = None
    t = mult
    while t <= min(n, pref):
        if n % t == 0:
            best = t
        t += mult
    return best if best is not None else n


def _cparams(n_axes, vmem_mb):
    return pltpu.CompilerParams(
        dimension_semantics=("arbitrary",) * n_axes,
        vmem_limit_bytes=min(int(vmem_mb * 1024 * 1024), VMEM_LIMIT_CAP))


def _softplus(x):
    return jnp.maximum(x, 0.0) + jnp.log1p(jnp.exp(-jnp.abs(x)))


def _split3(x):
    def top(v):
        bits = lax.bitcast_convert_type(v, jnp.uint32) & jnp.uint32(0xFFFF0000)
        return lax.bitcast_convert_type(bits, F32)

    x1 = top(x)
    r1 = x - x1
    x2 = top(r1)
    x3 = r1 - x2
    return x1.astype(BF16), x2.astype(BF16), x3.astype(BF16)


def _dot_sel_rhs(x, sel):
    x1, x2, x3 = _split3(x)
    d = functools.partial(jnp.dot, preferred_element_type=F32)
    return d(x1, sel) + d(x2, sel) + d(x3, sel)


def _dot_sel_lhs(sel, x):
    x1, x2, x3 = _split3(x)
    d = functools.partial(jnp.dot, preferred_element_type=F32)
    return d(sel, x1) + d(sel, x2) + d(sel, x3)


def _dot_nt(a, b):
    return lax.dot_general(a, b, (((1,), (1,)), ((), ())), preferred_element_type=F32)


def _rms_matmul_body(x_ref, g_ref, w_ref, o_ref, xn_ref):
    @pl.when(pl.program_id(1) == 0)
    def _():
        x = x_ref[...]
        ms = jnp.mean(x * x, axis=-1, keepdims=True)
        xn_ref[...] = (x * lax.rsqrt(ms + RMS_EPS) * g_ref[...]).astype(BF16)

    o_ref[...] = jnp.dot(xn_ref[...], w_ref[...], preferred_element_type=F32)


def _rms_matmul(x, g, w):
    m, k = x.shape
    n = w.shape[1]
    tm = _tile(m, 1024, SUBLANES)
    tn = _tile(n, 1024, LANES)
    vmem = (2 * tm * k * 4 + 2 * k * tn * 2 + 2 * tm * tn * 4 + tm * k * 2) / 2**20 + 8
    return pl.pallas_call(
        _rms_matmul_body,
        grid=(m // tm, n // tn),
        in_specs=[pl.BlockSpec((tm, k), lambda i, j: (i, 0)),
                  pl.BlockSpec((1, k), lambda i, j: (0, 0)),
                  pl.BlockSpec((k, tn), lambda i, j: (0, j))],
        out_specs=pl.BlockSpec((tm, tn), lambda i, j: (i, j)),
        out_shape=jax.ShapeDtypeStruct((m, n), F32),
        scratch_shapes=[pltpu.VMEM((tm, k), BF16)],
        compiler_params=_cparams(2, vmem),
        name="in_proj",
    )(x, g, w)


def _merge_body(*refs, n_first):
    first, second, ws, gs, o_ref = refs[0:4], refs[4:8], refs[8:12], refs[12:16], refs[16]

    def run(branches):
        acc = None
        for o_r, w_r, g_r in zip(branches, ws, gs):
            y = jax.nn.sigmoid(g_r[...]) * jnp.dot(o_r[...].astype(BF16), w_r[...],
                                                   preferred_element_type=F32)
            acc = y if acc is None else acc + y
        o_ref[...] = acc.astype(BF16)

    @pl.when(pl.program_id(0) < n_first)
    def _():
        run(first)

    @pl.when(pl.program_id(0) >= n_first)
    def _():
        run(second)


def _merge(outs_p, outs_s, wbs, proj, d_model):
    mp, w = outs_p[0].shape
    ms = outs_s[0].shape[0]
    tm = _tile(math.gcd(mp, ms), 512, 16)
    tn = _tile(d_model, 512, LANES)
    n_first = mp // tm
    p_specs = [pl.BlockSpec((tm, w), lambda i, j: (jnp.minimum(i, n_first - 1), 0)) for _ in range(4)]
    s_specs = [pl.BlockSpec((tm, w), lambda i, j: (jnp.maximum(i - n_first, 0), 0)) for _ in range(4)]
    w_specs = [pl.BlockSpec((w, tn), lambda i, j: (0, j)) for _ in range(4)]
    g_specs = [pl.BlockSpec((tm, tn), functools.partial(
        lambda i, j, base: (i, base + j), base=(OFF_GATE + b * d_model) // tn)) for b in range(4)]
    return pl.pallas_call(
        functools.partial(_merge_body, n_first=n_first),
        grid=((mp + ms) // tm, d_model // tn),
        in_specs=p_specs + s_specs + w_specs + g_specs,
        out_specs=pl.BlockSpec((tm, tn), lambda i, j: (i, j)),
        out_shape=jax.ShapeDtypeStruct((mp + ms, d_model), BF16),
        compiler_params=_cparams(2, 40),
        name="merge",
    )(*outs_p, *outs_s, *wbs, proj, proj, proj, proj)


def _matmul_res_body(a_ref, w_ref, r_ref, o_ref):
    o_ref[...] = r_ref[...] + jnp.dot(a_ref[...], w_ref[...], preferred_element_type=F32)


def _matmul_res(a, w, res):
    m, k = a.shape
    n = w.shape[1]
    tm = _tile(m, 512, 16)
    tn = _tile(n, 1024, LANES)
    return pl.pallas_call(
        _matmul_res_body,
        grid=(m // tm, n // tn),
        in_specs=[pl.BlockSpec((tm, k), lambda i, j: (i, 0)),
                  pl.BlockSpec((k, tn), lambda i, j: (0, j)),
                  pl.BlockSpec((tm, tn), lambda i, j: (i, j))],
        out_specs=pl.BlockSpec((tm, tn), lambda i, j: (i, j)),
        out_shape=jax.ShapeDtypeStruct((m, n), F32),
        compiler_params=_cparams(2, 40),
        name="out_proj",
    )(a, w, res)


def _mlp_body(x_ref, g_ref, wu_ref, wd_ref, fg_ref, o_ref, xn_ref, acc_ref, *, n_chunks, normed):
    c = pl.program_id(1)

    @pl.when(c == 0)
    def _():
        x = x_ref[...]
        ms = jnp.mean(x * x, axis=-1, keepdims=True)
        xn_ref[...] = (x * lax.rsqrt(ms + RMS_EPS) * g_ref[...]).astype(BF16)
        acc_ref[...] = jnp.zeros_like(acc_ref)

    h = jnp.dot(xn_ref[...], wu_ref[...], preferred_element_type=F32)
    h = jnp.square(jnp.maximum(h, 0.0))
    acc_ref[...] += jnp.dot(h.astype(BF16), wd_ref[...], preferred_element_type=F32)

    @pl.when(c == n_chunks - 1)
    def _():
        xo = x_ref[...] + acc_ref[...]
        if normed:
            ms = jnp.mean(xo * xo, axis=-1, keepdims=True)
            xo = xo * lax.rsqrt(ms + RMS_EPS) * fg_ref[...]
        o_ref[...] = xo


def _mlp(x, g, wu, wd, layer, final_g, normed):
    m, d = x.shape
    f = wu.shape[2]
    tm = _tile(m, 512, SUBLANES)
    tc = _tile(f, 1024, LANES)
    n_chunks = f // tc
    vmem = (2 * tm * d * 4 + 4 * d * tc * 2 + 2 * tm * d * 4 + tm * d * 2 + tm * d * 4) / 2**20 + 12
    return pl.pallas_call(
        functools.partial(_mlp_body, n_chunks=n_chunks, normed=normed),
        grid=(m // tm, n_chunks),
        in_specs=[pl.BlockSpec((tm, d), lambda i, c: (i, 0)),
                  pl.BlockSpec((1, d), lambda i, c: (0, 0)),
                  pl.BlockSpec((None, d, tc), lambda i, c: (layer, 0, c)),
                  pl.BlockSpec((None, tc, d), lambda i, c: (layer, c, 0)),
                  pl.BlockSpec((1, d), lambda i, c: (0, 0))],
        out_specs=pl.BlockSpec((tm, d), lambda i, c: (i, 0)),
        out_shape=jax.ShapeDtypeStruct((m, d), F32),
        scratch_shapes=[pltpu.VMEM((tm, d), BF16), pltpu.VMEM((tm, d), F32)],
        compiler_params=_cparams(2, vmem),
        name="mlp",
    )(x, g, wu, wd, final_g)


def _cumsum_rows(x, tc):
    if tc >= LANES:
        r = lax.broadcasted_iota(jnp.int32, (tc, tc), 0)
        c = lax.broadcasted_iota(jnp.int32, (tc, tc), 1)
        tri = jnp.where(c <= r, 1.0, 0.0).astype(BF16)
        return _dot_sel_lhs(tri, x)
    rows = lax.broadcasted_iota(jnp.int32, x.shape, 0)
    parts = [jnp.sum(jnp.where(rows <= t, x, 0.0), axis=0, keepdims=True) for t in range(tc)]
    return jnp.concatenate(parts, axis=0)


def _logf_cumsum_body(ff_ref, fb_ref, c0_ref, lf_ref, c_ref, carry_ref, *, tc):
    @pl.when(pl.program_id(1) == 0)
    def _():
        carry_ref[...] = c0_ref[...]

    lf = -_softplus(-(ff_ref[...] + fb_ref[...]))
    lf_ref[...] = lf
    cs = _cumsum_rows(lf, tc) + carry_ref[...]
    c_ref[...] = cs
    carry_ref[...] = cs[tc - 1:tc, :]


def _logf_cumsum(ff, fb, c0):
    nseq, t, h = ff.shape
    tc = _tile(t, 512, SUBLANES)
    blk = pl.BlockSpec((None, tc, h), lambda s, c: (s, c, 0))
    return pl.pallas_call(
        functools.partial(_logf_cumsum_body, tc=tc),
        grid=(nseq, t // tc),
        in_specs=[blk, pl.BlockSpec((1, h), lambda s, c: (0, 0)),
                  pl.BlockSpec((None, 1, h), lambda s, c: (s, 0, 0))],
        out_specs=[blk, blk],
        out_shape=[jax.ShapeDtypeStruct((nseq, t, h), F32)] * 2,
        scratch_shapes=[pltpu.VMEM((1, h), F32)],
        compiler_params=_cparams(2, 24),
        name="logf_cumsum",
    )(ff, fb, c0)


def _t5_bucket(n):
    max_exact = NUM_BUCKETS // 2
    nf = jnp.maximum(n, 1).astype(F32)
    large = max_exact + (jnp.log(nf / max_exact) / math.log(MAX_DISTANCE / max_exact)
                         * (NUM_BUCKETS - max_exact)).astype(jnp.int32)
    return jnp.where(n < max_exact, n, jnp.minimum(large, NUM_BUCKETS - 1))


def _t5_lookup(bucket, tab_ref, col):
    acc = jnp.zeros(bucket.shape, F32)
    for j in range(NUM_BUCKETS):
        acc = jnp.where(bucket == j, tab_ref[j, col], acc)
    return acc


def _softmax_update(s, v, m_ref, l_ref, acc_ref, idx):
    m_prev = m_ref[idx]
    m_new = jnp.maximum(m_prev, jnp.max(s, axis=-1, keepdims=True))
    alpha = jnp.exp(m_prev - m_new)
    p = jnp.exp(s - pltpu.repeat(m_new, s.shape[-1] // LANES, axis=1))
    l_ref[idx] = alpha * l_ref[idx] + jnp.sum(p, axis=-1, keepdims=True)
    acc_ref[idx] = (alpha[:, :v.shape[-1]] * acc_ref[idx]
                    + jnp.dot(p.astype(BF16), v, preferred_element_type=F32))
    m_ref[idx] = m_new


def _fox_flash_body(qm_ref, km_ref, q_ref, k_ref, v_ref, c_ref, ct_ref, o_ref, m_ref, l_ref, acc_ref, *, tq):
    qi = qm_ref[pl.program_id(1)]
    ki = km_ref[pl.program_id(1)]

    @pl.when(ki == 0)
    def _():
        m_ref[...] = jnp.full(m_ref.shape, NEG_INF, F32)
        l_ref[...] = jnp.zeros_like(l_ref)
        acc_ref[...] = jnp.zeros_like(acc_ref)

    first = lax.broadcasted_iota(jnp.int32, (tq, LANES), 1) < HEAD_DIM

    def tiles(masked):
        if masked:
            row = lax.broadcasted_iota(jnp.int32, (tq, tq), 0)
            col = lax.broadcasted_iota(jnp.int32, (tq, tq), 1)
            keep = col <= row
        for g in range(H_FOX // 2):
            grp = slice(g * LANES, (g + 1) * LANES)
            q2 = q_ref[:, grp] * QK_SCALE
            k2 = k_ref[:, grp].astype(BF16)
            v2 = v_ref[:, grp].astype(BF16)
            for j in range(2):
                h = 2 * g + j
                own = first if j == 0 else jnp.logical_not(first)
                s = _dot_nt(jnp.where(own, q2, 0.0).astype(BF16), k2)
                s = s + c_ref[:, h:h + 1] - ct_ref[h:h + 1, :]
                if masked:
                    s = jnp.where(keep, s, NEG_INF)
                m_prev = m_ref[h]
                m_new = jnp.maximum(m_prev, jnp.max(s, axis=-1, keepdims=True))
                alpha = jnp.exp(m_prev - m_new)
                p = jnp.exp(s - pltpu.repeat(m_new, tq // LANES, axis=1))
                l_ref[h] = alpha * l_ref[h] + jnp.sum(p, axis=-1, keepdims=True)
                pv = jnp.dot(p.astype(BF16), v2, preferred_element_type=F32)
                acc_ref[h] = alpha * acc_ref[h] + pv
                m_ref[h] = m_new

    @pl.when(ki < qi)
    def _():
        tiles(False)

    @pl.when(ki == qi)
    def _():
        tiles(True)
        for g in range(H_FOX // 2):
            o_ref[:, g * LANES:(g + 1) * LANES] = jnp.where(
                first, acc_ref[2 * g] / l_ref[2 * g], acc_ref[2 * g + 1] / l_ref[2 * g + 1])


def _qkv_specs(t, tq, col0, w):
    nq = t // tq
    c = col0 // w
    return [pl.BlockSpec((tq, w), lambda bi, s, qm, km: (bi * nq + qm[s], c)),
            pl.BlockSpec((tq, w), lambda bi, s, qm, km: (bi * nq + km[s], c + 1)),
            pl.BlockSpec((tq, w), lambda bi, s, qm, km: (bi * nq + km[s], c + 2))]


def _causal_pairs(nq):
    pairs = [(qi, ki) for qi in range(nq) for ki in range(qi + 1)]
    return (jnp.asarray([p[0] for p in pairs], jnp.int32), jnp.asarray([p[1] for p in pairs], jnp.int32))


def _fox_flash(proj, c, ct, b, t):
    w = W_BRANCH
    tq = _tile(t, 512, LANES)
    nq = t // tq
    qm, km = _causal_pairs(nq)
    grid_spec = pltpu.PrefetchScalarGridSpec(
        num_scalar_prefetch=2,
        grid=(b, qm.shape[0]),
        in_specs=_qkv_specs(t, tq, OFF_FQ, w) + [
            pl.BlockSpec((None, tq, H_FOX), lambda bi, s, qm, km: (bi, qm[s], 0)),
            pl.BlockSpec((None, H_FOX, tq), lambda bi, s, qm, km: (bi, 0, km[s]))],
        out_specs=pl.BlockSpec((tq, w), lambda bi, s, qm, km: (bi * nq + qm[s], 0)),
        scratch_shapes=[pltpu.VMEM((H_FOX, tq, LANES), F32), pltpu.VMEM((H_FOX, tq, LANES), F32),
                        pltpu.VMEM((H_FOX, tq, LANES), F32)])
    return pl.pallas_call(
        functools.partial(_fox_flash_body, tq=tq),
        grid_spec=grid_spec,
        out_shape=jax.ShapeDtypeStruct((b * t, w), F32),
        compiler_params=_cparams(2, 48),
        name="fox_flash",
    )(qm, km, proj, proj, proj, c, ct)


def _lambda(lq1_ref, lk1_ref, lq2_ref, lk2_ref, lam_init):
    s1 = jnp.sum(lq1_ref[...] * lk1_ref[...], axis=-1, keepdims=True)
    s2 = jnp.sum(lq2_ref[...] * lk2_ref[...], axis=-1, keepdims=True)
    return jnp.exp(s1) - jnp.exp(s2) + lam_init


def _subln(o, g, lam_init):
    ms = jnp.mean(o * o, axis=-1, keepdims=True)
    return o * lax.rsqrt(ms + SUBLN_EPS) * g * (1.0 - lam_init)


def _diff_flash_body(qm_ref, km_ref, tab_ref, q_ref, k_ref, v_ref, lq1_ref, lk1_ref, lq2_ref, lk2_ref,
                     g_ref, o_ref, bias_ref, m_ref, l_ref, acc_ref, *, tq, lam_init):
    bi = pl.program_id(0)
    qi = qm_ref[pl.program_id(1)]
    ki = km_ref[pl.program_id(1)]

    @pl.when((bi == 0) & (qi == 0) & (ki == 0))
    def _():
        def fill(rb, carry):
            r0 = pl.multiple_of(rb * SUBLANES, SUBLANES)
            rows = r0 + lax.broadcasted_iota(jnp.int32, (SUBLANES, tq), 0)
            cols = lax.broadcasted_iota(jnp.int32, (SUBLANES, tq), 1)
            for which in range(2):
                bucket = _t5_bucket(jnp.maximum(rows - cols + which * tq, 0))
                for c in range(2 * H_DIFF):
                    bias_ref[c, which, pl.ds(r0, SUBLANES), :] = _t5_lookup(bucket, tab_ref, c)
            return carry
        lax.fori_loop(0, tq // SUBLANES, fill, 0)

    @pl.when(ki == 0)
    def _():
        m_ref[...] = jnp.full(m_ref.shape, NEG_INF, F32)
        l_ref[...] = jnp.zeros_like(l_ref)
        acc_ref[...] = jnp.zeros_like(acc_ref)

    def tiles(mode):
        lane = lax.broadcasted_iota(jnp.int32, (tq, 2 * HEAD_DIM), 1)
        if mode == 0:
            row = lax.broadcasted_iota(jnp.int32, (tq, tq), 0)
            col = lax.broadcasted_iota(jnp.int32, (tq, tq), 1)
            keep = col <= row
        for h in range(H_DIFF):
            grp = slice(h * 2 * HEAD_DIM, (h + 1) * 2 * HEAD_DIM)
            q = q_ref[:, grp] * QK_SCALE
            k = k_ref[:, grp].astype(BF16)
            v = v_ref[:, grp].astype(BF16)
            for mp in range(2):
                c = mp * H_DIFF + h
                qm = jnp.where((lane >= mp * HEAD_DIM) & (lane < (mp + 1) * HEAD_DIM), q, 0.0)
                s = _dot_nt(qm.astype(BF16), k)
                if mode == 0:
                    s = jnp.where(keep, s + bias_ref[c, 0], NEG_INF)
                elif mode == 1:
                    s = s + bias_ref[c, 1]
                else:
                    s = s + tab_ref[NUM_BUCKETS - 1, c]
                _softmax_update(s, v, m_ref, l_ref, acc_ref, c)

    @pl.when(ki < qi - 1)
    def _():
        tiles(2)

    @pl.when(ki == qi - 1)
    def _():
        tiles(1)

    @pl.when(ki == qi)
    def _():
        tiles(0)
        lam = _lambda(lq1_ref, lk1_ref, lq2_ref, lk2_ref, lam_init)
        for h in range(H_DIFF):
            o = acc_ref[h] / l_ref[h] - lam * (acc_ref[H_DIFF + h] / l_ref[H_DIFF + h])
            o_ref[:, h * 2 * HEAD_DIM:(h + 1) * 2 * HEAD_DIM] = _subln(o, g_ref[...], lam_init)


def _diff_flash(table, proj, lq1, lk1, lq2, lk2, g, lam_init, b, t):
    w = W_BRANCH
    d = 2 * HEAD_DIM
    h = H_DIFF
    tq = _tile(t, 512, LANES)
    assert tq >= MAX_DISTANCE or tq == t, "far blocks must lie in the last bucket"
    nq = t // tq
    vec = pl.BlockSpec((1, HEAD_DIM), lambda bi, s, qm, km: (0, 0))
    qm, km = _causal_pairs(nq)
    grid_spec = pltpu.PrefetchScalarGridSpec(
        num_scalar_prefetch=2,
        grid=(b, qm.shape[0]),
        in_specs=[pl.BlockSpec(memory_space=pltpu.SMEM)] + _qkv_specs(t, tq, OFF_DQ, w) + [
            vec, vec, vec, vec, pl.BlockSpec((1, d), lambda bi, s, qm, km: (0, 0))],
        out_specs=pl.BlockSpec((tq, w), lambda bi, s, qm, km: (bi * nq + qm[s], 0)),
        scratch_shapes=[pltpu.VMEM((2 * h, 2, tq, tq), F32),
                        pltpu.VMEM((2 * h, tq, LANES), F32), pltpu.VMEM((2 * h, tq, LANES), F32),
                        pltpu.VMEM((2 * h, tq, d), F32)])
    return pl.pallas_call(
        functools.partial(_diff_flash_body, tq=tq, lam_init=lam_init),
        grid_spec=grid_spec,
        out_shape=jax.ShapeDtypeStruct((b * t, w), F32),
        compiler_params=_cparams(2, 54),
        name="diff_flash",
    )(qm, km, table, proj, proj, proj, lq1, lk1, lq2, lk2, g)


def _softmax_tiles(tiles):
    m = tiles[0][0].max(axis=-1, keepdims=True)
    for s, _ in tiles[1:]:
        m = jnp.maximum(m, s.max(axis=-1, keepdims=True))
    l, acc = None, None
    for s, pv in tiles:
        p = jnp.exp(s - m)
        ps, pa = jnp.sum(p, axis=-1, keepdims=True), pv(p.astype(BF16))
        l, acc = (ps, pa) if l is None else (l + ps, acc + pa)
    return acc / l


def _page_specs(layer, n_pages, rows, cols):
    return [pl.BlockSpec((None, None, rows, cols),
                         functools.partial(lambda b, pt, p: (layer, pt[b, p], 0, 0), p=p))
            for p in range(n_pages)]


def _fox_dec_body(pt_ref, qb_ref, *refs, n_pages, ts):
    kt = refs[:n_pages]
    vt = refs[n_pages:2 * n_pages]
    lf_all_ref, knt_ref, vnt_ref, cnb_ref, cnt_ref, o_ref = refs[2 * n_pages:]
    b = pl.program_id(0)
    lf = [lf_all_ref[pt_ref[b, p]] for p in range(n_pages)]
    rows = H_FOX * ts
    page = knt_ref.shape[-1]
    qb = qb_ref[...]
    qk = lambda k_ref: jnp.dot(qb, k_ref[...].astype(BF16), preferred_element_type=F32)
    pv = lambda v_ref: (lambda p: _dot_nt(p, v_ref[...].astype(BF16)))

    lane = lax.broadcasted_iota(jnp.int32, (ts, H_FOX, page), 2).reshape(rows, page)
    tok = lax.broadcasted_iota(jnp.int32, (ts, H_FOX, page), 0).reshape(rows, page)
    cn_keys = jnp.concatenate([cnt_ref[...]] * ts, axis=0)
    bias_new = jnp.where((lane <= tok) & (lane < ts), cnb_ref[...] - cn_keys, NEG_INF)
    tiles = [(qk(knt_ref) + bias_new, pv(vnt_ref))]

    j_idx = lax.broadcasted_iota(jnp.int32, (page, page), 0)
    k_idx = lax.broadcasted_iota(jnp.int32, (page, page), 1)
    later = jnp.where(j_idx > k_idx, 1.0, 0.0).astype(BF16)
    in_page = _dot_sel_rhs(jnp.concatenate(lf, axis=0), later)
    carry = jnp.zeros((H_FOX, 1), F32)
    for p in reversed(range(n_pages)):
        suffix = in_page[p * H_FOX:(p + 1) * H_FOX] + carry
        carry = suffix[:, 0:1] + lf[p][:, 0:1]
        bias = jnp.concatenate([suffix] * ts, axis=0) + cnb_ref[...]
        tiles.append((qk(kt[p]) + bias, pv(vt[p])))

    accn = _softmax_tiles(tiles).reshape(ts, H_FOX, W_BRANCH)
    col_head = lax.shift_right_logical(lax.broadcasted_iota(jnp.int32, (H_FOX, W_BRANCH), 1),
                                       HEAD_DIM.bit_length() - 1)
    own = col_head == lax.broadcasted_iota(jnp.int32, (H_FOX, W_BRANCH), 0)
    o_ref[...] = jnp.sum(jnp.where(own[None], accn, 0.0), axis=1)


def _fox_decode(page_table, qb, kt, vt, lft, knt, vnt, cnb, cnt, layer, ts):
    db, rows, w = qb.shape
    page = kt.shape[-1]
    n_pages = page_table.shape[1]
    per_seq = lambda r, c: pl.BlockSpec((None, r, c), lambda b, pt: (b, 0, 0))
    grid_spec = pltpu.PrefetchScalarGridSpec(
        num_scalar_prefetch=1,
        grid=(db,),
        in_specs=([per_seq(rows, w)] + _page_specs(layer, n_pages, w, page) * 2
                  + [pl.BlockSpec((None,) + lft.shape[1:], lambda b, pt: (layer, 0, 0, 0),
                                  pipeline_mode=pl.Buffered(1))]
                  + [per_seq(w, page), per_seq(w, page), per_seq(rows, page), per_seq(H_FOX, page)]),
        out_specs=per_seq(ts, w))
    return pl.pallas_call(
        functools.partial(_fox_dec_body, n_pages=n_pages, ts=ts),
        grid_spec=grid_spec,
        out_shape=jax.ShapeDtypeStruct((db, ts, w), F32),
        compiler_params=_cparams(1, 40),
        name="fox_decode",
    )(page_table, qb, *([kt] * n_pages), *([vt] * n_pages), lft, knt, vnt, cnb, cnt)


def _diff_dec_body(pt_ref, tab_ref, q_ref, *refs, n_pages, ts, page, lam_init):
    kp = refs[:n_pages]
    vp = refs[n_pages:2 * n_pages]
    (kn_ref, vn_ref, lq1_ref, lk1_ref, lq2_ref, lk2_ref, g_ref, o_ref, bias_ref) = refs[2 * n_pages:]
    rows = 2 * H_DIFF * ts
    cols = page * H_DIFF
    past = n_pages * page

    @pl.when(pl.program_id(0) == 0)
    def _():
        lane = lax.broadcasted_iota(jnp.int32, (ts, cols), 1)
        pos = lax.shift_right_logical(lane, H_DIFF.bit_length() - 1)
        head = lane & (H_DIFF - 1)
        qpos = past + lax.broadcasted_iota(jnp.int32, (ts, cols), 0)

        def fill(tile, kpos, valid):
            bucket = _t5_bucket(jnp.maximum(qpos - kpos, 0))
            for rb in range(2 * H_DIFF):
                h, mp = rb // 2, rb % 2
                val = _t5_lookup(bucket, tab_ref, mp * H_DIFF + h)
                bias_ref[tile, rb * ts:(rb + 1) * ts, :] = jnp.where(valid & (head == h), val, NEG_INF)

        def past_tile(tile, carry):
            kpos = tile * page + pos
            fill(tile, kpos, kpos <= qpos)
            return carry

        lax.fori_loop(0, n_pages, past_tile, 0)
        kpos = past + pos
        fill(n_pages, kpos, (kpos <= qpos) & (pos < ts))

    q = q_ref[...]
    tiles = []
    for p in range(n_pages + 1):
        k_ref, v_ref = (kp[p], vp[p]) if p < n_pages else (kn_ref, vn_ref)
        s = _dot_nt(q, k_ref[...].astype(BF16)) + bias_ref[p]
        tiles.append((s, lambda pr, v_ref=v_ref: jnp.dot(pr, v_ref[...].astype(BF16),
                                                         preferred_element_type=F32)))

    lam = _lambda(lq1_ref, lk1_ref, lq2_ref, lk2_ref, lam_init)
    on = _softmax_tiles(tiles)
    for h in range(H_DIFF):
        o1 = on[2 * h * ts:(2 * h + 1) * ts, :]
        o2 = on[(2 * h + 1) * ts:(2 * h + 2) * ts, :]
        o_ref[:, h * 2 * HEAD_DIM:(h + 1) * 2 * HEAD_DIM] = _subln(o1 - lam * o2, g_ref[...], lam_init)


def _diff_decode(page_table, table, q, kp, vp, kn, vn, lq1, lk1, lq2, lk2, g, layer, ts, lam_init):
    db, rows, d = q.shape
    cols = kp.shape[2]
    page = cols // H_DIFF
    n_pages = page_table.shape[1]
    per_seq = lambda r, c: pl.BlockSpec((None, r, c), lambda b, pt: (b, 0, 0))
    vec = pl.BlockSpec((1, HEAD_DIM), lambda b, pt: (0, 0))
    grid_spec = pltpu.PrefetchScalarGridSpec(
        num_scalar_prefetch=1,
        grid=(db,),
        in_specs=([pl.BlockSpec(memory_space=pltpu.SMEM), per_seq(rows, d)]
                  + _page_specs(layer, n_pages, cols, d) * 2
                  + [per_seq(cols, d), per_seq(cols, d), vec, vec, vec, vec,
                     pl.BlockSpec((1, d), lambda b, pt: (0, 0))]),
        out_specs=per_seq(ts, H_DIFF * d),
        scratch_shapes=[pltpu.VMEM((n_pages + 1, rows, cols), F32)])
    return pl.pallas_call(
        functools.partial(_diff_dec_body, n_pages=n_pages, ts=ts, page=page, lam_init=lam_init),
        grid_spec=grid_spec,
        out_shape=jax.ShapeDtypeStruct((db, ts, H_DIFF * d), F32),
        compiler_params=_cparams(1, 40),
        name="diff_decode",
    )(page_table, table, q, *([kp] * n_pages), *([vp] * n_pages), kn, vn, lq1, lk1, lq2, lk2, g)


def _lru_body(lx_ref, lg_ref, c0_ref, h0_ref, cw_ref, cb_ref, wa_ref, ba_ref, wx_ref, bx_ref,
              lam_ref, o_ref, hn_ref, cin_ref, a_ref, u_ref, hs_ref, hc_ref, *, tc):
    @pl.when(pl.program_id(1) == 0)
    def _():
        cin_ref[0:SUBLANES, :] = jnp.zeros((SUBLANES, W_BRANCH), F32)
        cin_ref[SUBLANES - (CONV_W - 1):SUBLANES, :] = c0_ref[...]
        hc_ref[...] = h0_ref[...]

    lx = lx_ref[...]
    cin_ref[SUBLANES:SUBLANES + tc, :] = lx
    base = SUBLANES - (CONV_W - 1)
    acc = cw_ref[0:1, :] * cin_ref[base:base + tc, :]
    for j in range(1, CONV_W):
        acc = acc + cw_ref[j:j + 1, :] * cin_ref[base + j:base + j + tc, :]
    xc = cb_ref[...] + acc
    xb = xc.astype(BF16)
    r_gate = jax.nn.sigmoid(jnp.dot(xb, wa_ref[...], preferred_element_type=F32) + ba_ref[...])
    i_gate = jax.nn.sigmoid(jnp.dot(xb, wx_ref[...], preferred_element_type=F32) + bx_ref[...])
    log_a = -LRU_C * r_gate * _softplus(-lam_ref[...])
    a_ref[...] = jnp.exp(log_a)
    u_ref[...] = jnp.sqrt(1.0 - jnp.exp(2.0 * log_a)) * (i_gate * xc)

    def step(t, h):
        h = a_ref[pl.ds(t, 1), :] * h + u_ref[pl.ds(t, 1), :]
        hs_ref[pl.ds(t, 1), :] = h
        return h

    h = lax.fori_loop(0, tc, step, hc_ref[...], unroll=8)
    hc_ref[...] = h
    hn_ref[...] = h
    o_ref[...] = hs_ref[...] * jax.nn.gelu(lg_ref[...])
    cin_ref[0:SUBLANES, :] = lx[tc - SUBLANES:tc, :]


def _lru(proj, row0, nseq, t, conv0, h0, layer, cw, cb, wa, ba, wx, bx, lam):
    m_total = proj.shape[0]
    tc = _tile(t, 512, SUBLANES)
    nc = t // tc
    blk0 = row0 // tc
    col = lambda off: (lambda s, c: (blk0 + s * nc + c, off // W_BRANCH))
    vec = pl.BlockSpec((1, W_BRANCH), lambda s, c: (0, 0))
    mat = pl.BlockSpec((W_BRANCH, W_BRANCH), lambda s, c: (0, 0))
    return pl.pallas_call(
        functools.partial(_lru_body, tc=tc),
        grid=(nseq, nc),
        in_specs=[pl.BlockSpec((tc, W_BRANCH), col(OFF_LX)),
                  pl.BlockSpec((tc, W_BRANCH), col(OFF_LG)),
                  pl.BlockSpec((None, None, CONV_W - 1, W_BRANCH), lambda s, c: (layer, s, 0, 0)),
                  pl.BlockSpec((None, None, 1, W_BRANCH), lambda s, c: (layer, s, 0, 0)),
                  pl.BlockSpec((CONV_W, W_BRANCH), lambda s, c: (0, 0)),
                  vec, mat, vec, mat, vec, vec],
        out_specs=[pl.BlockSpec((tc, W_BRANCH), lambda s, c: (s * nc + c, 0)),
                   pl.BlockSpec((None, 1, W_BRANCH), lambda s, c: (s, 0, 0))],
        out_shape=[jax.ShapeDtypeStruct((nseq * t, W_BRANCH), F32),
                   jax.ShapeDtypeStruct((nseq, 1, W_BRANCH), F32)],
        scratch_shapes=[pltpu.VMEM((tc + SUBLANES, W_BRANCH), F32),
                        pltpu.VMEM((tc, W_BRANCH), F32), pltpu.VMEM((tc, W_BRANCH), F32),
                        pltpu.VMEM((tc, W_BRANCH), F32), pltpu.VMEM((1, W_BRANCH), F32)],
        compiler_params=_cparams(2, 32),
        name="lru",
    )(proj, proj, conv0, h0, cw, cb, wa, ba, wx, bx, lam)


def _rwkv_prep_body(rr_ref, rk_ref, rv_ref, lo_ref, sh_ref, mu_ref, w0_ref, w2_ref, a0_ref, a2_ref,
                    g2_ref, kk_ref, ka_ref, rkk_ref, seg_ref,
                    r_o, w_o, k_o, v_o, a_o, b_o, g_o, bon_o, xs_ref, *, tc):
    wb = W_BRANCH

    @pl.when(pl.program_id(1) == 0)
    def _():
        xs_ref[0:SUBLANES, :] = jnp.zeros((SUBLANES, W_RWKV_PROJ), F32)
        xs_ref[SUBLANES - 1:SUBLANES, :] = sh_ref[...]

    xs_ref[SUBLANES:SUBLANES + tc, 0:wb] = rr_ref[...]
    xs_ref[SUBLANES:SUBLANES + tc, wb:2 * wb] = rk_ref[...]
    xs_ref[SUBLANES:SUBLANES + tc, 2 * wb:3 * wb] = rv_ref[...]
    xs_ref[SUBLANES:SUBLANES + tc, 3 * wb:W_RWKV_PROJ] = lo_ref[...]
    cur = xs_ref[SUBLANES:SUBLANES + tc, :]
    prev = xs_ref[SUBLANES - 1:SUBLANES - 1 + tc, :]
    rx = cur + (prev - cur) * mu_ref[...]
    xs_ref[0:SUBLANES, :] = cur[tc - SUBLANES:tc, :]

    rr = rx[:, 0:wb]
    rk = rx[:, wb:2 * wb]
    rv = rx[:, 2 * wb:3 * wb]
    wl_al = rx[:, 3 * wb:3 * wb + LANES]
    gl = rx[:, 3 * wb + LANES:W_RWKV_PROJ]
    dot = functools.partial(jnp.dot, preferred_element_type=F32)
    wd = -_softplus(-(w0_ref[...] + dot(jnp.tanh(wl_al).astype(BF16), w2_ref[...]))) - 0.5
    decay = jnp.exp(-jnp.exp(wd))
    aa = jax.nn.sigmoid(a0_ref[...] + dot(wl_al.astype(BF16), a2_ref[...]))
    gg = dot(jax.nn.sigmoid(gl).astype(BF16), g2_ref[...])
    kk = rk * kk_ref[...]
    norm = jnp.sqrt(_dot_sel_rhs(kk * kk, seg_ref[...]))
    kk = kk / jnp.maximum(norm, 1e-12)
    kh = rk * (1.0 + (aa - 1.0) * ka_ref[...])
    r_o[...] = rr
    w_o[...] = decay
    k_o[...] = kh
    v_o[...] = rv
    a_o[...] = -kk
    b_o[...] = kk * aa
    g_o[...] = gg
    bon_o[...] = _dot_sel_rhs(rr * kh * rkk_ref[...], seg_ref[...]) * rv


def _rwkv_prep(proj, row0, nseq, t, shift0, layer, mu, w0, w2p, a0, a2p, g2, k_k, k_a, r_k, seg):
    tc = _tile(t, 256, SUBLANES)
    nc = t // tc
    blk0 = row0 // tc
    wb = W_BRANCH
    col = lambda off, w: (lambda s, c: (blk0 + s * nc + c, off // w))
    vec = pl.BlockSpec((1, wb), lambda s, c: (0, 0))
    out_blk = pl.BlockSpec((tc, wb), lambda s, c: (s * nc + c, 0))
    lo_w = W_RWKV_PROJ - 3 * wb
    return pl.pallas_call(
        functools.partial(_rwkv_prep_body, tc=tc),
        grid=(nseq, nc),
        in_specs=[pl.BlockSpec((tc, wb), col(OFF_RW, wb)),
                  pl.BlockSpec((tc, wb), col(OFF_RW + wb, wb)),
                  pl.BlockSpec((tc, wb), col(OFF_RW + 2 * wb, wb)),
                  pl.BlockSpec((tc, lo_w), col(OFF_RW + 3 * wb, lo_w)),
                  pl.BlockSpec((None, None, 1, W_RWKV_PROJ), lambda s, c: (layer, s, 0, 0)),
                  pl.BlockSpec((1, W_RWKV_PROJ), lambda s, c: (0, 0)),
                  vec, pl.BlockSpec((LANES, wb), lambda s, c: (0, 0)),
                  vec, pl.BlockSpec((LANES, wb), lambda s, c: (0, 0)),
                  pl.BlockSpec((LANES, wb), lambda s, c: (0, 0)),
                  vec, vec, vec, pl.BlockSpec((wb, wb), lambda s, c: (0, 0))],
        out_specs=[out_blk] * 8,
        out_shape=[jax.ShapeDtypeStruct((nseq * t, wb), F32)] * 8,
        scratch_shapes=[pltpu.VMEM((tc + SUBLANES, W_RWKV_PROJ), F32)],
        compiler_params=_cparams(2, 40),
        name="rwkv_prep",
    )(proj, proj, proj, proj, shift0, mu, w0, w2p, a0, a2p, g2, k_k, k_a, r_k, seg)


def _rwkv_scan_body(r_ref, w_ref, k_ref, v_ref, a_ref, b_ref, s0_ref, y_ref, sn_ref, st_ref,
                    *, nb, tc, n_chunks):
    c = pl.program_id(1)

    @pl.when(c == 0)
    def _():
        st_ref[...] = s0_ref[...]

    n_pairs = H_RWKV // 2
    shape = (HEAD_DIM, LANES)
    lane = lax.broadcasted_iota(jnp.int32, shape, 1)
    first = lane < HEAD_DIM
    eye2 = jnp.where((lane & (HEAD_DIM - 1)) == lax.broadcasted_iota(jnp.int32, shape, 0), 1.0, 0.0)
    half = lambda d: lax.shift_right_logical(lax.broadcasted_iota(jnp.int32, (LANES, LANES), d),
                                             HEAD_DIM.bit_length() - 1)
    ones_bd = jnp.where(half(0) == half(1), 1.0, 0.0).astype(BF16)

    def step(t, states):
        new_states = []
        for s in range(nb):
            rows = [ref[s, pl.ds(t, 1), :] for ref in (r_ref, w_ref, k_ref, v_ref, a_ref, b_ref)]
            y_parts = []
            for p in range(n_pairs):
                r_t, w_t, k_t, v_t, a_t, b_t = (z[:, p * LANES:(p + 1) * LANES] for z in rows)
                st = states[s * n_pairs + p]
                sa = jnp.dot((st * a_t).astype(BF16), ones_bd, preferred_element_type=F32)
                d = eye2 * v_t
                v1 = jnp.sum(jnp.where(first, d, 0.0), axis=1, keepdims=True)
                v2 = jnp.sum(jnp.where(first, 0.0, d), axis=1, keepdims=True)
                st = st * w_t + sa * b_t + jnp.where(first, v1, v2) * k_t
                y = jnp.dot((st * r_t).astype(BF16), ones_bd, preferred_element_type=F32)
                y_parts.append(jnp.sum(eye2 * y, axis=0, keepdims=True))
                new_states.append(st)
            y_ref[s, pl.ds(t, 1), :] = jnp.concatenate(y_parts, axis=1)
        return tuple(new_states)

    init = tuple(st_ref[s, p] for s in range(nb) for p in range(n_pairs))
    final = lax.fori_loop(0, tc, step, init, unroll=64)
    for i, st in enumerate(final):
        st_ref[i // n_pairs, i % n_pairs] = st

    @pl.when(c == n_chunks - 1)
    def _():
        sn_ref[...] = st_ref[...]


def _pack_pairs(s):
    n = s.shape[0]
    s = s.reshape(n, H_RWKV // 2, 2, HEAD_DIM, HEAD_DIM).transpose(0, 1, 3, 2, 4)
    return s.reshape(n, H_RWKV // 2, HEAD_DIM, 2 * HEAD_DIM)


def _unpack_pairs(s):
    n = s.shape[0]
    s = s.reshape(n, H_RWKV // 2, HEAD_DIM, 2, HEAD_DIM).transpose(0, 1, 3, 2, 4)
    return s.reshape(n, H_RWKV, HEAD_DIM, HEAD_DIM)


def _rwkv_scan(r, w, k, v, a, b, s0):
    nseq, t, wd = r.shape
    nb = 2 if nseq % 2 == 0 else 1
    tc = _tile(t, LANES, SUBLANES)
    n_chunks = t // tc
    n_pairs = H_RWKV // 2
    blk = pl.BlockSpec((nb, tc, wd), lambda s, c: (s, c, 0))
    st_blk = pl.BlockSpec((nb, n_pairs, HEAD_DIM, LANES), lambda s, c: (s, 0, 0, 0))
    y, sn = pl.pallas_call(
        functools.partial(_rwkv_scan_body, nb=nb, tc=tc, n_chunks=n_chunks),
        grid=(nseq // nb, n_chunks),
        in_specs=[blk] * 6 + [st_blk],
        out_specs=[blk, st_blk],
        out_shape=[jax.ShapeDtypeStruct((nseq, t, wd), F32),
                   jax.ShapeDtypeStruct((nseq, n_pairs, HEAD_DIM, LANES), F32)],
        scratch_shapes=[pltpu.VMEM((nb, n_pairs, HEAD_DIM, LANES), F32)],
        compiler_params=_cparams(2, 40),
        name="rwkv_scan",
    )(r, w, k, v, a, b, _pack_pairs(s0))
    return y, _unpack_pairs(sn)


def _rwkv_post_body(y_ref, g_ref, bon_ref, lg_ref, lb_ref, seg_ref, o_ref):
    y = y_ref[...]
    inv_n = 1.0 / HEAD_DIM
    mu = _dot_sel_rhs(y, seg_ref[...]) * inv_n
    d = y - mu
    var = _dot_sel_rhs(d * d, seg_ref[...]) * inv_n
    yn = d * lax.rsqrt(var + GN_EPS) * lg_ref[...] + lb_ref[...]
    o_ref[...] = (yn + bon_ref[...]) * g_ref[...]


def _rwkv_post(y, gg, bonus, lnx_g, lnx_b, seg):
    m, wb = y.shape
    tm = _tile(m, 512, SUBLANES)
    blk = pl.BlockSpec((tm, wb), lambda i: (i, 0))
    vec = pl.BlockSpec((1, wb), lambda i: (0, 0))
    return pl.pallas_call(
        _rwkv_post_body,
        grid=(m // tm,),
        in_specs=[blk, blk, blk, vec, vec, pl.BlockSpec((wb, wb), lambda i: (0, 0))],
        out_specs=blk,
        out_shape=jax.ShapeDtypeStruct((m, wb), F32),
        compiler_params=_cparams(1, 24),
        name="rwkv_post",
    )(y, gg, bonus, lnx_g, lnx_b, seg)


def _block_diag(blocks):
    n, bi, bj = blocks.shape
    eye = jnp.eye(n, dtype=blocks.dtype)
    return jnp.einsum('nij,nm->nimj', blocks, eye).reshape(n * bi, n * bj)


def kernel(x_prompt, x_sample, cache_fox_k, cache_fox_v, cache_fox_logf, cache_diff_k, cache_diff_v, state_lru_conv, state_lru_h, state_rwkv_shift, state_rwkv_wkv, page_table, norm1_g, w_in, fox_fb, lam_q1, lam_k1, lam_q2, lam_k2, diff_subln_g, t5_table, lru_conv_w, lru_conv_b, lru_wa, lru_ba, lru_wx, lru_bx, lru_lambda, rwkv_mu, rwkv_w0, rwkv_w2, rwkv_a0, rwkv_a2, rwkv_g2, rwkv_k_k, rwkv_k_a, rwkv_r_k, rwkv_lnx_g, rwkv_lnx_b, wb_fox, wb_diff, wb_lru, wb_rwkv, w_out, norm2_g, w_up, w_down, final_g):
    B, T, D = x_prompt.shape
    DB, TS, _ = x_sample.shape
    L = w_in.shape[0]
    n_pool, page = cache_fox_k.shape[1], cache_fox_k.shape[2]
    n_pages = page_table.shape[1]
    mp, ms = B * T, DB * TS
    wb = W_BRANCH
    assert TS % SUBLANES == 0 and TS <= page and T % SUBLANES == 0

    x = jnp.concatenate([x_prompt.reshape(mp, D), x_sample.reshape(ms, D)], axis=0)
    kt_fox = cache_fox_k.transpose(0, 1, 3, 4, 2).reshape(L, n_pool, wb, page)
    vt_fox = cache_fox_v.transpose(0, 1, 3, 4, 2).reshape(L, n_pool, wb, page)
    lft_fox = cache_fox_logf.transpose(0, 1, 3, 2)
    kp_diff = cache_diff_k.reshape(L, n_pool, page * H_DIFF, 2 * HEAD_DIM)
    vp_diff = cache_diff_v.reshape(L, n_pool, page * H_DIFF, 2 * HEAD_DIM)
    seg = _block_diag(jnp.ones((H_RWKV, HEAD_DIM, HEAD_DIM), BF16))
    zeros_conv = jnp.zeros((L, B, CONV_W - 1, wb), F32)
    zeros_h = jnp.zeros((L, B, 1, wb), F32)
    zeros_shift = jnp.zeros((L, B, 1, W_RWKV_PROJ), F32)
    zeros_wkv = jnp.zeros((B, H_RWKV, HEAD_DIM, HEAD_DIM), F32)
    eye_f = jnp.eye(H_FOX, dtype=F32)
    eye_2 = jnp.eye(2, dtype=F32)
    row = lambda v: v.reshape(1, -1)

    def pad_rows(z, n):
        return jnp.pad(z, ((0, 0), (0, n - z.shape[1]), (0, 0)))

    def pad_lanes(z):
        return jnp.pad(z, ((0, 0), (0, 0), (0, page - z.shape[2])))

    w_up_b, w_down_b = w_up.astype(BF16), w_down.astype(BF16)
    p_states, s_states = [], []
    for l in range(L):
        lam_init = 0.8 - 0.6 * math.exp(-0.3 * l)
        w = w_in[l]
        n_a = OFF_FF
        split_ff = 3 * wb
        w_r = jnp.concatenate(
            [w[:, :split_ff], w[:, split_ff + H_FOX:split_ff + H_FOX + (n_a - split_ff)],
             w[:, split_ff:split_ff + H_FOX], jnp.zeros((D, FF_PAD), F32),
             w[:, split_ff + H_FOX + (n_a - split_ff):]], axis=1).astype(BF16)
        proj = _rms_matmul(x, row(norm1_g[l]), w_r)
        pp, ps = proj[:mp], proj[mp:]

        lf_p, c_p = _logf_cumsum(pp[:, OFF_FF:OFF_FF + H_FOX].reshape(B, T, H_FOX), row(fox_fb[l]),
                                 jnp.zeros((B, 1, H_FOX), F32))
        lf_s, cn = _logf_cumsum(ps[:, OFF_FF:OFF_FF + H_FOX].reshape(DB, TS, H_FOX), row(fox_fb[l]),
                                jnp.zeros((DB, 1, H_FOX), F32))

        o_fox_p = _fox_flash(proj, c_p, c_p.transpose(0, 2, 1), B, T)

        fq_s = ps[:, OFF_FQ:OFF_FQ + wb].reshape(DB, TS, H_FOX, HEAD_DIM) * QK_SCALE
        qb = jnp.einsum('bthd,hg->bthgd', fq_s, eye_f).reshape(DB, TS * H_FOX, wb).astype(BF16)
        cnb = jnp.broadcast_to(cn.reshape(DB, TS * H_FOX, 1), (DB, TS * H_FOX, page))
        cnt = pad_lanes(cn.transpose(0, 2, 1))
        knt = pad_lanes(ps[:, OFF_FK:OFF_FK + wb].reshape(DB, TS, wb).transpose(0, 2, 1))
        vnt = pad_lanes(ps[:, OFF_FV:OFF_FV + wb].reshape(DB, TS, wb).transpose(0, 2, 1))
        o_fox_s = _fox_decode(page_table, qb, kt_fox, vt_fox, lft_fox, knt, vnt, cnb, cnt, l, TS).reshape(ms, wb)

        lam_vecs = (row(lam_q1[l]), row(lam_k1[l]), row(lam_q2[l]), row(lam_k2[l]))
        g_sub = row(diff_subln_g[l])
        o_diff_p = _diff_flash(t5_table, proj, *lam_vecs, g_sub, lam_init, B, T)

        dq_s = ps[:, OFF_DQ:OFF_DQ + wb].reshape(DB, TS, H_DIFF, 2, HEAD_DIM) * QK_SCALE
        qb_d = jnp.einsum('bthmd,mn->bhmtnd', dq_s, eye_2).reshape(
            DB, 2 * H_DIFF * TS, 2 * HEAD_DIM).astype(BF16)
        kn_d = pad_rows(ps[:, OFF_DK:OFF_DK + wb].reshape(DB, TS * H_DIFF, 2 * HEAD_DIM), page * H_DIFF)
        vn_d = pad_rows(ps[:, OFF_DV:OFF_DV + wb].reshape(DB, TS * H_DIFF, 2 * HEAD_DIM), page * H_DIFF)
        o_diff_s = _diff_decode(page_table, t5_table, qb_d, kp_diff, vp_diff, kn_d, vn_d,
                                *lam_vecs, g_sub, l, TS, lam_init).reshape(ms, wb)

        lru_w = (lru_conv_w[l], row(lru_conv_b[l]), _block_diag(lru_wa[l]).astype(BF16), row(lru_ba[l]),
                 _block_diag(lru_wx[l]).astype(BF16), row(lru_bx[l]), row(lru_lambda[l]))
        o_lru_p, h_p = _lru(proj, 0, B, T, zeros_conv, zeros_h, l, *lru_w)
        o_lru_s, h_s = _lru(proj, mp, DB, TS, state_lru_conv, state_lru_h.reshape(L, DB, 1, wb), l, *lru_w)

        rank = rwkv_w2.shape[1]
        w2p = jnp.concatenate([rwkv_w2[l], jnp.zeros((LANES - rank, wb), F32)], axis=0).astype(BF16)
        a2p = jnp.concatenate([jnp.zeros((LANES - rwkv_a2.shape[1], wb), F32), rwkv_a2[l]], axis=0).astype(BF16)
        rw_w = (row(rwkv_mu[l]), row(rwkv_w0[l]), w2p, row(rwkv_a0[l]), a2p, rwkv_g2[l].astype(BF16),
                row(rwkv_k_k[l]), row(rwkv_k_a[l]), row(rwkv_r_k[l]), seg)
        prep_p = _rwkv_prep(proj, 0, B, T, zeros_shift, l, *rw_w)
        prep_s = _rwkv_prep(proj, mp, DB, TS, state_rwkv_shift.reshape(L, DB, 1, W_RWKV_PROJ), l, *rw_w)
        y_p, wkv_p = _rwkv_scan(*(z.reshape(B, T, wb) for z in prep_p[:6]), zeros_wkv)
        y_s, wkv_s = _rwkv_scan(*(z.reshape(DB, TS, wb) for z in prep_s[:6]), state_rwkv_wkv[l])
        lnx = (row(rwkv_lnx_g[l]), row(rwkv_lnx_b[l]), seg)
        o_rwkv_p = _rwkv_post(y_p.reshape(mp, wb), prep_p[6], prep_p[7], *lnx)
        o_rwkv_s = _rwkv_post(y_s.reshape(ms, wb), prep_s[6], prep_s[7], *lnx)

        wbs = (wb_fox[l].astype(BF16), wb_diff[l].astype(BF16), wb_lru[l].astype(BF16), wb_rwkv[l].astype(BF16))
        mixed = _merge((o_fox_p, o_diff_p, o_lru_p, o_rwkv_p), (o_fox_s, o_diff_s, o_lru_s, o_rwkv_s),
                       wbs, proj, D)
        x = _matmul_res(mixed, w_out[l].astype(BF16), x)
        x = _mlp(x, row(norm2_g[l]), w_up_b, w_down_b, l, row(final_g), normed=(l == L - 1))

        def states(pz, n, t, lf, conv0, h_last, wkv):
            k8 = min(t, SUBLANES)
            cols = slice(OFF_LX, OFF_RW + W_RWKV_PROJ)
            if k8 == t:
                tail = pz[:, cols].reshape(n, t, -1)
            else:
                tail = jnp.stack([pz[(i + 1) * t - k8:(i + 1) * t, cols] for i in range(n)])
            conv_in = jnp.concatenate([conv0, tail[:, :, :wb]], axis=1)
            return (pz[:, OFF_FK:OFF_FK + wb], pz[:, OFF_FV:OFF_FV + wb], lf,
                    pz[:, OFF_DK:OFF_DK + wb], pz[:, OFF_DV:OFF_DV + wb],
                    conv_in[:, -(CONV_W - 1):], h_last.reshape(n, wb),
                    tail[:, -1, OFF_RW - OFF_LX:], wkv)

        p_states.append(states(pp, B, T, lf_p, zeros_conv[0], h_p, wkv_p))
        s_states.append(states(ps, DB, TS, lf_s, state_lru_conv[l], h_s, wkv_s))

    y_prompt = x[:mp].reshape(B, T, D)
    y_sample = x[mp:].reshape(DB, TS, D)
    def stacked(layers, n, t):
        out = [jnp.stack(z) for z in zip(*layers)]
        for i, (h, d) in ((0, (H_FOX, HEAD_DIM)), (1, (H_FOX, HEAD_DIM)),
                          (3, (H_DIFF, 2 * HEAD_DIM)), (4, (H_DIFF, 2 * HEAD_DIM))):
            out[i] = out[i].reshape(L, n, t, h, d)
        return out

    return (y_prompt, y_sample, *stacked(p_states, B, T), *stacked(s_states, DB, TS))
```
